```python
import math
import jax
import jax.numpy as jnp
from jax import lax
import numpy as np

D_MODEL = 4096
BATCH = 1
SEQ = 8192
DEPTH = 2
DEC_BATCH = 16
DEC_SEQ = 64
PAST_LEN = 2048

CHUNK = 64
Q_BLOCK = 128
HEAD_DIM = 128
H_A = D_MODEL // (2 * HEAD_DIM)
KV_A = H_A // 4
H_IDX = 16
D_IDX = 64
TOPK_MAX = 256
H_B = D_MODEL // (4 * HEAD_DIM)
H_C = D_MODEL // (2 * HEAD_DIM)
H_D = D_MODEL // (2 * HEAD_DIM)
BAND_CHUNKS = 8
REL_CLIP = 128
T5_BUCKETS = 32
T5_MAX_DIST = 128
PL_DIM = 256
RMS_EPS = 1e-6
N_EVEN = (DEPTH + 1) // 2
N_ODD = DEPTH // 2
EVEN_SIZES = (H_A * HEAD_DIM, KV_A * HEAD_DIM, KV_A * HEAD_DIM, H_IDX * D_IDX, D_IDX, H_IDX, H_A * HEAD_DIM,
              H_B * 2 * HEAD_DIM, H_B * 2 * HEAD_DIM, H_B * 2 * HEAD_DIM, H_B * 2 * HEAD_DIM)
ODD_SIZES = (H_C * HEAD_DIM,) * 4 + (H_D * HEAD_DIM,) * 4
MIX_EVEN = H_A * HEAD_DIM + H_B * 2 * HEAD_DIM
MIX_ODD = (H_C + H_D) * HEAD_DIM

kernel_name = "hybrid_chunk_stream_encoder_step"


def rms_norm(x, g):
    xf = x.astype(jnp.float32)
    y = xf * lax.rsqrt(jnp.mean(xf * xf, axis=-1, keepdims=True) + RMS_EPS)
    return (y * g.astype(jnp.float32)).astype(x.dtype)


def split_cols(y, sizes):
    return jnp.split(y, np.cumsum(sizes)[:-1].tolist(), axis=-1)


def t5_bucket(rel):
    half = T5_BUCKETS // 2
    max_exact = half // 2
    n = jnp.abs(rel)
    nf = jnp.maximum(n, 1).astype(jnp.float32)
    large = max_exact + (jnp.log(nf / max_exact) / math.log(T5_MAX_DIST / max_exact)
                         * (half - max_exact)).astype(jnp.int32)
    large = jnp.minimum(large, half - 1)
    return jnp.where(rel < 0, half, 0) + jnp.where(n < max_exact, n, large)


def chunk_visible(qpos, kpos):
    return (kpos // CHUNK) <= (qpos // CHUNK)


def query_blocks(fn, *xs):
    n = xs[0].shape[1]
    if n <= Q_BLOCK:
        return fn(*xs)
    nb = n // Q_BLOCK
    blocks = tuple(x.reshape(x.shape[0], nb, Q_BLOCK, *x.shape[2:]).swapaxes(0, 1) for x in xs)
    out = lax.map(lambda b: fn(*b), blocks)
    return out.swapaxes(0, 1).reshape(out.shape[1], n, *out.shape[3:])


def dsa_core(q, qi, wi, qpos, k, v, ki, kpos, t5_tab, n_sel):
    B, Tq = q.shape[:2]
    s = jnp.einsum("bqhd,bsd->bqhs", qi.astype(jnp.float32), ki.astype(jnp.float32)) * (D_IDX ** -0.5)
    score = jnp.einsum("bqhs,bqh->bqs", jax.nn.relu(s), wi.astype(jnp.float32)) * (H_IDX ** -0.5)
    score = jnp.where(chunk_visible(qpos[:, None], kpos[None, :]), score, -jnp.inf)
    _, idx = lax.top_k(score, n_sel)
    sel_pos = kpos[idx]
    valid = chunk_visible(qpos[None, :, None], sel_pos)
    gather = jax.vmap(lambda a, i: a[i])
    ks = gather(k, idx)
    vs = gather(v, idx)
    G = H_A // KV_A
    qg = q.reshape(B, Tq, KV_A, G, HEAD_DIM)
    logits = jnp.einsum("bqngd,bqknd->bqngk", qg, ks).astype(jnp.float32) * (HEAD_DIM ** -0.5)
    bias = t5_tab[:, :H_A][t5_bucket(qpos[None, :, None] - sel_pos)]
    bias = jnp.moveaxis(bias, -1, 2).reshape(B, Tq, KV_A, G, n_sel)
    logits = jnp.where(valid[:, :, None, None, :], logits + bias, -jnp.inf)
    p = jax.nn.softmax(logits, axis=-1).astype(v.dtype)
    o = jnp.einsum("bqngk,bqknd->bqngd", p, vs)
    return o.reshape(B, Tq, H_A * HEAD_DIM)


def diff_core(q, qpos, k, v, kpos, t5_tab, lam, lam_init, subln_g):
    B, Tq = q.shape[:2]
    logits = jnp.einsum("bqhcd,bshcd->bhcqs", q, k).astype(jnp.float32) * (HEAD_DIM ** -0.5)
    bias = t5_tab[:, H_A:][t5_bucket(qpos[:, None] - kpos[None, :])]
    bias = jnp.transpose(bias, (2, 0, 1))[None, :, None]
    vis = chunk_visible(qpos[:, None], kpos[None, :])
    a = jax.nn.softmax(jnp.where(vis, logits + bias, -jnp.inf), axis=-1)
    attn = (a[:, :, 0] - lam * a[:, :, 1]).astype(v.dtype)
    o = jnp.einsum("bhqs,bshe->bqhe", attn, v)
    o = rms_norm(o, subln_g) * (1.0 - lam_init)
    return o.reshape(B, Tq, H_B * 2 * HEAD_DIM)


def stick_core(q, qpos, k, v, kpos):
    B, Tq = q.shape[:2]
    z = jnp.einsum("bqhd,bshd->bhqs", q, k).astype(jnp.float32) * (HEAD_DIM ** -0.5)
    before = kpos[None, :] < qpos[:, None]
    log_fail = jnp.where(before, jax.nn.log_sigmoid(-z), 0.0)
    later = lax.cumsum(log_fail, axis=3, reverse=True) - log_fail
    w = jnp.where(before, jnp.exp(jax.nn.log_sigmoid(z) + later), 0.0).astype(v.dtype)
    o = jnp.einsum("bhqs,bshd->bqhd", w, v)
    return o.reshape(B, Tq, H_C * HEAD_DIM)


def band_core(q, qpos, k, v, kpos, rel_tab):
    B, Tq = q.shape[:2]
    logits = jnp.einsum("bqhd,bshd->bhqs", q, k).astype(jnp.float32) * (HEAD_DIM ** -0.5)
    rel = jnp.clip(qpos[:, None] - kpos[None, :], -REL_CLIP, REL_CLIP) + REL_CLIP
    bias = jnp.transpose(rel_tab[rel], (2, 0, 1))
    qc = qpos[:, None] // CHUNK
    kc = kpos[None, :] // CHUNK
    vis = (kpos[None, :] >= 0) & (kc <= qc) & (kc >= qc - BAND_CHUNKS)
    p = jax.nn.softmax(jnp.where(vis, logits + bias, -jnp.inf), axis=-1).astype(v.dtype)
    o = jnp.einsum("bhqs,bshd->bqhd", p, v)
    return o.reshape(B, Tq, H_D * HEAD_DIM)


def band_prompt(q, k, v, rel_tab):
    B, T = q.shape[:2]
    pad = BAND_CHUNKS * CHUNK
    band = (BAND_CHUNKS + 1) * CHUNK
    kp = jnp.pad(k, ((0, 0), (pad, 0), (0, 0), (0, 0)))
    vp = jnp.pad(v, ((0, 0), (pad, 0), (0, 0), (0, 0)))

    def one_chunk(c):
        start = c * CHUNK
        qc = lax.dynamic_slice_in_dim(q, start, CHUNK, axis=1)
        kc = lax.dynamic_slice_in_dim(kp, start, band, axis=1)
        vc = lax.dynamic_slice_in_dim(vp, start, band, axis=1)
        qpos = start + jnp.arange(CHUNK)
        kpos = start - pad + jnp.arange(band)
        return band_core(qc, qpos, kc, vc, kpos, rel_tab)

    out = lax.map(one_chunk, jnp.arange(T // CHUNK))
    return out.swapaxes(0, 1).reshape(B, T, out.shape[-1])


def even_layer(h, past, w_in, w_out, t5_tab, lam_vec, subln_g, lam_init):
    B, T, _ = h.shape
    aq, ak, av, aqi, aki, aw, ag, bq, bk, bv, bg = split_cols(h @ w_in, EVEN_SIZES)
    aq = aq.reshape(B, T, H_A, HEAD_DIM)
    aqi = aqi.reshape(B, T, H_IDX, D_IDX)
    bq = bq.reshape(B, T, H_B, 2, HEAD_DIM)
    new_a = jnp.stack([ak.reshape(B, T, KV_A, HEAD_DIM), av.reshape(B, T, KV_A, HEAD_DIM)], axis=2)
    new_b = jnp.stack([bk.reshape(B, T, H_B, 2 * HEAD_DIM), bv.reshape(B, T, H_B, 2 * HEAD_DIM)], axis=2)
    if past is None:
        p0 = 0
        full_a, full_ki, full_b = new_a, aki, new_b
    else:
        past_a, past_ki, past_b = past
        p0 = past_a.shape[1]
        full_a = jnp.concatenate([past_a, new_a], axis=1)
        full_ki = jnp.concatenate([past_ki, aki], axis=1)
        full_b = jnp.concatenate([past_b, new_b], axis=1)
    L = p0 + T
    kpos = jnp.arange(L)
    qpos = p0 + jnp.arange(T)
    n_sel = min(TOPK_MAX, L // 4)
    ka, va = full_a[:, :, 0], full_a[:, :, 1]
    kb = full_b[:, :, 0].reshape(B, L, H_B, 2, HEAD_DIM)
    vb = full_b[:, :, 1]
    lq1, lk1, lq2, lk2 = lam_vec.astype(jnp.float32)
    lam = jnp.exp(jnp.sum(lq1 * lk1)) - jnp.exp(jnp.sum(lq2 * lk2)) + lam_init
    o_a = query_blocks(lambda q, qi, wi, qp: dsa_core(q, qi, wi, qp[0], ka, va, full_ki, kpos, t5_tab, n_sel),
                       aq, aqi, aw, qpos[None])
    o_b = query_blocks(lambda q, qp: diff_core(q, qp[0], kb, vb, kpos, t5_tab, lam, lam_init, subln_g),
                       bq, qpos[None])
    mixed = jnp.concatenate([o_a * jax.nn.silu(ag), o_b * jax.nn.silu(bg)], axis=-1)
    return mixed @ w_out, new_a, aki, new_b


def odd_layer(h, past, w_in, w_out, rel_tab):
    B, T, _ = h.shape
    cq, ck, cv, cg, dq, dk, dv, dg = split_cols(h @ w_in, ODD_SIZES)
    cq, ck, cv = (t.reshape(B, T, H_C, HEAD_DIM) for t in (cq, ck, cv))
    dq, dk, dv = (t.reshape(B, T, H_D, HEAD_DIM) for t in (dq, dk, dv))
    new_c = jnp.stack([ck, cv], axis=2)
    new_d = jnp.stack([dk, dv], axis=2)
    if past is None:
        kpos = jnp.arange(T)
        o_c = query_blocks(lambda q, qp: stick_core(q, qp[0], ck, cv, kpos), cq, kpos[None])
        o_d = band_prompt(dq, dk, dv, rel_tab)
        d_state = new_d[:, T - min(BAND_CHUNKS * CHUNK, T):]
    else:
        past_c, past_d = past
        p0 = past_c.shape[1]
        win = past_d.shape[1]
        full_c = jnp.concatenate([past_c, new_c], axis=1)
        kpos = jnp.arange(p0 + T)
        qpos = p0 + jnp.arange(T)
        kc, vc = full_c[:, :, 0], full_c[:, :, 1]
        o_c = query_blocks(lambda q, qp: stick_core(q, qp[0], kc, vc, kpos), cq, qpos[None])
        full_d = jnp.concatenate([past_d, new_d], axis=1)
        dpos = p0 - win + jnp.arange(win + T)
        o_d = band_core(dq, qpos, full_d[:, :, 0], full_d[:, :, 1], dpos, rel_tab)
        d_state = full_d[:, T:]
    mixed = jnp.concatenate([o_c * jax.nn.silu(cg), o_d * jax.nn.silu(dg)], axis=-1)
    return mixed @ w_out, new_c, d_state


def finish_layer(x, mix, p_i, g_post, w_proj, g_pl, w_gate):
    x = x + rms_norm(mix, g_post)
    e = rms_norm(p_i @ w_proj, g_pl)
    return x + e * jax.nn.sigmoid(x @ w_gate)


def setup_inputs(seed: int = 0) -> dict:
    key = jax.random.key(seed)
    ks = jax.random.split(key, 22)

    def nrm(k, shape, scale=1.0):
        return scale * jax.random.normal(k, shape, jnp.float32)

    d_win = min(BAND_CHUNKS * CHUNK, PAST_LEN)
    in_even = int(sum(EVEN_SIZES))
    in_odd = int(sum(ODD_SIZES))
    return {
        "x_prompt": nrm(ks[0], (BATCH, SEQ, D_MODEL)),
        "x_sample": nrm(ks[1], (DEC_BATCH, DEC_SEQ, D_MODEL)),
        "p_prompt": nrm(ks[2], (DEPTH, BATCH, SEQ, PL_DIM)),
        "p_sample": nrm(ks[3], (DEPTH, DEC_BATCH, DEC_SEQ, PL_DIM)),
        "cache_a_kv": nrm(ks[4], (N_EVEN, DEC_BATCH, PAST_LEN, 2, KV_A, HEAD_DIM)),
        "cache_a_kidx": nrm(ks[5], (N_EVEN, DEC_BATCH, PAST_LEN, D_IDX)),
        "cache_b_kv": nrm(ks[6], (N_EVEN, DEC_BATCH, PAST_LEN, 2, H_B, 2 * HEAD_DIM)),
        "cache_c_kv": nrm(ks[7], (N_ODD, DEC_BATCH, PAST_LEN, 2, H_C, HEAD_DIM)),
        "cache_d_kv": nrm(ks[8], (N_ODD, DEC_BATCH, d_win, 2, H_D, HEAD_DIM)),
        "norm_pre": 1.0 + nrm(ks[9], (DEPTH, D_MODEL), 0.1),
        "norm_post": 1.0 + nrm(ks[10], (DEPTH, D_MODEL), 0.1),
        "w_in_even": nrm(ks[11], (N_EVEN, D_MODEL, in_even), D_MODEL ** -0.5),
        "w_out_even": nrm(ks[12], (N_EVEN, MIX_EVEN, D_MODEL), MIX_EVEN ** -0.5),
        "t5_bias": nrm(ks[13], (T5_BUCKETS, H_A + H_B), 0.2),
        "diff_lambda": nrm(ks[14], (N_EVEN, 4, HEAD_DIM), 0.1),
        "diff_subln": 1.0 + nrm(ks[15], (N_EVEN, 2 * HEAD_DIM), 0.1),
        "w_in_odd": nrm(ks[16], (N_ODD, D_MODEL, in_odd), D_MODEL ** -0.5),
        "w_out_odd": nrm(ks[17], (N_ODD, MIX_ODD, D_MODEL), MIX_ODD ** -0.5),
        "d_rel_bias": nrm(ks[18], (N_ODD, 2 * REL_CLIP + 1, H_D), 0.2),
        "w_pl_proj": nrm(ks[19], (DEPTH, PL_DIM, D_MODEL), PL_DIM ** -0.5),
        "pl_norm": 1.0 + nrm(ks[20], (DEPTH, D_MODEL), 0.1),
        "w_pl_gate": nrm(ks[21], (DEPTH, D_MODEL, D_MODEL), D_MODEL ** -0.5),
    }


def reference(x_prompt, x_sample, p_prompt, p_sample, cache_a_kv, cache_a_kidx, cache_b_kv, cache_c_kv,
              cache_d_kv, norm_pre, norm_post, w_in_even, w_out_even, t5_bias, diff_lambda, diff_subln,
              w_in_odd, w_out_odd, d_rel_bias, w_pl_proj, pl_norm, w_pl_gate):
    xp, xs = x_prompt, x_sample
    a_kv_p, a_kv_s, a_ki_p, a_ki_s, b_kv_p, b_kv_s = [], [], [], [], [], []
    c_kv_p, c_kv_s, d_kv_p, d_kv_s = [], [], [], []
    for i in range(DEPTH):
        j = i // 2
        hp = rms_norm(xp, norm_pre[i])
        hs = rms_norm(xs, norm_pre[i])
        if i % 2 == 0:
            lam_init = 0.8 - 0.6 * math.exp(-0.3 * i)
            prm = (w_in_even[j], w_out_even[j], t5_bias, diff_lambda[j], diff_subln[j], lam_init)
            mp, akv, aki, bkv = even_layer(hp, None, *prm)
            a_kv_p.append(akv)
            a_ki_p.append(aki)
            b_kv_p.append(bkv)
            ms, akv, aki, bkv = even_layer(hs, (cache_a_kv[j], cache_a_kidx[j], cache_b_kv[j]), *prm)
            a_kv_s.append(akv)
            a_ki_s.append(aki)
            b_kv_s.append(bkv)
        else:
            prm = (w_in_odd[j], w_out_odd[j], d_rel_bias[j])
            mp, ckv, dkv = odd_layer(hp, None, *prm)
            c_kv_p.append(ckv)
            d_kv_p.append(dkv)
            ms, ckv, dkv = odd_layer(hs, (cache_c_kv[j], cache_d_kv[j]), *prm)
            c_kv_s.append(ckv)
            d_kv_s.append(dkv)
        xp = finish_layer(xp, mp, p_prompt[i], norm_post[i], w_pl_proj[i], pl_norm[i], w_pl_gate[i])
        xs = finish_layer(xs, ms, p_sample[i], norm_post[i], w_pl_proj[i], pl_norm[i], w_pl_gate[i])
    return (xp, xs, jnp.stack(a_kv_p), jnp.stack(a_kv_s), jnp.stack(a_ki_p), jnp.stack(a_ki_s),
            jnp.stack(b_kv_p), jnp.stack(b_kv_s), jnp.stack(c_kv_p), jnp.stack(c_kv_s),
            jnp.stack(d_kv_p), jnp.stack(d_kv_s))
```

```python
import functools
import math

import numpy as np
import jax
import jax.numpy as jnp
from jax import lax
from jax.experimental import pallas as pl
from jax.experimental.pallas import tpu as pltpu

F32 = jnp.float32
BF16 = jnp.bfloat16
I32 = jnp.int32

D_MODEL = 4096
CHUNK = 64
CHUNK_SHIFT = 6
HEAD_DIM = 128
H_A = 16
KV_A = 4
GROUP_A = H_A // KV_A
H_IDX = 16
D_IDX = 64
TOPK_MAX = 256
H_B = 8
H_C = 16
H_D = 16
BAND_CHUNKS = 8
REL_CLIP = 128
T5_BUCKETS = 32
T5_MAX_DIST = 128
RMS_EPS = 1e-6
QA = H_A * HEAD_DIM
KVA = KV_A * HEAD_DIM
QI = H_IDX * D_IDX
ATT_SCALE = HEAD_DIM ** -0.5
IDX_SCALE = (D_IDX ** -0.5) * (H_IDX ** -0.5)

LANES = 128
NEG = -1e30
INT_MIN = -(2 ** 31)
VMEM_LIMIT = 56 * 1024 * 1024
N_BIAS_VARIANTS = 3


def _cparams(*sem):
    return pltpu.CompilerParams(dimension_semantics=sem, vmem_limit_bytes=VMEM_LIMIT)


def _sigmoid(x):
    return 1.0 / (1.0 + jnp.exp(-x))


def _lane_tile(x, n):
    return x if n == 1 else jnp.concatenate([x] * n, axis=1)


def _rmsnorm_cast_kernel(x_ref, g_ref, o_ref):
    x = x_ref[...]
    ms = jnp.mean(x * x, axis=-1, keepdims=True)
    o_ref[...] = (x * lax.rsqrt(ms + RMS_EPS) * g_ref[...]).astype(o_ref.dtype)


def rmsnorm_cast(x, g, bm=256):
    m, d = x.shape
    bm = min(bm, m)
    return pl.pallas_call(
        _rmsnorm_cast_kernel,
        grid=(m // bm,),
        in_specs=[pl.BlockSpec((bm, d), lambda i: (i, 0)), pl.BlockSpec((1, d), lambda i: (0, 0))],
        out_specs=pl.BlockSpec((bm, d), lambda i: (i, 0)),
        out_shape=jax.ShapeDtypeStruct((m, d), BF16),
        compiler_params=_cparams("parallel"),
    )(x, g.reshape(1, d))


def _residual_norm_kernel(x_ref, mix_ref, g_ref, o_ref, ob_ref):
    mix = mix_ref[...]
    ms = jnp.mean(mix * mix, axis=-1, keepdims=True)
    x1 = x_ref[...] + mix * lax.rsqrt(ms + RMS_EPS) * g_ref[...]
    o_ref[...] = x1
    ob_ref[...] = x1.astype(BF16)


def residual_norm(x, mix, g, bm=256):
    m, d = x.shape
    bm = min(bm, m)
    row = pl.BlockSpec((bm, d), lambda i: (i, 0))
    return pl.pallas_call(
        _residual_norm_kernel,
        grid=(m // bm,),
        in_specs=[row, row, pl.BlockSpec((1, d), lambda i: (0, 0))],
        out_specs=[row, row],
        out_shape=[jax.ShapeDtypeStruct((m, d), F32), jax.ShapeDtypeStruct((m, d), BF16)],
        compiler_params=_cparams("parallel"),
    )(x, mix, g.reshape(1, d))


def _embed_norm_kernel(p_ref, w_ref, g_ref, o_ref):
    y = jnp.dot(p_ref[...], w_ref[...], preferred_element_type=F32)
    ms = jnp.mean(y * y, axis=-1, keepdims=True)
    o_ref[...] = y * lax.rsqrt(ms + RMS_EPS) * g_ref[...]


def embed_norm(p, w, g, bm=256):
    m, k = p.shape
    d = w.shape[1]
    bm = min(bm, m)
    return pl.pallas_call(
        _embed_norm_kernel,
        grid=(m // bm,),
        in_specs=[pl.BlockSpec((bm, k), lambda i: (i, 0)), pl.BlockSpec((k, d), lambda i: (0, 0)),
                  pl.BlockSpec((1, d), lambda i: (0, 0))],
        out_specs=pl.BlockSpec((bm, d), lambda i: (i, 0)),
        out_shape=jax.ShapeDtypeStruct((m, d), F32),
        compiler_params=_cparams("parallel"),
    )(p, w, g.reshape(1, d))


def _mm_kernel(a_ref, w_ref, *o_refs):
    acc = jnp.dot(a_ref[...], w_ref[...], preferred_element_type=F32)
    for o_ref in o_refs:
        o_ref[...] = acc.astype(o_ref.dtype)


def matmul(a, w, out_dtypes, bm=1024, bn=1024):
    m, k = a.shape
    n = w.shape[1]
    bm, bn = min(bm, m), min(bn, n)
    o_spec = pl.BlockSpec((bm, bn), lambda i, j: (i, j))
    outs = pl.pallas_call(
        _mm_kernel,
        grid=(m // bm, n // bn),
        in_specs=[pl.BlockSpec((bm, k), lambda i, j: (i, 0)), pl.BlockSpec((k, bn), lambda i, j: (0, j))],
        out_specs=[o_spec] * len(out_dtypes),
        out_shape=[jax.ShapeDtypeStruct((m, n), dt) for dt in out_dtypes],
        compiler_params=_cparams("parallel", "parallel"),
    )(a, w)
    return outs


def _gate_mm_kernel(a_ref, w_ref, x_ref, e_ref, o_ref):
    logits = jnp.dot(a_ref[...], w_ref[...], preferred_element_type=F32)
    o_ref[...] = x_ref[...] + e_ref[...] * _sigmoid(logits)


def gate_matmul(a, w, x, e, bm=512, bn=1024):
    m, k = a.shape
    n = w.shape[1]
    bm, bn = min(bm, m), min(bn, n)
    tile = pl.BlockSpec((bm, bn), lambda i, j: (i, j))
    return pl.pallas_call(
        _gate_mm_kernel,
        grid=(m // bm, n // bn),
        in_specs=[pl.BlockSpec((bm, k), lambda i, j: (i, 0)), pl.BlockSpec((k, bn), lambda i, j: (0, j)),
                  tile, tile],
        out_specs=tile,
        out_shape=jax.ShapeDtypeStruct((m, n), F32),
        compiler_params=_cparams("parallel", "parallel"),
    )(a, w, x, e)


def _resident(block_shape, index_map):
    return pl.BlockSpec(block_shape, index_map, pipeline_mode=pl.Buffered(1))


def _online_softmax_step(s, vmat, m_ref, l_ref, acc_ref, idx, tk):
    m_prev = m_ref[idx]
    m_new = jnp.maximum(m_prev, jnp.max(s, axis=1, keepdims=True))
    alpha = jnp.exp(m_prev - m_new)
    p = jnp.exp(s - _lane_tile(m_new, tk // LANES))
    l_ref[idx] = alpha * l_ref[idx] + jnp.sum(p, axis=1, keepdims=True)
    pv = jnp.dot(p.astype(BF16), vmat, preferred_element_type=F32)
    acc_ref[idx] = _lane_tile(alpha, vmat.shape[1] // LANES) * acc_ref[idx] + pv
    m_ref[idx] = m_new


def _qk(q, kmat):
    return lax.dot_general(q, kmat, (((1,), (1,)), ((), ())), preferred_element_type=F32)


def _dsa_kernel(q_ref, qi_ref, w_ref, kt_ref, kv_ref, gate_ref, bias_ref, o_ref,
                qs_ref, key_ref, m_ref, l_ref, acc_ref, *, tq, tk, p0, n_sel, idx_bits):
    i = pl.program_id(1)
    q0 = p0 + i * tq
    qb = p0 // tk + i * (tq // tk)
    nvis = qb + 1
    rows4 = GROUP_A * tq

    row_chunk = (q0 + lax.broadcasted_iota(I32, (tq, tk), 0)) >> CHUNK_SHIFT
    lane = lax.broadcasted_iota(I32, (tq, tk), 1)

    wv = w_ref[:, D_IDX:D_IDX + H_IDX] * IDX_SCALE

    def score_tile(k, carry):
        acc = jnp.zeros((tq, tk), F32)
        for pair in range(H_IDX // 2):
            qp = qi_ref[:, LANES * pair:LANES * (pair + 1)]
            for half in range(2):
                s = jnp.dot(qp, kt_ref[k, half], preferred_element_type=F32)
                h = 2 * pair + half
                acc = acc + jnp.maximum(s, 0.0) * wv[:, h:h + 1]
        bits = pltpu.bitcast(acc, I32)
        key = bits ^ ((bits >> 31) & 0x7FFFFFFF)
        key_chunk = (k * tk + lane) >> CHUNK_SHIFT
        key_ref[k] = jnp.where(key_chunk <= row_chunk, key, INT_MIN)
        return carry

    lax.fori_loop(0, nvis, score_tile, 0)

    def count_where(pred):
        def body(k, cnt):
            return cnt + jnp.where(pred(k, key_ref[k]), 1.0, 0.0)
        cnt = lax.fori_loop(0, nvis, body, jnp.zeros((tq, tk), F32))
        return jnp.sum(cnt, axis=1, keepdims=True)

    def count_ge(cand):
        return count_where(lambda k, key: key >= cand)

    def thr_bit(b, t):
        cand = t + lax.shift_left(jnp.int32(1), 31 - b)
        return jnp.where(count_ge(cand) >= n_sel, cand, t)

    thr = lax.fori_loop(0, 32, thr_bit, jnp.full((tq, 1), INT_MIN, I32))
    thr = jnp.maximum(thr, INT_MIN + 1)

    @pl.when(jnp.max(count_ge(thr)) > n_sel)
    def _():
        need = n_sel - count_ge(thr + 1)

        def count_eq_below(c):
            return count_where(lambda k, key: (key == thr) & ((k * tk + lane) < c))

        def idx_bit(b, c):
            cand = c + lax.shift_left(jnp.int32(1), idx_bits - 1 - b)
            return jnp.where(count_eq_below(cand) <= need, cand, c)

        cut = lax.fori_loop(0, idx_bits, idx_bit, jnp.zeros((tq, 1), I32))

        def demote(k, carry):
            key = key_ref[k]
            key_ref[k] = jnp.where((key == thr) & ((k * tk + lane) >= cut), key - 1, key)
            return carry

        lax.fori_loop(0, nvis, demote, 0)

    for n in range(KV_A):
        for g in range(GROUP_A):
            h = GROUP_A * n + g
            qs_ref[n, g * tq:(g + 1) * tq, :] = q_ref[:, HEAD_DIM * h:HEAD_DIM * (h + 1)]
    m_ref[...] = jnp.full(m_ref.shape, NEG, F32)
    l_ref[...] = jnp.zeros(l_ref.shape, F32)
    acc_ref[...] = jnp.zeros(acc_ref.shape, F32)

    def attend_tile(k, carry):
        ks = pl.ds(pl.multiple_of(k * tk, tk), tk)
        variant = jnp.minimum(qb - k, N_BIAS_VARIANTS - 1)
        madd = jnp.where(key_ref[k] >= thr, 0.0, NEG)
        madd4 = jnp.concatenate([madd] * GROUP_A, axis=0)
        for n in range(KV_A):
            kmat = kv_ref[ks, HEAD_DIM * n:HEAD_DIM * (n + 1)]
            vmat = kv_ref[ks, KVA + HEAD_DIM * n:KVA + HEAD_DIM * (n + 1)]
            s = _qk(qs_ref[n], kmat) * ATT_SCALE
            s = s + bias_ref[variant, rows4 * n:rows4 * (n + 1), :] + madd4
            _online_softmax_step(s, vmat, m_ref, l_ref, acc_ref, n, tk)
        return carry

    lax.fori_loop(0, nvis, attend_tile, 0)

    for n in range(KV_A):
        o = acc_ref[n] / l_ref[n]
        for g in range(GROUP_A):
            h = GROUP_A * n + g
            gt = gate_ref[:, HEAD_DIM * h:HEAD_DIM * (h + 1)]
            o_ref[:, HEAD_DIM * h:HEAD_DIM * (h + 1)] = (
                o[g * tq:(g + 1) * tq] * (gt * _sigmoid(gt))).astype(o_ref.dtype)


def dsa_attention(qcat, kiw, kt, kv, gcat, bias, *, tq, tk, p0, n_sel):
    b, t, _ = qcat.shape
    lp = kv.shape[1]
    nk = lp // tk
    assert tk == LANES and p0 % tk == 0 and (tq % tk == 0 or t == tq) and t % tq == 0
    assert p0 + t <= lp and lp % tk == 0
    idx_bits = int(lp).bit_length()
    kern = functools.partial(_dsa_kernel, tq=tq, tk=tk, p0=p0, n_sel=n_sel, idx_bits=idx_bits)
    rows4 = GROUP_A * tq
    return pl.pallas_call(
        kern,
        grid=(b, t // tq),
        in_specs=[
            pl.BlockSpec((None, tq, QA), lambda bi, i: (bi, i, 0)),
            pl.BlockSpec((None, tq, QI), lambda bi, i: (bi, i, QA // QI)),
            pl.BlockSpec((None, tq, LANES), lambda bi, i: (bi, i, 0)),
            _resident((None, nk, 2, LANES, tk), lambda bi, i: (bi, 0, 0, 0, 0)),
            _resident((None, lp, 2 * KVA), lambda bi, i: (bi, 0, 0)),
            pl.BlockSpec((None, tq, QA), lambda bi, i: (bi, i, 0)),
            _resident((N_BIAS_VARIANTS, H_A * tq, tk), lambda bi, i: (0, 0, 0)),
        ],
        out_specs=pl.BlockSpec((None, tq, QA), lambda bi, i: (bi, i, 0)),
        out_shape=jax.ShapeDtypeStruct((b, t, QA), BF16),
        scratch_shapes=[
            pltpu.VMEM((KV_A, rows4, HEAD_DIM), BF16),
            pltpu.VMEM((nk, tq, tk), I32),
            pltpu.VMEM((KV_A, rows4, LANES), F32),
            pltpu.VMEM((KV_A, rows4, LANES), F32),
            pltpu.VMEM((KV_A, rows4, HEAD_DIM), F32),
        ],
        compiler_params=_cparams("parallel", "arbitrary"),
        name="dsa_attention",
    )(qcat, qcat, kiw, kt, kv, gcat, bias)


def _diff_kernel(q_ref, k_ref, v_ref, gate_ref, bias_ref, lam_ref, subln_ref, o_ref,
                 m_ref, l_ref, acc_ref, *, tq, tk, p0, lam_init):
    i = pl.program_id(2)
    q0 = p0 + i * tq
    qb = p0 // tk + i * (tq // tk)
    row_chunk = (q0 + lax.broadcasted_iota(I32, (tq, tk), 0)) >> CHUNK_SHIFT
    lane = lax.broadcasted_iota(I32, (tq, tk), 1)

    m_ref[...] = jnp.full(m_ref.shape, NEG, F32)
    l_ref[...] = jnp.zeros(l_ref.shape, F32)
    acc_ref[...] = jnp.zeros(acc_ref.shape, F32)

    def step(k, masked):
        ks = pl.ds(pl.multiple_of(k * tk, tk), tk)
        bias = bias_ref[jnp.minimum(qb - k, N_BIAS_VARIANTS - 1)]
        if masked:
            key_chunk = (k * tk + lane) >> CHUNK_SHIFT
            bias = bias + jnp.where(key_chunk <= row_chunk, 0.0, NEG)
        vmat = v_ref[ks, :]
        for c in range(2):
            kmat = k_ref[ks, HEAD_DIM * c:HEAD_DIM * (c + 1)]
            s = _qk(q_ref[:, HEAD_DIM * c:HEAD_DIM * (c + 1)], kmat) * ATT_SCALE + bias
            _online_softmax_step(s, vmat, m_ref, l_ref, acc_ref, c, tk)

    def full_tile(k, carry):
        step(k, False)
        return carry

    lax.fori_loop(0, qb, full_tile, 0)
    step(qb, True)

    lam_q1, lam_k1, lam_q2, lam_k2 = (lam_ref[r:r + 1, :] for r in range(4))
    lam = (jnp.exp(jnp.sum(lam_q1 * lam_k1, axis=1, keepdims=True))
           - jnp.exp(jnp.sum(lam_q2 * lam_k2, axis=1, keepdims=True)) + lam_init)
    nv = acc_ref.shape[2] // LANES
    o = acc_ref[0] / _lane_tile(l_ref[0], nv) - lam * (acc_ref[1] / _lane_tile(l_ref[1], nv))
    ms = jnp.mean(o * o, axis=-1, keepdims=True)
    y = o * lax.rsqrt(ms + RMS_EPS) * subln_ref[...] * (1.0 - lam_init)
    gt = gate_ref[...]
    o_ref[...] = (y * (gt * _sigmoid(gt))).astype(o_ref.dtype)


def diff_attention(qcat, kv, gcat, bias, lam_vec, subln, *, tq, tk, p0, lam_init):
    b, t, _ = qcat.shape
    lp = kv.shape[1]
    dv = 2 * HEAD_DIM
    assert p0 % tk == 0 and (tq % tk == 0 or t == tq) and t % tq == 0 and lp % tk == 0
    kern = functools.partial(_diff_kernel, tq=tq, tk=tk, p0=p0, lam_init=lam_init)
    q_off = (QA + QI) // dv
    return pl.pallas_call(
        kern,
        grid=(b, H_B, t // tq),
        in_specs=[
            pl.BlockSpec((None, tq, dv), lambda bi, h, i: (bi, i, q_off + h)),
            pl.BlockSpec((None, lp, dv), lambda bi, h, i: (bi, 0, h)),
            pl.BlockSpec((None, lp, dv), lambda bi, h, i: (bi, 0, H_B + h)),
            pl.BlockSpec((None, tq, dv), lambda bi, h, i: (bi, i, H_B + h)),
            pl.BlockSpec((N_BIAS_VARIANTS, None, tq, tk), lambda bi, h, i: (0, h, 0, 0)),
            pl.BlockSpec((4, HEAD_DIM), lambda bi, h, i: (0, 0)),
            pl.BlockSpec((1, dv), lambda bi, h, i: (0, 0)),
        ],
        out_specs=pl.BlockSpec((None, tq, dv), lambda bi, h, i: (bi, i, h)),
        out_shape=jax.ShapeDtypeStruct((b, t, QA), BF16),
        scratch_shapes=[
            pltpu.VMEM((2, tq, LANES), F32),
            pltpu.VMEM((2, tq, LANES), F32),
            pltpu.VMEM((2, tq, dv), F32),
        ],
        compiler_params=_cparams("parallel", "parallel", "arbitrary"),
        name="diff_attention",
    )(qcat, kv, kv, gcat, bias, lam_vec, subln.reshape(1, dv))


def _stick_kernel(q_ref, k_ref, v_ref, gate_ref, tri_ref, o_ref, acc_ref, carry_ref, *, tq, tk, p0):
    i = pl.program_id(2)
    q0 = p0 + i * tq
    qb = p0 // tk + i * (tq // tk)
    row_pos = q0 + lax.broadcasted_iota(I32, (tq, tk), 0)
    lane = lax.broadcasted_iota(I32, (tq, tk), 1)
    reps = tk // LANES

    acc_ref[...] = jnp.zeros(acc_ref.shape, F32)
    carry_ref[...] = jnp.zeros(carry_ref.shape, F32)

    def step(k, masked):
        ks = pl.ds(pl.multiple_of(k * tk, tk), tk)
        z = _qk(q_ref[...], k_ref[ks, :]) * ATT_SCALE
        softplus = jnp.maximum(z, 0.0) + jnp.log1p(jnp.exp(-jnp.abs(z)))
        log_fail = -softplus
        log_hit = z - softplus
        if masked:
            before = (k * tk + lane) < row_pos
            log_fail = jnp.where(before, log_fail, 0.0)
        hi = log_fail.astype(BF16)
        lo = (log_fail - hi.astype(F32)).astype(BF16)
        later = (jnp.dot(hi, tri_ref[...], preferred_element_type=F32)
                 + jnp.dot(lo, tri_ref[...], preferred_element_type=F32))
        w = jnp.exp(log_hit + later + _lane_tile(carry_ref[...], reps))
        if masked:
            w = jnp.where(before, w, 0.0)
        acc_ref[...] += jnp.dot(w.astype(BF16), v_ref[ks, :], preferred_element_type=F32)
        carry_ref[...] += jnp.sum(log_fail, axis=1, keepdims=True)

    step(qb, True)

    def full_tile(j, carry):
        step(qb - 1 - j, False)
        return carry

    lax.fori_loop(0, qb, full_tile, 0)

    gt = gate_ref[...]
    o_ref[...] = (acc_ref[...] * (gt * _sigmoid(gt))).astype(o_ref.dtype)


def stick_attention(qcat, kv, gcat, tri, *, tq, tk, p0):
    b, t, _ = qcat.shape
    lp = kv.shape[1]
    assert p0 % tk == 0 and (tq == tk or (t == tq and tq <= tk)) and t % tq == 0 and lp % tk == 0
    kern = functools.partial(_stick_kernel, tq=tq, tk=tk, p0=p0)
    return pl.pallas_call(
        kern,
        grid=(b, H_C, t // tq),
        in_specs=[
            pl.BlockSpec((None, tq, HEAD_DIM), lambda bi, h, i: (bi, i, h)),
            pl.BlockSpec((None, lp, HEAD_DIM), lambda bi, h, i: (bi, 0, h)),
            pl.BlockSpec((None, lp, HEAD_DIM), lambda bi, h, i: (bi, 0, H_C + h)),
            pl.BlockSpec((None, tq, HEAD_DIM), lambda bi, h, i: (bi, i, h)),
            pl.BlockSpec((tk, tk), lambda bi, h, i: (0, 0)),
        ],
        out_specs=pl.BlockSpec((None, tq, HEAD_DIM), lambda bi, h, i: (bi, i, h)),
        out_shape=jax.ShapeDtypeStruct((b, t, QA), BF16),
        scratch_shapes=[pltpu.VMEM((tq, HEAD_DIM), F32), pltpu.VMEM((tq, LANES), F32)],
        compiler_params=_cparams("parallel", "parallel", "arbitrary"),
        name="stick_attention",
    )(qcat, kv, kv, gcat, tri)


def _band_kernel(q_ref, k_ref, v_ref, gate_ref, bias_ref, o_ref, m_ref, l_ref, acc_ref,
                 *, tq, tk, p0, kbase):
    i = pl.program_id(2)
    q0 = p0 + i * tq
    qb = (p0 - kbase) // tk + i * (tq // tk)
    k_lo = jnp.maximum(qb - (BAND_CHUNKS * CHUNK) // tk, 0)
    row_chunk = (q0 + lax.broadcasted_iota(I32, (tq, tk), 0)) >> CHUNK_SHIFT
    lane = lax.broadcasted_iota(I32, (tq, tk), 1)

    m_ref[...] = jnp.full(m_ref.shape, NEG, F32)
    l_ref[...] = jnp.zeros(l_ref.shape, F32)
    acc_ref[...] = jnp.zeros(acc_ref.shape, F32)

    def step(k, carry):
        ks = pl.ds(pl.multiple_of(k * tk, tk), tk)
        key_chunk = (kbase + k * tk + lane) >> CHUNK_SHIFT
        visible = (key_chunk <= row_chunk) & (key_chunk >= row_chunk - BAND_CHUNKS)
        bias = bias_ref[jnp.minimum(qb - k, N_BIAS_VARIANTS - 1)] + jnp.where(visible, 0.0, NEG)
        s = _qk(q_ref[...], k_ref[ks, :]) * ATT_SCALE + bias
        _online_softmax_step(s, v_ref[ks, :], m_ref, l_ref, acc_ref, 0, tk)
        return carry

    lax.fori_loop(k_lo, qb + 1, step, 0)

    gt = gate_ref[...]
    o_ref[...] = (acc_ref[0] / l_ref[0] * (gt * _sigmoid(gt))).astype(o_ref.dtype)


def band_attention(qcat, kv, gcat, bias, *, tq, tk, p0, kbase):
    b, t, _ = qcat.shape
    lp = kv.shape[1]
    assert tk == LANES and (p0 - kbase) % tk == 0 and kbase % CHUNK == 0 and lp % tk == 0
    assert (tq % tk == 0 or t == tq) and t % tq == 0
    kern = functools.partial(_band_kernel, tq=tq, tk=tk, p0=p0, kbase=kbase)
    return pl.pallas_call(
        kern,
        grid=(b, H_D, t // tq),
        in_specs=[
            pl.BlockSpec((None, tq, HEAD_DIM), lambda bi, h, i: (bi, i, H_C + h)),
            pl.BlockSpec((None, lp, HEAD_DIM), lambda bi, h, i: (bi, 0, h)),
            pl.BlockSpec((None, lp, HEAD_DIM), lambda bi, h, i: (bi, 0, H_D + h)),
            pl.BlockSpec((None, tq, HEAD_DIM), lambda bi, h, i: (bi, i, H_C + h)),
            pl.BlockSpec((N_BIAS_VARIANTS, None, tq, tk), lambda bi, h, i: (0, h, 0, 0)),
        ],
        out_specs=pl.BlockSpec((None, tq, HEAD_DIM), lambda bi, h, i: (bi, i, h)),
        out_shape=jax.ShapeDtypeStruct((b, t, QA), BF16),
        scratch_shapes=[
            pltpu.VMEM((1, tq, LANES), F32),
            pltpu.VMEM((1, tq, LANES), F32),
            pltpu.VMEM((1, tq, HEAD_DIM), F32),
        ],
        compiler_params=_cparams("parallel", "parallel", "arbitrary"),
        name="band_attention",
    )(qcat, kv, kv, gcat, bias)


def _t5_bucket_np(rel):
    half = T5_BUCKETS // 2
    max_exact = half // 2
    n = np.abs(rel)
    nf = np.maximum(n, 1).astype(np.float64)
    large = max_exact + (np.log(nf / max_exact) / math.log(T5_MAX_DIST / max_exact)
                         * (half - max_exact)).astype(np.int32)
    large = np.minimum(large, half - 1)
    return np.where(rel < 0, half, 0) + np.where(n < max_exact, n, large)


def _tile_rel(tq, tk):
    v = np.arange(N_BIAS_VARIANTS)[:, None, None]
    i = np.arange(tq)[None, :, None]
    j = np.arange(tk)[None, None, :]
    return v * tk + i - j


def t5_bias_tiles(t5_tab, tq, tk):
    idx = _t5_bucket_np(_tile_rel(tq, tk)).astype(np.int32)
    assert (idx[-1] == T5_BUCKETS // 2 - 1).all()
    return jnp.moveaxis(t5_tab[idx], -1, 1)


def band_bias_tiles(rel_tab, tq, tk):
    idx = (np.clip(_tile_rel(tq, tk), -REL_CLIP, REL_CLIP) + REL_CLIP).astype(np.int32)
    assert (idx[-1] == 2 * REL_CLIP).all()
    return jnp.moveaxis(rel_tab[idx], -1, 1)


def index_key_tiles(ki, tk):
    b, lp, d = ki.shape
    kt = jnp.swapaxes(ki.astype(BF16).reshape(b, lp // tk, tk, d), 2, 3)
    z = jnp.zeros_like(kt)
    return jnp.stack([jnp.concatenate([kt, z], axis=2), jnp.concatenate([z, kt], axis=2)], axis=2)


def _pad_rows(x, lp):
    return jnp.pad(x, ((0, 0), (0, lp - x.shape[1]), (0, 0)))


def _even_weights(w_in):
    c = np.cumsum((0, QA, KVA, KVA, QI, D_IDX, H_IDX, QA, QA, QA, QA, QA)).tolist()
    aq, ak, av, aqi, aki, aw, ag, bq, bk, bv, bg = (w_in[:, c[r]:c[r + 1]] for r in range(11))
    pad = jnp.zeros((w_in.shape[0], LANES - D_IDX - H_IDX), w_in.dtype)
    cat = lambda *xs: jnp.concatenate(xs, axis=1).astype(BF16)
    return cat(aq, aqi, bq), cat(ak, av), cat(bk, bv), cat(aki, aw, pad), cat(ag, bg)


def _odd_weights(w_in):
    cq, ck, cv, cg, dq, dk, dv, dg = (w_in[:, QA * r:QA * (r + 1)] for r in range(8))
    cat = lambda *xs: jnp.concatenate(xs, axis=1).astype(BF16)
    return cat(cq, dq), cat(ck, cv), cat(dk, dv), cat(cg, dg)


def _even_layer(h, past, weights, t5_tab, lam_vec, subln, lam_init, *, tq_a, tq_b, tk_b):
    w_q, w_akv, w_bkv, w_kiw, w_g = weights
    b, t, d = h.shape
    h2 = h.reshape(b * t, d)
    (qcat,) = matmul(h2, w_q, (BF16,))
    akv, akv_b = matmul(h2, w_akv, (F32, BF16))
    bkv, bkv_b = matmul(h2, w_bkv, (F32, BF16))
    (kiw,) = matmul(h2, w_kiw, (F32,))
    (gcat,) = matmul(h2, w_g, (F32,))
    qcat, akv_b, bkv_b, kiw, gcat = (x.reshape(b, t, -1) for x in (qcat, akv_b, bkv_b, kiw, gcat))
    aki = kiw[..., :D_IDX]
    if past is None:
        p0 = 0
        full_a, full_ki, full_b = akv_b, aki, bkv_b
    else:
        past_a, past_ki, past_b = past
        p0 = past_a.shape[1]
        lp = p0 + 2 * t
        full_a = _pad_rows(jnp.concatenate([past_a.reshape(b, p0, -1).astype(BF16), akv_b], axis=1), lp)
        full_ki = _pad_rows(jnp.concatenate([past_ki, aki], axis=1), lp)
        full_b = _pad_rows(jnp.concatenate([past_b.reshape(b, p0, -1).astype(BF16), bkv_b], axis=1), lp)
    n_sel = min(TOPK_MAX, (p0 + t) // 4)
    tk_a = LANES
    bias_a = t5_bias_tiles(t5_tab[:, :H_A], tq_a, tk_a).reshape(N_BIAS_VARIANTS, H_A * tq_a, tk_a)
    o_a = dsa_attention(qcat, kiw, index_key_tiles(full_ki, tk_a), full_a, gcat, bias_a,
                        tq=tq_a, tk=tk_a, p0=p0, n_sel=n_sel)
    bias_b = t5_bias_tiles(t5_tab[:, H_A:], tq_b, tk_b)
    o_b = diff_attention(qcat, full_b, gcat, bias_b, lam_vec, subln, tq=tq_b, tk=tk_b, p0=p0,
                         lam_init=lam_init)
    mixed = jnp.concatenate([o_a, o_b], axis=-1).reshape(b * t, 2 * QA)
    return mixed, akv.reshape(b, t, -1), aki, bkv.reshape(b, t, -1)


def _odd_layer(h, past, weights, rel_tab, *, tq_c, tk_c, tq_d):
    w_q, w_ckv, w_dkv, w_g = weights
    b, t, d = h.shape
    h2 = h.reshape(b * t, d)
    (qcat,) = matmul(h2, w_q, (BF16,))
    ckv, ckv_b = matmul(h2, w_ckv, (F32, BF16))
    dkv, dkv_b = matmul(h2, w_dkv, (F32, BF16))
    (gcat,) = matmul(h2, w_g, (F32,))
    qcat, ckv_b, dkv_b, gcat = (x.reshape(b, t, -1) for x in (qcat, ckv_b, dkv_b, gcat))
    tk_d = LANES
    if past is None:
        p0, kbase = 0, 0
        full_c, full_d = ckv_b, dkv_b
    else:
        past_c, past_d = past
        p0 = past_c.shape[1]
        win = past_d.shape[1]
        kbase = p0 - win
        full_c = _pad_rows(jnp.concatenate([past_c.reshape(b, p0, -1).astype(BF16), ckv_b], axis=1), p0 + 2 * t)
        full_d = _pad_rows(jnp.concatenate([past_d.reshape(b, win, -1).astype(BF16), dkv_b], axis=1), win + 2 * t)
    tri = jnp.asarray(np.tril(np.ones((tk_c, tk_c), np.float32), -1), BF16)
    o_c = stick_attention(qcat, full_c, gcat, tri, tq=tq_c, tk=tk_c, p0=p0)
    bias_d = band_bias_tiles(rel_tab, tq_d, tk_d)
    o_d = band_attention(qcat, full_d, gcat, bias_d, tq=tq_d, tk=tk_d, p0=p0, kbase=kbase)
    mixed = jnp.concatenate([o_c, o_d], axis=-1).reshape(b * t, 2 * QA)
    return mixed, ckv.reshape(b, t, -1), dkv.reshape(b, t, -1)


def _finish_layer(x, mixed, w_out, p_i, g_post, w_proj, g_pl, w_gate):
    (mix,) = matmul(mixed, w_out, (F32,))
    x1, x1_b = residual_norm(x, mix, g_post)
    e = embed_norm(p_i.astype(BF16), w_proj, g_pl)
    return gate_matmul(x1_b, w_gate, x1, e)


def kernel(x_prompt, x_sample, p_prompt, p_sample, cache_a_kv, cache_a_kidx, cache_b_kv, cache_c_kv,
           cache_d_kv, norm_pre, norm_post, w_in_even, w_out_even, t5_bias, diff_lambda, diff_subln,
           w_in_odd, w_out_odd, d_rel_bias, w_pl_proj, pl_norm, w_pl_gate):
    bp, tp, d = x_prompt.shape
    bs, ts, _ = x_sample.shape
    depth = norm_pre.shape[0]
    assert d == D_MODEL and ts == CHUNK and tp % 256 == 0
    xp = x_prompt.reshape(bp * tp, d)
    xs = x_sample.reshape(bs * ts, d)
    outs = {name: [] for name in ("a_kv_p", "a_kv_s", "a_ki_p", "a_ki_s", "b_kv_p", "b_kv_s",
                                  "c_kv_p", "c_kv_s", "d_kv_p", "d_kv_s")}
    for i in range(depth):
        j = i // 2
        hp = rmsnorm_cast(xp, norm_pre[i]).reshape(bp, tp, d)
        hs = rmsnorm_cast(xs, norm_pre[i]).reshape(bs, ts, d)
        if i % 2 == 0:
            lam_init = 0.8 - 0.6 * math.exp(-0.3 * i)
            weights = _even_weights(w_in_even[j])
            w_out = w_out_even[j].astype(BF16)
            prm = (weights, t5_bias, diff_lambda[j], diff_subln[j], lam_init)
            mp, akv, aki, bkv = _even_layer(hp, None, *prm, tq_a=128, tq_b=256, tk_b=256)
            outs["a_kv_p"].append(akv.reshape(bp, tp, 2, KV_A, HEAD_DIM))
            outs["a_ki_p"].append(aki)
            outs["b_kv_p"].append(bkv.reshape(bp, tp, 2, H_B, 2 * HEAD_DIM))
            ms, akv, aki, bkv = _even_layer(hs, (cache_a_kv[j], cache_a_kidx[j], cache_b_kv[j]), *prm,
                                            tq_a=ts, tq_b=ts, tk_b=LANES)
            outs["a_kv_s"].append(akv.reshape(bs, ts, 2, KV_A, HEAD_DIM))
            outs["a_ki_s"].append(aki)
            outs["b_kv_s"].append(bkv.reshape(bs, ts, 2, H_B, 2 * HEAD_DIM))
        else:
            weights = _odd_weights(w_in_odd[j])
            w_out = w_out_odd[j].astype(BF16)
            mp, ckv, dkv = _odd_layer(hp, None, weights, d_rel_bias[j], tq_c=256, tk_c=256, tq_d=128)
            win_p = min(BAND_CHUNKS * CHUNK, tp)
            outs["c_kv_p"].append(ckv.reshape(bp, tp, 2, H_C, HEAD_DIM))
            outs["d_kv_p"].append(dkv[:, tp - win_p:].reshape(bp, win_p, 2, H_D, HEAD_DIM))
            ms, ckv, dkv = _odd_layer(hs, (cache_c_kv[j], cache_d_kv[j]), weights, d_rel_bias[j],
                                      tq_c=ts, tk_c=LANES, tq_d=ts)
            outs["c_kv_s"].append(ckv.reshape(bs, ts, 2, H_C, HEAD_DIM))
            full_d = jnp.concatenate([cache_d_kv[j], dkv.reshape(bs, ts, 2, H_D, HEAD_DIM)], axis=1)
            outs["d_kv_s"].append(full_d[:, ts:])
        fin = (norm_post[i], w_pl_proj[i].astype(BF16), pl_norm[i], w_pl_gate[i].astype(BF16))
        xp = _finish_layer(xp, mp, w_out, p_prompt[i].reshape(bp * tp, -1), *fin)
        xs = _finish_layer(xs, ms, w_out, p_sample[i].reshape(bs * ts, -1), *fin)
    st = lambda name: jnp.stack(outs[name])
    return (xp.reshape(bp, tp, d), xs.reshape(bs, ts, d), st("a_kv_p"), st("a_kv_s"), st("a_ki_p"), st("a_ki_s"),
            st("b_kv_p"), st("b_kv_s"), st("c_kv_p"), st("c_kv_s"), st("d_kv_p"), st("d_kv_s"))
```

```python
import functools
import math

import numpy as np
import jax
import jax.numpy as jnp
from jax import lax
from jax.experimental import pallas as pl
from jax.experimental.pallas import tpu as pltpu

F32 = jnp.float32
BF16 = jnp.bfloat16
I32 = jnp.int32

D_MODEL = 4096
CHUNK = 64
CHUNK_SHIFT = 6
HEAD_DIM = 128
H_A = 16
KV_A = 4
GROUP_A = H_A // KV_A
H_IDX = 16
D_IDX = 64
TOPK_MAX = 256
H_B = 8
H_C = 16
H_D = 16
BAND_CHUNKS = 8
REL_CLIP = 128
T5_BUCKETS = 32
T5_MAX_DIST = 128
RMS_EPS = 1e-6
QA = H_A * HEAD_DIM
KVA = KV_A * HEAD_DIM
QI = H_IDX * D_IDX
DV_B = 2 * HEAD_DIM
ATT_SCALE = HEAD_DIM ** -0.5
IDX_SCALE = (D_IDX ** -0.5) * (H_IDX ** -0.5)

LANES = 128
NEG = -1e30
INT_MIN = -(2 ** 31)
VMEM_LIMIT = 56 * 1024 * 1024
N_BIAS_VARIANTS = 3
TK_SAMPLE = LANES
PT_SAMPLE = 512
HB_SAMPLE = 8


def _cparams(*sem):
    return pltpu.CompilerParams(dimension_semantics=sem, vmem_limit_bytes=VMEM_LIMIT)


def _sigmoid(x):
    return 1.0 / (1.0 + jnp.exp(-x))


def _silu(x):
    return x * _sigmoid(x)


def _lane_tile(x, n):
    return x if n == 1 else jnp.concatenate([x] * n, axis=1)


def _cols(h, width=HEAD_DIM):
    return slice(width * h, width * (h + 1))


def _rmsnorm_cast_kernel(x_ref, g_ref, o_ref):
    x = x_ref[...]
    ms = jnp.mean(x * x, axis=-1, keepdims=True)
    o_ref[...] = (x * lax.rsqrt(ms + RMS_EPS) * g_ref[...]).astype(o_ref.dtype)


def rmsnorm_cast(x, g, bm=256):
    m, d = x.shape
    bm = min(bm, m)
    return pl.pallas_call(
        _rmsnorm_cast_kernel,
        grid=(m // bm,),
        in_specs=[pl.BlockSpec((bm, d), lambda i: (i, 0)), pl.BlockSpec((1, d), lambda i: (0, 0))],
        out_specs=pl.BlockSpec((bm, d), lambda i: (i, 0)),
        out_shape=jax.ShapeDtypeStruct((m, d), BF16),
        compiler_params=_cparams("parallel"),
        name="rmsnorm_cast",
    )(x, g.reshape(1, d))


def _residual_norm_kernel(x_ref, mix_ref, g_ref, o_ref, ob_ref):
    mix = mix_ref[...]
    ms = jnp.mean(mix * mix, axis=-1, keepdims=True)
    x1 = x_ref[...] + mix * lax.rsqrt(ms + RMS_EPS) * g_ref[...]
    o_ref[...] = x1
    ob_ref[...] = x1.astype(BF16)


def residual_norm(x, mix, g, bm=256):
    m, d = x.shape
    bm = min(bm, m)
    row = pl.BlockSpec((bm, d), lambda i: (i, 0))
    return pl.pallas_call(
        _residual_norm_kernel,
        grid=(m // bm,),
        in_specs=[row, row, pl.BlockSpec((1, d), lambda i: (0, 0))],
        out_specs=[row, row],
        out_shape=[jax.ShapeDtypeStruct((m, d), F32), jax.ShapeDtypeStruct((m, d), BF16)],
        compiler_params=_cparams("parallel"),
        name="residual_norm",
    )(x, mix, g.reshape(1, d))


def _embed_norm_kernel(p_ref, w_ref, g_ref, o_ref):
    y = jnp.dot(p_ref[...], w_ref[...], preferred_element_type=F32)
    ms = jnp.mean(y * y, axis=-1, keepdims=True)
    o_ref[...] = y * lax.rsqrt(ms + RMS_EPS) * g_ref[...]


def embed_norm(p, w, g, bm=256):
    m, k = p.shape
    d = w.shape[1]
    bm = min(bm, m)
    return pl.pallas_call(
        _embed_norm_kernel,
        grid=(m // bm,),
        in_specs=[pl.BlockSpec((bm, k), lambda i: (i, 0)), pl.BlockSpec((k, d), lambda i: (0, 0)),
                  pl.BlockSpec((1, d), lambda i: (0, 0))],
        out_specs=pl.BlockSpec((bm, d), lambda i: (i, 0)),
        out_shape=jax.ShapeDtypeStruct((m, d), F32),
        compiler_params=_cparams("parallel"),
        name="embed_norm",
    )(p, w, g.reshape(1, d))


def _mm_kernel(a_ref, w_ref, *o_refs):
    acc = jnp.dot(a_ref[...], w_ref[...], preferred_element_type=F32)
    for o_ref in o_refs:
        o_ref[...] = acc.astype(o_ref.dtype)


def matmul(a, w, out_dtypes, bm=1024, bn=1024):
    m, k = a.shape
    n = w.shape[1]
    bm, bn = min(bm, m), min(bn, n)
    o_spec = pl.BlockSpec((bm, bn), lambda i, j: (i, j))
    outs = pl.pallas_call(
        _mm_kernel,
        grid=(m // bm, n // bn),
        in_specs=[pl.BlockSpec((bm, k), lambda i, j: (i, 0)), pl.BlockSpec((k, bn), lambda i, j: (0, j))],
        out_specs=[o_spec] * len(out_dtypes),
        out_shape=[jax.ShapeDtypeStruct((m, n), dt) for dt in out_dtypes],
        compiler_params=_cparams("parallel", "parallel"),
        name="matmul",
    )(a, w)
    return outs


def _gate_mm_kernel(a_ref, w_ref, x_ref, e_ref, o_ref):
    logits = jnp.dot(a_ref[...], w_ref[...], preferred_element_type=F32)
    o_ref[...] = x_ref[...] + e_ref[...] * _sigmoid(logits)


def gate_matmul(a, w, x, e, bm=512, bn=1024):
    m, k = a.shape
    n = w.shape[1]
    bm, bn = min(bm, m), min(bn, n)
    tile = pl.BlockSpec((bm, bn), lambda i, j: (i, j))
    return pl.pallas_call(
        _gate_mm_kernel,
        grid=(m // bm, n // bn),
        in_specs=[pl.BlockSpec((bm, k), lambda i, j: (i, 0)), pl.BlockSpec((k, bn), lambda i, j: (0, j)),
                  tile, tile],
        out_specs=tile,
        out_shape=jax.ShapeDtypeStruct((m, n), F32),
        compiler_params=_cparams("parallel", "parallel"),
        name="gate_matmul",
    )(a, w, x, e)


def _resident(block_shape, index_map):
    return pl.BlockSpec(block_shape, index_map, pipeline_mode=pl.Buffered(1))


def _qk(q, kmat):
    return lax.dot_general(q, kmat, (((1,), (1,)), ((), ())), preferred_element_type=F32)


def _softmax_tile(s, vmat, m_prev, acc_prev):
    tk = s.shape[1]
    m_new = jnp.maximum(m_prev, jnp.max(s, axis=1, keepdims=True))
    alpha = jnp.exp(m_prev - m_new)
    p = jnp.exp(s - _lane_tile(m_new, tk // LANES))
    v_ext = jnp.concatenate([vmat, jnp.ones((tk, LANES), BF16)], axis=1)
    pv = jnp.dot(p.astype(BF16), v_ext, preferred_element_type=F32)
    return m_new, _lane_tile(alpha, v_ext.shape[1] // LANES) * acc_prev + pv


def _softmax_finish(acc):
    dv = acc.shape[1] - LANES
    return acc[:, :dv] / _lane_tile(acc[:, dv:], dv // LANES)


def _stick_tile(q, kmat, vmat, tri_ref, carry, before):
    tk = kmat.shape[0]
    z = _qk(q, kmat) * ATT_SCALE
    softplus = jnp.maximum(z, 0.0) + jnp.log(1.0 + jnp.exp(-jnp.abs(z)))
    log_fail = -softplus
    hi = log_fail.astype(BF16)
    if before is None:
        sums = jnp.dot(hi, tri_ref[:tk, :], preferred_element_type=F32)
    else:
        log_fail = jnp.where(before, log_fail, 0.0)
        hi = log_fail.astype(BF16)
        lo = (log_fail - hi.astype(F32)).astype(BF16)
        sums = jnp.dot(jnp.concatenate([hi, lo], axis=1), tri_ref[...], preferred_element_type=F32)
    w = jnp.exp((z - softplus) + sums[:, :tk] + _lane_tile(carry, tk // LANES))
    if before is not None:
        w = jnp.where(before, w, 0.0)
    pv = jnp.dot(w.astype(BF16), vmat, preferred_element_type=F32)
    return pv, carry + sums[:, tk:]


def _diff_finish(acc0, acc1, lam_ref, subln_ref, gate, lam_init):
    lam_q1, lam_k1, lam_q2, lam_k2 = (lam_ref[r:r + 1, :] for r in range(4))
    lam = (jnp.exp(jnp.sum(lam_q1 * lam_k1, axis=1, keepdims=True))
           - jnp.exp(jnp.sum(lam_q2 * lam_k2, axis=1, keepdims=True)) + lam_init)
    o = _softmax_finish(acc0) - lam * _softmax_finish(acc1)
    ms = jnp.mean(o * o, axis=-1, keepdims=True)
    y = o * lax.rsqrt(ms + RMS_EPS) * subln_ref[...] * (1.0 - lam_init)
    return y * _silu(gate)


def _pad_new_tile(x, tk):
    return jnp.concatenate([x, jnp.zeros((tk - x.shape[0], x.shape[1]), x.dtype)], axis=0)


def _dsa_scores_to_keys(qih_ref, wb_ref, ki_tile, visible):
    tq = wb_ref.shape[1]
    s_all = _qk(qih_ref[...], ki_tile)
    acc = jnp.zeros((tq, ki_tile.shape[0]), F32)
    for h in range(H_IDX):
        acc = acc + jnp.maximum(s_all[h * tq:(h + 1) * tq], 0.0) * wb_ref[h]
    bits = pltpu.bitcast(acc, I32)
    key = bits ^ ((bits >> 31) & 0x7FFFFFFF)
    return jnp.where(visible, key, INT_MIN)


def _dsa_prepare(q_ref, qi_ref, w_ref, qs_ref, qih_ref, wb_ref, tq):
    wv = w_ref[:, D_IDX:D_IDX + H_IDX] * IDX_SCALE
    for h in range(H_IDX):
        qih_ref[h * tq:(h + 1) * tq, :] = qi_ref[:, _cols(h, D_IDX)]
        wb_ref[h] = jnp.broadcast_to(wv[:, h:h + 1], (tq, LANES))
    for n in range(KV_A):
        for g in range(GROUP_A):
            qs_ref[n, g * tq:(g + 1) * tq, :] = q_ref[:, _cols(GROUP_A * n + g)]


def _dsa_threshold(key_ref, nvis, n_sel, idx_bits, tq, tk):
    lane = lax.broadcasted_iota(I32, (tq, tk), 1)

    def count_where(pred):
        def body(k, cnt):
            return cnt + jnp.where(pred(k, key_ref[k]), 1.0, 0.0)
        cnt = lax.fori_loop(0, nvis, body, jnp.zeros((tq, tk), F32))
        return jnp.sum(cnt, axis=1, keepdims=True)

    def count_ge(cand):
        return count_where(lambda k, key: key >= cand)

    def thr_bit(b, t):
        cand = t + lax.shift_left(jnp.int32(1), 31 - b)
        return jnp.where(count_ge(cand) >= n_sel, cand, t)

    thr = lax.fori_loop(0, 32, thr_bit, jnp.full((tq, 1), INT_MIN, I32))
    thr = jnp.maximum(thr, INT_MIN + 1)

    @pl.when(jnp.max(count_ge(thr)) > n_sel)
    def _():
        need = n_sel - count_ge(thr + 1)

        def count_eq_below(c):
            return count_where(lambda k, key: (key == thr) & ((k * tk + lane) < c))

        def idx_bit(b, c):
            cand = c + lax.shift_left(jnp.int32(1), idx_bits - 1 - b)
            return jnp.where(count_eq_below(cand) <= need, cand, c)

        cut = lax.fori_loop(0, idx_bits, idx_bit, jnp.zeros((tq, 1), I32))

        def demote(k, carry):
            key = key_ref[k]
            key_ref[k] = jnp.where((key == thr) & ((k * tk + lane) >= cut), key - 1, key)
            return carry

        lax.fori_loop(0, nvis, demote, 0)

    return thr


def _dsa_attend_tile(qs_ref, key_tile, thr, bias_tile, kv_of_head, m_ref, acc_ref, tq):
    rows4 = GROUP_A * tq
    madd = jnp.where(key_tile >= thr, 0.0, NEG)
    madd4 = jnp.concatenate([madd] * GROUP_A, axis=0)
    for n in range(KV_A):
        kmat, vmat = kv_of_head(n)
        s = _qk(qs_ref[n], kmat) * ATT_SCALE + bias_tile[rows4 * n:rows4 * (n + 1), :] + madd4
        m_ref[n], acc_ref[n] = _softmax_tile(s, vmat, m_ref[n], acc_ref[n])


def _dsa_finish(acc_ref, gate_ref, o_ref, tq):
    for n in range(KV_A):
        o = _softmax_finish(acc_ref[n])
        for g in range(GROUP_A):
            cs = _cols(GROUP_A * n + g)
            o_ref[:, cs] = (o[g * tq:(g + 1) * tq] * _silu(gate_ref[:, cs])).astype(o_ref.dtype)


def _dsa_scratch(nk, tq, tk):
    rows4 = GROUP_A * tq
    return [
        pltpu.VMEM((KV_A, rows4, HEAD_DIM), BF16),
        pltpu.VMEM((H_IDX * tq, D_IDX), BF16),
        pltpu.VMEM((H_IDX, tq, LANES), F32),
        pltpu.VMEM((nk, tq, tk), I32),
        pltpu.VMEM((KV_A, rows4, LANES), F32),
        pltpu.VMEM((KV_A, rows4, HEAD_DIM + LANES), F32),
    ]


def _dsa_prompt_kernel(q_ref, qi_ref, w_ref, ki_ref, kv_ref, gate_ref, bias_ref, o_ref,
                       qs_ref, qih_ref, wb_ref, key_ref, m_ref, acc_ref, *, tq, tk, n_sel, idx_bits):
    i = pl.program_id(1)
    nvis = i + 1
    row_chunk = (i * tq + lax.broadcasted_iota(I32, (tq, tk), 0)) >> CHUNK_SHIFT
    lane = lax.broadcasted_iota(I32, (tq, tk), 1)
    _dsa_prepare(q_ref, qi_ref, w_ref, qs_ref, qih_ref, wb_ref, tq)

    def score_tile(k, carry):
        ks = pl.ds(pl.multiple_of(k * tk, tk), tk)
        visible = ((k * tk + lane) >> CHUNK_SHIFT) <= row_chunk
        key_ref[k] = _dsa_scores_to_keys(qih_ref, wb_ref, ki_ref[ks, :], visible)
        return carry

    lax.fori_loop(0, nvis, score_tile, 0)
    thr = _dsa_threshold(key_ref, nvis, n_sel, idx_bits, tq, tk)

    m_ref[...] = jnp.full(m_ref.shape, NEG, F32)
    acc_ref[...] = jnp.zeros(acc_ref.shape, F32)

    def attend_tile(k, carry):
        ks = pl.ds(pl.multiple_of(k * tk, tk), tk)
        bias_tile = bias_ref[jnp.minimum(i - k, N_BIAS_VARIANTS - 1)]
        kv_of_head = lambda n: (kv_ref[ks, _cols(n)], kv_ref[ks, KVA + HEAD_DIM * n:KVA + HEAD_DIM * (n + 1)])
        _dsa_attend_tile(qs_ref, key_ref[k], thr, bias_tile, kv_of_head, m_ref, acc_ref, tq)
        return carry

    lax.fori_loop(0, nvis, attend_tile, 0)
    _dsa_finish(acc_ref, gate_ref, o_ref, tq)


def dsa_prompt(qcat, kiw, ki, kv, gcat, bias, *, n_sel):
    b, t, _ = qcat.shape
    tq = tk = LANES
    nk = t // tk
    kern = functools.partial(_dsa_prompt_kernel, tq=tq, tk=tk, n_sel=n_sel, idx_bits=int(t).bit_length())
    return pl.pallas_call(
        kern,
        grid=(b, t // tq),
        in_specs=[
            pl.BlockSpec((None, tq, QA), lambda bi, i: (bi, i, 0)),
            pl.BlockSpec((None, tq, QI), lambda bi, i: (bi, i, 2 * QA // QI)),
            pl.BlockSpec((None, tq, LANES), lambda bi, i: (bi, i, 0)),
            _resident((None, t, D_IDX), lambda bi, i: (bi, 0, 0)),
            _resident((None, t, 2 * KVA), lambda bi, i: (bi, 0, 0)),
            pl.BlockSpec((None, tq, QA), lambda bi, i: (bi, i, 0)),
            _resident((N_BIAS_VARIANTS, H_A * tq, tk), lambda bi, i: (0, 0, 0)),
        ],
        out_specs=pl.BlockSpec((None, tq, QA), lambda bi, i: (bi, i, 0)),
        out_shape=jax.ShapeDtypeStruct((b, t, QA), BF16),
        scratch_shapes=_dsa_scratch(nk, tq, tk),
        compiler_params=_cparams("parallel", "arbitrary"),
        name="dsa_prompt",
    )(qcat, qcat, kiw, ki, kv, gcat, bias)


def _dsa_sample_kernel(q_ref, qi_ref, w_ref, kin_ref, kvn_ref, gate_ref, bias_ref, kic_ref, kvc_ref, o_ref,
                       qs_ref, qih_ref, wb_ref, key_ref, m_ref, acc_ref, thr_ref, *, tq, p0, pt, n_sel, idx_bits):
    kt = pl.program_id(1)
    tk = TK_SAMPLE
    n_cache = p0 // tk
    sub = pt // tk

    @pl.when(kt == 0)
    def _():
        new_visible = lax.broadcasted_iota(I32, (tq, tk), 1) < tq
        _dsa_prepare(q_ref, qi_ref, w_ref, qs_ref, qih_ref, wb_ref, tq)

        def score_tile(k, carry):
            ks = pl.ds(pl.multiple_of(k * tk, tk), tk)
            key_ref[k] = _dsa_scores_to_keys(qih_ref, wb_ref, kic_ref[ks, :].astype(BF16), True)
            return carry

        lax.fori_loop(0, n_cache, score_tile, 0)
        ki_new = _pad_new_tile(kin_ref[:, :D_IDX].astype(BF16), tk)
        key_ref[n_cache] = _dsa_scores_to_keys(qih_ref, wb_ref, ki_new, new_visible)
        thr = _dsa_threshold(key_ref, n_cache + 1, n_sel, idx_bits, tq, tk)
        thr_ref[...] = jnp.broadcast_to(thr, thr_ref.shape)
        m_ref[...] = jnp.full(m_ref.shape, NEG, F32)
        acc_ref[...] = jnp.zeros(acc_ref.shape, F32)

    thr = thr_ref[...]

    def attend_tile(j, carry):
        rows = pl.ds(pl.multiple_of(j * tk, tk), tk)
        k = kt * sub + j
        bias_tile = bias_ref[jnp.minimum(n_cache - k, N_BIAS_VARIANTS - 1)]
        kv_of_head = lambda n: (kvc_ref[rows, 0, n, :].astype(BF16), kvc_ref[rows, 1, n, :].astype(BF16))
        _dsa_attend_tile(qs_ref, key_ref[k], thr, bias_tile, kv_of_head, m_ref, acc_ref, tq)
        return carry

    lax.fori_loop(0, sub, attend_tile, 0)

    @pl.when(kt == pl.num_programs(1) - 1)
    def _():
        kv_new = lambda n: (_pad_new_tile(kvn_ref[:, _cols(n)], tk),
                            _pad_new_tile(kvn_ref[:, KVA + HEAD_DIM * n:KVA + HEAD_DIM * (n + 1)], tk))
        _dsa_attend_tile(qs_ref, key_ref[n_cache], thr, bias_ref[0], kv_new, m_ref, acc_ref, tq)
        _dsa_finish(acc_ref, gate_ref, o_ref, tq)


def dsa_sample(qcat, kiw, kv_new, gcat, bias, cache_kidx, cache_kv, layer, *, n_sel):
    b, tq, _ = qcat.shape
    p0 = cache_kv.shape[2]
    tk, pt = TK_SAMPLE, PT_SAMPLE
    assert p0 % pt == 0 and tq <= tk
    nk = p0 // tk + 1
    kern = functools.partial(_dsa_sample_kernel, tq=tq, p0=p0, pt=pt, n_sel=n_sel,
                             idx_bits=int(p0 + tk).bit_length())
    return pl.pallas_call(
        kern,
        grid=(b, p0 // pt),
        in_specs=[
            pl.BlockSpec((None, tq, QA), lambda bi, kt: (bi, 0, 0)),
            pl.BlockSpec((None, tq, QI), lambda bi, kt: (bi, 0, 2 * QA // QI)),
            pl.BlockSpec((None, tq, LANES), lambda bi, kt: (bi, 0, 0)),
            pl.BlockSpec((None, tq, LANES), lambda bi, kt: (bi, 0, 0)),
            pl.BlockSpec((None, tq, 2 * KVA), lambda bi, kt: (bi, 0, 0)),
            pl.BlockSpec((None, tq, QA), lambda bi, kt: (bi, 0, 0)),
            pl.BlockSpec((N_BIAS_VARIANTS, H_A * tq, tk), lambda bi, kt: (0, 0, 0)),
            pl.BlockSpec((None, None, p0, D_IDX), lambda bi, kt: (layer, bi, 0, 0)),
            pl.BlockSpec((None, None, pt, 2, KV_A, HEAD_DIM), lambda bi, kt: (layer, bi, kt, 0, 0, 0)),
        ],
        out_specs=pl.BlockSpec((None, tq, QA), lambda bi, kt: (bi, 0, 0)),
        out_shape=jax.ShapeDtypeStruct((b, tq, QA), BF16),
        scratch_shapes=_dsa_scratch(nk, tq, tk) + [pltpu.VMEM((tq, tk), I32)],
        compiler_params=_cparams("parallel", "arbitrary"),
        name="dsa_sample",
    )(qcat, qcat, kiw, kiw, kv_new, gcat, bias, cache_kidx, cache_kv)


def _diff_prompt_kernel(q_ref, k_ref, v_ref, gate_ref, bias_ref, lam_ref, subln_ref, o_ref,
                        m_ref, acc_ref, *, tq, tk, hb, lam_init):
    i = pl.program_id(2)
    row_chunk = (i * tq + lax.broadcasted_iota(I32, (tq, tk), 0)) >> CHUNK_SHIFT
    lane = lax.broadcasted_iota(I32, (tq, tk), 1)
    m_ref[...] = jnp.full(m_ref.shape, NEG, F32)
    acc_ref[...] = jnp.zeros(acc_ref.shape, F32)

    def step(k, masked):
        ks = pl.ds(pl.multiple_of(k * tk, tk), tk)
        variant = jnp.minimum(i - k, N_BIAS_VARIANTS - 1)
        madd = jnp.where(((k * tk + lane) >> CHUNK_SHIFT) <= row_chunk, 0.0, NEG) if masked else None
        for h in range(hb):
            bias = bias_ref[variant, h]
            if masked:
                bias = bias + madd
            vmat = v_ref[ks, _cols(h, DV_B)]
            for c in range(2):
                col = slice(DV_B * h + HEAD_DIM * c, DV_B * h + HEAD_DIM * (c + 1))
                s = _qk(q_ref[:, col], k_ref[ks, col]) * ATT_SCALE + bias
                m_ref[h, c], acc_ref[h, c] = _softmax_tile(s, vmat, m_ref[h, c], acc_ref[h, c])

    def full_tile(k, carry):
        step(k, False)
        return carry

    lax.fori_loop(0, i, full_tile, 0)
    step(i, True)
    for h in range(hb):
        cs = _cols(h, DV_B)
        o_ref[:, cs] = _diff_finish(acc_ref[h, 0], acc_ref[h, 1], lam_ref, subln_ref, gate_ref[:, cs],
                                    lam_init).astype(o_ref.dtype)


def diff_prompt(qcat, kv, gcat, bias, lam_vec, subln, *, tq, hb, lam_init):
    b, t, _ = qcat.shape
    tk = tq
    w = hb * DV_B
    kern = functools.partial(_diff_prompt_kernel, tq=tq, tk=tk, hb=hb, lam_init=lam_init)
    return pl.pallas_call(
        kern,
        grid=(b, H_B // hb, t // tq),
        in_specs=[
            pl.BlockSpec((None, tq, w), lambda bi, g, i: (bi, i, QA // w + g)),
            _resident((None, t, w), lambda bi, g, i: (bi, 0, g)),
            _resident((None, t, w), lambda bi, g, i: (bi, 0, QA // w + g)),
            pl.BlockSpec((None, tq, w), lambda bi, g, i: (bi, i, QA // w + g)),
            _resident((N_BIAS_VARIANTS, hb, tq, tk), lambda bi, g, i: (0, g, 0, 0)),
            pl.BlockSpec((4, HEAD_DIM), lambda bi, g, i: (0, 0)),
            pl.BlockSpec((1, DV_B), lambda bi, g, i: (0, 0)),
        ],
        out_specs=pl.BlockSpec((None, tq, w), lambda bi, g, i: (bi, i, g)),
        out_shape=jax.ShapeDtypeStruct((b, t, QA), BF16),
        scratch_shapes=[pltpu.VMEM((hb, 2, tq, LANES), F32), pltpu.VMEM((hb, 2, tq, DV_B + LANES), F32)],
        compiler_params=_cparams("parallel", "parallel", "arbitrary"),
        name="diff_prompt",
    )(qcat, kv, kv, gcat, bias, lam_vec, subln.reshape(1, DV_B))


def _diff_sample_kernel(q_ref, kn_ref, vn_ref, gate_ref, bias_ref, lam_ref, subln_ref, kc_ref, vc_ref, o_ref,
                        m_ref, acc_ref, *, tq, p0, pt, lam_init):
    kt = pl.program_id(1)
    tk = TK_SAMPLE
    n_cache = p0 // tk
    sub = pt // tk

    @pl.when(kt == 0)
    def _():
        m_ref[...] = jnp.full(m_ref.shape, NEG, F32)
        acc_ref[...] = jnp.zeros(acc_ref.shape, F32)

    def update(h, kmat2, vmat, bias):
        for c in range(2):
            q = q_ref[:, DV_B * h + HEAD_DIM * c:DV_B * h + HEAD_DIM * (c + 1)]
            s = _qk(q, kmat2[:, _cols(c)]) * ATT_SCALE + bias
            m_ref[h, c], acc_ref[h, c] = _softmax_tile(s, vmat, m_ref[h, c], acc_ref[h, c])

    def cache_tile(j, carry):
        rows = pl.ds(pl.multiple_of(j * tk, tk), tk)
        variant = jnp.minimum(n_cache - (kt * sub + j), N_BIAS_VARIANTS - 1)
        for h in range(H_B):
            update(h, kc_ref[rows, h, :].astype(BF16), vc_ref[rows, h, :].astype(BF16), bias_ref[variant, h])
        return carry

    lax.fori_loop(0, sub, cache_tile, 0)

    @pl.when(kt == pl.num_programs(1) - 1)
    def _():
        new_madd = jnp.where(lax.broadcasted_iota(I32, (tq, tk), 1) < tq, 0.0, NEG)
        for h in range(H_B):
            cs = _cols(h, DV_B)
            update(h, _pad_new_tile(kn_ref[:, cs], tk), _pad_new_tile(vn_ref[:, cs], tk), bias_ref[0, h] + new_madd)
            o_ref[:, cs] = _diff_finish(acc_ref[h, 0], acc_ref[h, 1], lam_ref, subln_ref, gate_ref[:, cs],
                                        lam_init).astype(o_ref.dtype)


def diff_sample(qcat, kv_new, gcat, bias, lam_vec, subln, cache_kv, layer, *, lam_init):
    b, tq, _ = qcat.shape
    p0 = cache_kv.shape[2]
    tk, pt = TK_SAMPLE, PT_SAMPLE
    assert p0 % pt == 0 and tq <= tk and H_B == HB_SAMPLE
    kern = functools.partial(_diff_sample_kernel, tq=tq, p0=p0, pt=pt, lam_init=lam_init)
    cache_spec = lambda sel: pl.BlockSpec((None, None, pt, None, H_B, DV_B),
                                          lambda bi, kt: (layer, bi, kt, sel, 0, 0))
    return pl.pallas_call(
        kern,
        grid=(b, p0 // pt),
        in_specs=[
            pl.BlockSpec((None, tq, QA), lambda bi, kt: (bi, 0, 1)),
            pl.BlockSpec((None, tq, QA), lambda bi, kt: (bi, 0, 0)),
            pl.BlockSpec((None, tq, QA), lambda bi, kt: (bi, 0, 1)),
            pl.BlockSpec((None, tq, QA), lambda bi, kt: (bi, 0, 1)),
            pl.BlockSpec((N_BIAS_VARIANTS, H_B, tq, tk), lambda bi, kt: (0, 0, 0, 0)),
            pl.BlockSpec((4, HEAD_DIM), lambda bi, kt: (0, 0)),
            pl.BlockSpec((1, DV_B), lambda bi, kt: (0, 0)),
            cache_spec(0), cache_spec(1),
        ],
        out_specs=pl.BlockSpec((None, tq, QA), lambda bi, kt: (bi, 0, 0)),
        out_shape=jax.ShapeDtypeStruct((b, tq, QA), BF16),
        scratch_shapes=[pltpu.VMEM((H_B, 2, tq, LANES), F32), pltpu.VMEM((H_B, 2, tq, DV_B + LANES), F32)],
        compiler_params=_cparams("parallel", "arbitrary"),
        name="diff_sample",
    )(qcat, kv_new, kv_new, gcat, bias, lam_vec, subln.reshape(1, DV_B), cache_kv, cache_kv)


def _stick_prompt_kernel(q_ref, k_ref, v_ref, gate_ref, tri_ref, o_ref, acc_ref, carry_ref, *, tq, tk, hb):
    i = pl.program_id(2)
    before = lax.broadcasted_iota(I32, (tq, tk), 1) < lax.broadcasted_iota(I32, (tq, tk), 0)
    acc_ref[...] = jnp.zeros(acc_ref.shape, F32)
    carry_ref[...] = jnp.zeros(carry_ref.shape, F32)

    def step(k, mask):
        ks = pl.ds(pl.multiple_of(k * tk, tk), tk)
        for h in range(hb):
            cs = _cols(h)
            pv, carry_ref[h] = _stick_tile(q_ref[:, cs], k_ref[ks, cs], v_ref[ks, cs], tri_ref,
                                           carry_ref[h], mask)
            acc_ref[h] += pv

    step(i, before)

    def full_tile(j, carry):
        step(i - 1 - j, None)
        return carry

    lax.fori_loop(0, i, full_tile, 0)
    for h in range(hb):
        cs = _cols(h)
        o_ref[:, cs] = (acc_ref[h] * _silu(gate_ref[:, cs])).astype(o_ref.dtype)


def stick_prompt(qcat, kv, gcat, tri_ext, *, tq, hb):
    b, t, _ = qcat.shape
    tk = tq
    w = hb * HEAD_DIM
    kern = functools.partial(_stick_prompt_kernel, tq=tq, tk=tk, hb=hb)
    return pl.pallas_call(
        kern,
        grid=(b, H_C // hb, t // tq),
        in_specs=[
            pl.BlockSpec((None, tq, w), lambda bi, g, i: (bi, i, g)),
            _resident((None, t, w), lambda bi, g, i: (bi, 0, g)),
            _resident((None, t, w), lambda bi, g, i: (bi, 0, QA // w + g)),
            pl.BlockSpec((None, tq, w), lambda bi, g, i: (bi, i, g)),
            _resident((2 * tk, tk + LANES), lambda bi, g, i: (0, 0)),
        ],
        out_specs=pl.BlockSpec((None, tq, w), lambda bi, g, i: (bi, i, g)),
        out_shape=jax.ShapeDtypeStruct((b, t, QA), BF16),
        scratch_shapes=[pltpu.VMEM((hb, tq, HEAD_DIM), F32), pltpu.VMEM((hb, tq, LANES), F32)],
        compiler_params=_cparams("parallel", "parallel", "arbitrary"),
        name="stick_prompt",
    )(qcat, kv, kv, gcat, tri_ext)


def _stick_sample_kernel(q_ref, kn_ref, vn_ref, gate_ref, tri_ref, kc_ref, vc_ref, o_ref, acc_ref, carry_ref,
                         *, tq, pt):
    kt = pl.program_id(2)
    tk = TK_SAMPLE
    sub = pt // tk
    hb = HB_SAMPLE

    @pl.when(kt == 0)
    def _():
        before = lax.broadcasted_iota(I32, (tq, tk), 1) < lax.broadcasted_iota(I32, (tq, tk), 0)
        for h in range(hb):
            cs = _cols(h)
            acc_ref[h], carry_ref[h] = _stick_tile(
                q_ref[:, cs], _pad_new_tile(kn_ref[:, cs], tk), _pad_new_tile(vn_ref[:, cs], tk), tri_ref,
                jnp.zeros((tq, LANES), F32), before)

    def cache_tile(j, carry):
        rows = pl.ds(pl.multiple_of((sub - 1 - j) * tk, tk), tk)
        for h in range(hb):
            pv, carry_ref[h] = _stick_tile(q_ref[:, _cols(h)], kc_ref[rows, h, :].astype(BF16),
                                           vc_ref[rows, h, :].astype(BF16), tri_ref, carry_ref[h], None)
            acc_ref[h] += pv
        return carry

    lax.fori_loop(0, sub, cache_tile, 0)

    @pl.when(kt == pl.num_programs(2) - 1)
    def _():
        for h in range(hb):
            cs = _cols(h)
            o_ref[:, cs] = (acc_ref[h] * _silu(gate_ref[:, cs])).astype(o_ref.dtype)


def stick_sample(qcat, kv_new, gcat, tri_ext, cache_kv, layer):
    b, tq, _ = qcat.shape
    p0 = cache_kv.shape[2]
    pt = PT_SAMPLE
    w = HB_SAMPLE * HEAD_DIM
    assert p0 % pt == 0 and tq <= TK_SAMPLE
    n_kt = p0 // pt
    kern = functools.partial(_stick_sample_kernel, tq=tq, pt=pt)
    cache_spec = lambda sel: pl.BlockSpec((None, None, pt, None, HB_SAMPLE, HEAD_DIM),
                                          lambda bi, g, kt: (layer, bi, n_kt - 1 - kt, sel, g, 0))
    return pl.pallas_call(
        kern,
        grid=(b, H_C // HB_SAMPLE, n_kt),
        in_specs=[
            pl.BlockSpec((None, tq, w), lambda bi, g, kt: (bi, 0, g)),
            pl.BlockSpec((None, tq, w), lambda bi, g, kt: (bi, 0, g)),
            pl.BlockSpec((None, tq, w), lambda bi, g, kt: (bi, 0, QA // w + g)),
            pl.BlockSpec((None, tq, w), lambda bi, g, kt: (bi, 0, g)),
            pl.BlockSpec((2 * TK_SAMPLE, TK_SAMPLE + LANES), lambda bi, g, kt: (0, 0)),
            cache_spec(0), cache_spec(1),
        ],
        out_specs=pl.BlockSpec((None, tq, w), lambda bi, g, kt: (bi, 0, g)),
        out_shape=jax.ShapeDtypeStruct((b, tq, QA), BF16),
        scratch_shapes=[pltpu.VMEM((HB_SAMPLE, tq, HEAD_DIM), F32), pltpu.VMEM((HB_SAMPLE, tq, LANES), F32)],
        compiler_params=_cparams("parallel", "parallel", "arbitrary"),
        name="stick_sample",
    )(qcat, kv_new, kv_new, gcat, tri_ext, cache_kv, cache_kv)


def _band_prompt_kernel(q_ref, k_ref, v_ref, gate_ref, bias_ref, o_ref, m_ref, acc_ref, *, tq, tk, hb):
    i = pl.program_id(2)
    k_lo = jnp.maximum(i - (BAND_CHUNKS * CHUNK) // tk, 0)
    row_chunk = (i * tq + lax.broadcasted_iota(I32, (tq, tk), 0)) >> CHUNK_SHIFT
    lane = lax.broadcasted_iota(I32, (tq, tk), 1)
    m_ref[...] = jnp.full(m_ref.shape, NEG, F32)
    acc_ref[...] = jnp.zeros(acc_ref.shape, F32)

    def step(k, carry):
        ks = pl.ds(pl.multiple_of(k * tk, tk), tk)
        key_chunk = (k * tk + lane) >> CHUNK_SHIFT
        visible = (key_chunk <= row_chunk) & (key_chunk >= row_chunk - BAND_CHUNKS)
        madd = jnp.where(visible, 0.0, NEG)
        variant = jnp.minimum(i - k, N_BIAS_VARIANTS - 1)
        for h in range(hb):
            cs = _cols(h)
            s = _qk(q_ref[:, cs], k_ref[ks, cs]) * ATT_SCALE + bias_ref[variant, h] + madd
            m_ref[h], acc_ref[h] = _softmax_tile(s, v_ref[ks, cs], m_ref[h], acc_ref[h])
        return carry

    lax.fori_loop(k_lo, i + 1, step, 0)
    for h in range(hb):
        cs = _cols(h)
        o_ref[:, cs] = (_softmax_finish(acc_ref[h]) * _silu(gate_ref[:, cs])).astype(o_ref.dtype)


def band_prompt(qcat, kv, gcat, bias, *, hb):
    b, t, _ = qcat.shape
    tq = tk = LANES
    w = hb * HEAD_DIM
    kern = functools.partial(_band_prompt_kernel, tq=tq, tk=tk, hb=hb)
    return pl.pallas_call(
        kern,
        grid=(b, H_D // hb, t // tq),
        in_specs=[
            pl.BlockSpec((None, tq, w), lambda bi, g, i: (bi, i, QA // w + g)),
            _resident((None, t, w), lambda bi, g, i: (bi, 0, g)),
            _resident((None, t, w), lambda bi, g, i: (bi, 0, QA // w + g)),
            pl.BlockSpec((None, tq, w), lambda bi, g, i: (bi, i, QA // w + g)),
            _resident((N_BIAS_VARIANTS, hb, tq, tk), lambda bi, g, i: (0, g, 0, 0)),
        ],
        out_specs=pl.BlockSpec((None, tq, w), lambda bi, g, i: (bi, i, g)),
        out_shape=jax.ShapeDtypeStruct((b, t, QA), BF16),
        scratch_shapes=[pltpu.VMEM((hb, tq, LANES), F32), pltpu.VMEM((hb, tq, HEAD_DIM + LANES), F32)],
        compiler_params=_cparams("parallel", "parallel", "arbitrary"),
        name="band_prompt",
    )(qcat, kv, kv, gcat, bias)


def _band_sample_kernel(q_ref, kn_ref, vn_ref, gate_ref, bias_ref, kc_ref, vc_ref, o_ref, *, tq, win):
    tk = TK_SAMPLE
    n_cache = win // tk
    new_madd = jnp.where(lax.broadcasted_iota(I32, (tq, tk), 1) < tq, 0.0, NEG)
    for h in range(HB_SAMPLE):
        cs = _cols(h)
        q = q_ref[:, cs]
        m, acc = jnp.full((tq, LANES), NEG, F32), jnp.zeros((tq, HEAD_DIM + LANES), F32)
        for k in range(n_cache):
            rows = slice(k * tk, (k + 1) * tk)
            s = _qk(q, kc_ref[rows, h, :].astype(BF16)) * ATT_SCALE + bias_ref[
                min(n_cache - k, N_BIAS_VARIANTS - 1), h]
            m, acc = _softmax_tile(s, vc_ref[rows, h, :].astype(BF16), m, acc)
        s = _qk(q, _pad_new_tile(kn_ref[:, cs], tk)) * ATT_SCALE + bias_ref[0, h] + new_madd
        m, acc = _softmax_tile(s, _pad_new_tile(vn_ref[:, cs], tk), m, acc)
        o_ref[:, cs] = (_softmax_finish(acc) * _silu(gate_ref[:, cs])).astype(o_ref.dtype)


def band_sample(qcat, kv_new, gcat, bias, cache_kv, layer):
    b, tq, _ = qcat.shape
    win = cache_kv.shape[2]
    w = HB_SAMPLE * HEAD_DIM
    assert win == BAND_CHUNKS * CHUNK and win % TK_SAMPLE == 0 and tq == CHUNK
    kern = functools.partial(_band_sample_kernel, tq=tq, win=win)
    cache_spec = lambda sel: pl.BlockSpec((None, None, win, None, HB_SAMPLE, HEAD_DIM),
                                          lambda bi, g: (layer, bi, 0, sel, g, 0))
    return pl.pallas_call(
        kern,
        grid=(b, H_D // HB_SAMPLE),
        in_specs=[
            pl.BlockSpec((None, tq, w), lambda bi, g: (bi, 0, QA // w + g)),
            pl.BlockSpec((None, tq, w), lambda bi, g: (bi, 0, g)),
            pl.BlockSpec((None, tq, w), lambda bi, g: (bi, 0, QA // w + g)),
            pl.BlockSpec((None, tq, w), lambda bi, g: (bi, 0, QA // w + g)),
            pl.BlockSpec((N_BIAS_VARIANTS, HB_SAMPLE, tq, TK_SAMPLE), lambda bi, g: (0, g, 0, 0)),
            cache_spec(0), cache_spec(1),
        ],
        out_specs=pl.BlockSpec((None, tq, w), lambda bi, g: (bi, 0, g)),
        out_shape=jax.ShapeDtypeStruct((b, tq, QA), BF16),
        compiler_params=_cparams("parallel", "parallel"),
        name="band_sample",
    )(qcat, kv_new, kv_new, gcat, bias, cache_kv, cache_kv)


def _t5_bucket_np(rel):
    half = T5_BUCKETS // 2
    max_exact = half // 2
    n = np.abs(rel)
    nf = np.maximum(n, 1).astype(np.float64)
    large = max_exact + (np.log(nf / max_exact) / math.log(T5_MAX_DIST / max_exact)
                         * (half - max_exact)).astype(np.int32)
    large = np.minimum(large, half - 1)
    return np.where(rel < 0, half, 0) + np.where(n < max_exact, n, large)


def _band_index_np(rel):
    return np.clip(rel, -REL_CLIP, REL_CLIP) + REL_CLIP


def bias_tiles(table, index_of_rel, tq, tk):
    rows = (N_BIAS_VARIANTS - 1) * tk + tq
    n = tk + rows
    m = np.arange(n)
    rel_of_m = np.where(m < tk, -m, n - m)
    assert (index_of_rel(np.arange(tk + 1, rows)) == index_of_rel(np.array(rows))).all()
    u = table[index_of_rel(rel_of_m).astype(np.int32)].T
    h = u.shape[0]
    circ = jnp.tile(u, (1, rows))[:, :rows * (n - 1)].reshape(h, rows, n - 1)[:, :, :tk]
    return jnp.stack([circ[:, v * tk:v * tk + tq] for v in range(N_BIAS_VARIANTS)], axis=0)


def _tri_ext(tk):
    tri = np.tril(np.ones((tk, tk), np.float32), -1)
    ext = np.concatenate([tri, np.ones((tk, LANES), np.float32)], axis=1)
    return jnp.asarray(np.concatenate([ext, ext], axis=0), BF16)


def _even_weights(w_in):
    c = np.cumsum((0, QA, KVA, KVA, QI, D_IDX, H_IDX, QA, QA, QA, QA, QA)).tolist()
    aq, ak, av, aqi, aki, aw, ag, bq, bk, bv, bg = (w_in[:, c[r]:c[r + 1]] for r in range(11))
    pad = jnp.zeros((w_in.shape[0], LANES - D_IDX - H_IDX), w_in.dtype)
    cat = lambda *xs: jnp.concatenate(xs, axis=1).astype(BF16)
    return cat(aq, bq, aqi), cat(ak, av), cat(bk, bv), cat(aki, aw, pad), cat(ag, bg)


def _odd_weights(w_in):
    cq, ck, cv, cg, dq, dk, dv, dg = (w_in[:, QA * r:QA * (r + 1)] for r in range(8))
    cat = lambda *xs: jnp.concatenate(xs, axis=1).astype(BF16)
    return cat(cq, dq), cat(ck, cv), cat(dk, dv), cat(cg, dg)


def _even_layer(h, caches, layer, weights, t5_tab, lam_vec, subln, lam_init):
    w_q, w_akv, w_bkv, w_kiw, w_g = weights
    b, t, d = h.shape
    h2 = h.reshape(b * t, d)
    (qcat,) = matmul(h2, w_q, (BF16,))
    akv, akv_b = matmul(h2, w_akv, (F32, BF16))
    bkv, bkv_b = matmul(h2, w_bkv, (F32, BF16))
    (kiw,) = matmul(h2, w_kiw, (F32,))
    (gcat,) = matmul(h2, w_g, (F32,))
    qcat, akv_b, bkv_b, kiw, gcat = (x.reshape(b, t, -1) for x in (qcat, akv_b, bkv_b, kiw, gcat))
    aki = kiw[..., :D_IDX]
    t5_a, t5_b = t5_tab[:, :H_A], t5_tab[:, H_A:]
    if caches is None:
        tq_b = 256
        bias_a = bias_tiles(t5_a, _t5_bucket_np, LANES, LANES).reshape(N_BIAS_VARIANTS, H_A * LANES, LANES)
        o_a = dsa_prompt(qcat, kiw, aki.astype(BF16), akv_b, gcat, bias_a, n_sel=min(TOPK_MAX, t // 4))
        o_b = diff_prompt(qcat, bkv_b, gcat, bias_tiles(t5_b, _t5_bucket_np, tq_b, tq_b), lam_vec, subln,
                          tq=tq_b, hb=2, lam_init=lam_init)
    else:
        cache_a, cache_ki, cache_b = caches
        p0 = cache_a.shape[2]
        bias_a = bias_tiles(t5_a, _t5_bucket_np, t, TK_SAMPLE).reshape(N_BIAS_VARIANTS, H_A * t, TK_SAMPLE)
        o_a = dsa_sample(qcat, kiw, akv_b, gcat, bias_a, cache_ki, cache_a, layer,
                         n_sel=min(TOPK_MAX, (p0 + t) // 4))
        o_b = diff_sample(qcat, bkv_b, gcat, bias_tiles(t5_b, _t5_bucket_np, t, TK_SAMPLE), lam_vec, subln,
                          cache_b, layer, lam_init=lam_init)
    mixed = jnp.concatenate([o_a, o_b], axis=-1).reshape(b * t, 2 * QA)
    return mixed, akv.reshape(b, t, -1), aki, bkv.reshape(b, t, -1)


def _odd_layer(h, caches, layer, weights, rel_tab):
    w_q, w_ckv, w_dkv, w_g = weights
    b, t, d = h.shape
    h2 = h.reshape(b * t, d)
    (qcat,) = matmul(h2, w_q, (BF16,))
    ckv, ckv_b = matmul(h2, w_ckv, (F32, BF16))
    dkv, dkv_b = matmul(h2, w_dkv, (F32, BF16))
    (gcat,) = matmul(h2, w_g, (F32,))
    qcat, ckv_b, dkv_b, gcat = (x.reshape(b, t, -1) for x in (qcat, ckv_b, dkv_b, gcat))
    if caches is None:
        tq_c = 256
        o_c = stick_prompt(qcat, ckv_b, gcat, _tri_ext(tq_c), tq=tq_c, hb=2)
        o_d = band_prompt(qcat, dkv_b, gcat, bias_tiles(rel_tab, _band_index_np, LANES, LANES), hb=4)
    else:
        cache_c, cache_d = caches
        o_c = stick_sample(qcat, ckv_b, gcat, _tri_ext(TK_SAMPLE), cache_c, layer)
        o_d = band_sample(qcat, dkv_b, gcat, bias_tiles(rel_tab, _band_index_np, t, TK_SAMPLE), cache_d, layer)
    mixed = jnp.concatenate([o_c, o_d], axis=-1).reshape(b * t, 2 * QA)
    return mixed, ckv.reshape(b, t, -1), dkv.reshape(b, t, -1)


def _finish_layer(x, mixed, w_out, p_i, g_post, w_proj, g_pl, w_gate):
    (mix,) = matmul(mixed, w_out, (F32,))
    x1, x1_b = residual_norm(x, mix, g_post)
    e = embed_norm(p_i.astype(BF16), w_proj, g_pl)
    return gate_matmul(x1_b, w_gate, x1, e)


def kernel(x_prompt, x_sample, p_prompt, p_sample, cache_a_kv, cache_a_kidx, cache_b_kv, cache_c_kv,
           cache_d_kv, norm_pre, norm_post, w_in_even, w_out_even, t5_bias, diff_lambda, diff_subln,
           w_in_odd, w_out_odd, d_rel_bias, w_pl_proj, pl_norm, w_pl_gate):
    bp, tp, d = x_prompt.shape
    bs, ts, _ = x_sample.shape
    depth = norm_pre.shape[0]
    assert d == D_MODEL and ts == CHUNK and tp % 256 == 0
    xp = x_prompt.reshape(bp * tp, d)
    xs = x_sample.reshape(bs * ts, d)
    outs = {name: [] for name in ("a_kv_p", "a_kv_s", "a_ki_p", "a_ki_s", "b_kv_p", "b_kv_s",
                                  "c_kv_p", "c_kv_s", "d_kv_p", "d_kv_s")}
    for i in range(depth):
        j = i // 2
        hp = rmsnorm_cast(xp, norm_pre[i]).reshape(bp, tp, d)
        hs = rmsnorm_cast(xs, norm_pre[i]).reshape(bs, ts, d)
        if i % 2 == 0:
            lam_init = 0.8 - 0.6 * math.exp(-0.3 * i)
            weights = _even_weights(w_in_even[j])
            w_out = w_out_even[j].astype(BF16)
            prm = (weights, t5_bias, diff_lambda[j], diff_subln[j], lam_init)
            mp, akv, aki, bkv = _even_layer(hp, None, j, *prm)
            outs["a_kv_p"].append(akv.reshape(bp, tp, 2, KV_A, HEAD_DIM))
            outs["a_ki_p"].append(aki)
            outs["b_kv_p"].append(bkv.reshape(bp, tp, 2, H_B, DV_B))
            ms, akv, aki, bkv = _even_layer(hs, (cache_a_kv, cache_a_kidx, cache_b_kv), j, *prm)
            outs["a_kv_s"].append(akv.reshape(bs, ts, 2, KV_A, HEAD_DIM))
            outs["a_ki_s"].append(aki)
            outs["b_kv_s"].append(bkv.reshape(bs, ts, 2, H_B, DV_B))
        else:
            weights = _odd_weights(w_in_odd[j])
            w_out = w_out_odd[j].astype(BF16)
            mp, ckv, dkv = _odd_layer(hp, None, j, weights, d_rel_bias[j])
            win_p = min(BAND_CHUNKS * CHUNK, tp)
            outs["c_kv_p"].append(ckv.reshape(bp, tp, 2, H_C, HEAD_DIM))
            outs["d_kv_p"].append(dkv[:, tp - win_p:].reshape(bp, win_p, 2, H_D, HEAD_DIM))
            ms, ckv, dkv = _odd_layer(hs, (cache_c_kv, cache_d_kv), j, weights, d_rel_bias[j])
            outs["c_kv_s"].append(ckv.reshape(bs, ts, 2, H_C, HEAD_DIM))
            full_d = jnp.concatenate([cache_d_kv[j], dkv.reshape(bs, ts, 2, H_D, HEAD_DIM)], axis=1)
            outs["d_kv_s"].append(full_d[:, ts:])
        fin = (norm_post[i], w_pl_proj[i].astype(BF16), pl_norm[i], w_pl_gate[i].astype(BF16))
        xp = _finish_layer(xp, mp, w_out, p_prompt[i].reshape(bp * tp, -1), *fin)
        xs = _finish_layer(xs, ms, w_out, p_sample[i].reshape(bs * ts, -1), *fin)
    st = lambda name: jnp.stack(outs[name])
    return (xp.reshape(bp, tp, d), xs.reshape(bs, ts, d), st("a_kv_p"), st("a_kv_s"), st("a_ki_p"), st("a_ki_s"),
            st("b_kv_p"), st("b_kv_s"), st("c_kv_p"), st("c_kv_s"), st("d_kv_p"), st("d_kv_s"))
```

```python
import functools
import math

import numpy as np
import jax
import jax.numpy as jnp
from jax import lax
from jax.experimental import pallas as pl
from jax.experimental.pallas import tpu as pltpu

F32 = jnp.float32
BF16 = jnp.bfloat16
I32 = jnp.int32

D_MODEL = 4096
CHUNK = 64
CHUNK_SHIFT = 6
HEAD_DIM = 128
H_A = 16
KV_A = 4
GROUP_A = H_A // KV_A
H_IDX = 16
D_IDX = 64
TOPK_MAX = 256
H_B = 8
H_C = 16
H_D = 16
BAND_CHUNKS = 8
REL_CLIP = 128
T5_BUCKETS = 32
T5_MAX_DIST = 128
RMS_EPS = 1e-6
QA = H_A * HEAD_DIM
KVA = KV_A * HEAD_DIM
QI = H_IDX * D_IDX
DV_B = 2 * HEAD_DIM
ATT_SCALE = HEAD_DIM ** -0.5
IDX_SCALE = (D_IDX ** -0.5) * (H_IDX ** -0.5)

LANES = 128
NEG = -1e30
INT_MIN = -(2 ** 31)
VMEM_LIMIT = 56 * 1024 * 1024
N_BIAS_VARIANTS = 3
TK_SAMPLE = LANES
PT_SAMPLE = 512
HB_SAMPLE = 8
DSA_TILES_PER_STEP = 4


def _cparams(*sem):
    return pltpu.CompilerParams(dimension_semantics=sem, vmem_limit_bytes=VMEM_LIMIT)


def _sigmoid(x):
    return 1.0 / (1.0 + jnp.exp(-x))


def _silu(x):
    return x * _sigmoid(x)


def _lane_tile(x, n):
    return x if n == 1 else jnp.concatenate([x] * n, axis=1)


def _cols(h, width=HEAD_DIM):
    return slice(width * h, width * (h + 1))


def _rmsnorm_cast_kernel(x_ref, g_ref, o_ref):
    x = x_ref[...]
    ms = jnp.mean(x * x, axis=-1, keepdims=True)
    o_ref[...] = (x * lax.rsqrt(ms + RMS_EPS) * g_ref[...]).astype(o_ref.dtype)


def rmsnorm_cast(x, g, bm=256):
    m, d = x.shape
    bm = min(bm, m)
    return pl.pallas_call(
        _rmsnorm_cast_kernel,
        grid=(m // bm,),
        in_specs=[pl.BlockSpec((bm, d), lambda i: (i, 0)), pl.BlockSpec((1, d), lambda i: (0, 0))],
        out_specs=pl.BlockSpec((bm, d), lambda i: (i, 0)),
        out_shape=jax.ShapeDtypeStruct((m, d), BF16),
        compiler_params=_cparams("parallel"),
        name="rmsnorm_cast",
    )(x, g.reshape(1, d))


def _residual_norm_kernel(x_ref, mix_ref, g_ref, o_ref, ob_ref):
    mix = mix_ref[...]
    ms = jnp.mean(mix * mix, axis=-1, keepdims=True)
    x1 = x_ref[...] + mix * lax.rsqrt(ms + RMS_EPS) * g_ref[...]
    o_ref[...] = x1
    ob_ref[...] = x1.astype(BF16)


def residual_norm(x, mix, g, bm=256):
    m, d = x.shape
    bm = min(bm, m)
    row = pl.BlockSpec((bm, d), lambda i: (i, 0))
    return pl.pallas_call(
        _residual_norm_kernel,
        grid=(m // bm,),
        in_specs=[row, row, pl.BlockSpec((1, d), lambda i: (0, 0))],
        out_specs=[row, row],
        out_shape=[jax.ShapeDtypeStruct((m, d), F32), jax.ShapeDtypeStruct((m, d), BF16)],
        compiler_params=_cparams("parallel"),
        name="residual_norm",
    )(x, mix, g.reshape(1, d))


def _embed_norm_kernel(p_ref, w_ref, g_ref, o_ref):
    y = jnp.dot(p_ref[...], w_ref[...], preferred_element_type=F32)
    ms = jnp.mean(y * y, axis=-1, keepdims=True)
    o_ref[...] = y * lax.rsqrt(ms + RMS_EPS) * g_ref[...]


def embed_norm(p, w, g, bm=256):
    m, k = p.shape
    d = w.shape[1]
    bm = min(bm, m)
    return pl.pallas_call(
        _embed_norm_kernel,
        grid=(m // bm,),
        in_specs=[pl.BlockSpec((bm, k), lambda i: (i, 0)), pl.BlockSpec((k, d), lambda i: (0, 0)),
                  pl.BlockSpec((1, d), lambda i: (0, 0))],
        out_specs=pl.BlockSpec((bm, d), lambda i: (i, 0)),
        out_shape=jax.ShapeDtypeStruct((m, d), F32),
        compiler_params=_cparams("parallel"),
        name="embed_norm",
    )(p, w, g.reshape(1, d))


def _mm_kernel(a_ref, w_ref, *o_refs):
    acc = jnp.dot(a_ref[...], w_ref[...], preferred_element_type=F32)
    for o_ref in o_refs:
        o_ref[...] = acc.astype(o_ref.dtype)


def matmul(a, w, out_dtypes, bm=1024, bn=1024):
    m, k = a.shape
    n = w.shape[1]
    bm, bn = min(bm, m), min(bn, n)
    o_spec = pl.BlockSpec((bm, bn), lambda i, j: (i, j))
    outs = pl.pallas_call(
        _mm_kernel,
        grid=(m // bm, n // bn),
        in_specs=[pl.BlockSpec((bm, k), lambda i, j: (i, 0)), pl.BlockSpec((k, bn), lambda i, j: (0, j))],
        out_specs=[o_spec] * len(out_dtypes),
        out_shape=[jax.ShapeDtypeStruct((m, n), dt) for dt in out_dtypes],
        compiler_params=_cparams("parallel", "parallel"),
        name="matmul",
    )(a, w)
    return outs


def _gate_mm_kernel(a_ref, w_ref, x_ref, e_ref, o_ref):
    logits = jnp.dot(a_ref[...], w_ref[...], preferred_element_type=F32)
    o_ref[...] = x_ref[...] + e_ref[...] * _sigmoid(logits)


def gate_matmul(a, w, x, e, bm=512, bn=1024):
    m, k = a.shape
    n = w.shape[1]
    bm, bn = min(bm, m), min(bn, n)
    tile = pl.BlockSpec((bm, bn), lambda i, j: (i, j))
    return pl.pallas_call(
        _gate_mm_kernel,
        grid=(m // bm, n // bn),
        in_specs=[pl.BlockSpec((bm, k), lambda i, j: (i, 0)), pl.BlockSpec((k, bn), lambda i, j: (0, j)),
                  tile, tile],
        out_specs=tile,
        out_shape=jax.ShapeDtypeStruct((m, n), F32),
        compiler_params=_cparams("parallel", "parallel"),
        name="gate_matmul",
    )(a, w, x, e)


def _resident(block_shape, index_map):
    return pl.BlockSpec(block_shape, index_map, pipeline_mode=pl.Buffered(1))


def _qk(q, kmat):
    return lax.dot_general(q, kmat, (((1,), (1,)), ((), ())), preferred_element_type=F32)


def _softmax_tile(s, vmat, m_prev, acc_prev, mxu_rowsum=True):
    tk = s.shape[1]
    m_new = jnp.maximum(m_prev, jnp.max(s, axis=1, keepdims=True))
    alpha = jnp.exp(m_prev - m_new)
    p = jnp.exp(s - _lane_tile(m_new, tk // LANES))
    if mxu_rowsum:
        v_ext = jnp.concatenate([vmat, jnp.ones((tk, LANES), BF16)], axis=1)
        pv = jnp.dot(p.astype(BF16), v_ext, preferred_element_type=F32)
    else:
        row = jnp.broadcast_to(jnp.sum(p, axis=1, keepdims=True), m_prev.shape)
        pv = jnp.concatenate([jnp.dot(p.astype(BF16), vmat, preferred_element_type=F32), row], axis=1)
    return m_new, _lane_tile(alpha, pv.shape[1] // LANES) * acc_prev + pv


def _softmax_finish(acc):
    dv = acc.shape[1] - LANES
    return acc[:, :dv] / _lane_tile(acc[:, dv:], dv // LANES)


def _stick_tile(q, kmat, vmat, tri, carry, before):
    tk = kmat.shape[0]
    z = _qk(q, kmat) * ATT_SCALE
    softplus = jnp.maximum(z, 0.0) + jnp.log(1.0 + jnp.exp(-jnp.abs(z)))
    log_fail = -softplus
    hi = log_fail.astype(BF16)
    if before is None:
        sums = jnp.dot(hi, tri, preferred_element_type=F32)
    else:
        log_fail = jnp.where(before, log_fail, 0.0)
        hi = log_fail.astype(BF16)
        lo = (log_fail - hi.astype(F32)).astype(BF16)
        sums = jnp.dot(jnp.concatenate([hi, lo], axis=1), tri, preferred_element_type=F32)
    w = jnp.exp((z - softplus) + sums[:, :tk] + _lane_tile(carry, tk // LANES))
    if before is not None:
        w = jnp.where(before, w, 0.0)
    pv = jnp.dot(w.astype(BF16), vmat, preferred_element_type=F32)
    return pv, carry + sums[:, tk:]


def _diff_finish(acc0, acc1, lam_ref, subln_ref, gate, lam_init):
    lam_q1, lam_k1, lam_q2, lam_k2 = (lam_ref[r:r + 1, :] for r in range(4))
    lam = (jnp.exp(jnp.sum(lam_q1 * lam_k1, axis=1, keepdims=True))
           - jnp.exp(jnp.sum(lam_q2 * lam_k2, axis=1, keepdims=True)) + lam_init)
    o = _softmax_finish(acc0) - lam * _softmax_finish(acc1)
    ms = jnp.mean(o * o, axis=-1, keepdims=True)
    y = o * lax.rsqrt(ms + RMS_EPS) * subln_ref[...] * (1.0 - lam_init)
    return y * _silu(gate)


def _pad_new_tile(x, tk):
    return jnp.concatenate([x, jnp.zeros((tk - x.shape[0], x.shape[1]), x.dtype)], axis=0)


def _dsa_scores_to_keys(qih_ref, wb_ref, ki_tile, visible):
    tq = wb_ref.shape[1]
    s_all = _qk(qih_ref[...], ki_tile)
    acc = jnp.zeros((tq, ki_tile.shape[0]), F32)
    for h in range(H_IDX):
        acc = acc + jnp.maximum(s_all[h * tq:(h + 1) * tq], 0.0) * wb_ref[h]
    bits = pltpu.bitcast(acc, I32)
    key = bits ^ ((bits >> 31) & 0x7FFFFFFF)
    return jnp.where(visible, key, INT_MIN)


def _dsa_prepare(q_ref, qi_ref, w_ref, qs_ref, qih_ref, wb_ref, tq):
    wv = w_ref[:, D_IDX:D_IDX + H_IDX] * IDX_SCALE
    for h in range(H_IDX):
        qih_ref[h * tq:(h + 1) * tq, :] = qi_ref[:, _cols(h, D_IDX)]
        wb_ref[h] = jnp.broadcast_to(wv[:, h:h + 1], (tq, LANES))
    for n in range(KV_A):
        for g in range(GROUP_A):
            qs_ref[n, g * tq:(g + 1) * tq, :] = q_ref[:, _cols(GROUP_A * n + g)]


def _dsa_threshold(key_ref, nvis, n_sel, idx_bits, tq, tk):
    lane = lax.broadcasted_iota(I32, (tq, tk), 1)

    def count_where(pred):
        def body(k, cnt):
            return cnt + jnp.where(pred(k, key_ref[k]), 1.0, 0.0)
        cnt = lax.fori_loop(0, nvis, body, jnp.zeros((tq, tk), F32))
        return jnp.sum(cnt, axis=1, keepdims=True)

    def count_ge(cand):
        return count_where(lambda k, key: key >= cand)

    def thr_bit(b, t):
        cand = t + lax.shift_left(jnp.int32(1), 31 - b)
        return jnp.where(count_ge(cand) >= n_sel, cand, t)

    thr = lax.fori_loop(0, 32, thr_bit, jnp.full((tq, 1), INT_MIN, I32))
    thr = jnp.maximum(thr, INT_MIN + 1)

    @pl.when(jnp.max(count_ge(thr)) > n_sel)
    def _():
        need = n_sel - count_ge(thr + 1)

        def count_eq_below(c):
            return count_where(lambda k, key: (key == thr) & ((k * tk + lane) < c))

        def idx_bit(b, c):
            cand = c + lax.shift_left(jnp.int32(1), idx_bits - 1 - b)
            return jnp.where(count_eq_below(cand) <= need, cand, c)

        cut = lax.fori_loop(0, idx_bits, idx_bit, jnp.zeros((tq, 1), I32))

        def demote(k, carry):
            key = key_ref[k]
            key_ref[k] = jnp.where((key == thr) & ((k * tk + lane) >= cut), key - 1, key)
            return carry

        lax.fori_loop(0, nvis, demote, 0)

    return thr


def _dsa_attend_tiles(qs_ref, key_tiles, thr, bias_tiles, kv_of_head, m_ref, acc_ref, tq):
    rows4 = GROUP_A * tq
    madd = jnp.concatenate([jnp.where(key_tile >= thr, 0.0, NEG) for key_tile in key_tiles], axis=1)
    madd4 = jnp.concatenate([madd] * GROUP_A, axis=0)
    loaded = [(qs_ref[n], *kv_of_head(n), m_ref[n], acc_ref[n]) for n in range(KV_A)]
    new = []
    for n, (q, kmat, vmat, m_prev, acc_prev) in enumerate(loaded):
        bias = jnp.concatenate([b[rows4 * n:rows4 * (n + 1), :] for b in bias_tiles], axis=1)
        s = _qk(q, kmat) * ATT_SCALE + bias + madd4
        new.append(_softmax_tile(s, vmat, m_prev, acc_prev))
    for n, (m_new, acc_new) in enumerate(new):
        m_ref[n], acc_ref[n] = m_new, acc_new


def _dsa_finish(acc_ref, gate_ref, o_ref, tq):
    for n in range(KV_A):
        o = _softmax_finish(acc_ref[n])
        for g in range(GROUP_A):
            cs = _cols(GROUP_A * n + g)
            o_ref[:, cs] = (o[g * tq:(g + 1) * tq] * _silu(gate_ref[:, cs])).astype(o_ref.dtype)


def _dsa_scratch(nk, tq, tk):
    rows4 = GROUP_A * tq
    return [
        pltpu.VMEM((KV_A, rows4, HEAD_DIM), BF16),
        pltpu.VMEM((H_IDX * tq, D_IDX), BF16),
        pltpu.VMEM((H_IDX, tq, LANES), F32),
        pltpu.VMEM((nk, tq, tk), I32),
        pltpu.VMEM((KV_A, rows4, LANES), F32),
        pltpu.VMEM((KV_A, rows4, HEAD_DIM + LANES), F32),
    ]


def _dsa_prompt_kernel(q_ref, qi_ref, w_ref, ki_ref, kv_ref, gate_ref, bias_ref, o_ref,
                       qs_ref, qih_ref, wb_ref, key_ref, m_ref, acc_ref, *, tq, tk, group, n_sel, idx_bits):
    i = pl.program_id(1)
    nvis = i + 1
    row_chunk = (i * tq + lax.broadcasted_iota(I32, (tq, tk), 0)) >> CHUNK_SHIFT
    lane = lax.broadcasted_iota(I32, (tq, tk), 1)
    _dsa_prepare(q_ref, qi_ref, w_ref, qs_ref, qih_ref, wb_ref, tq)

    def score_tile(k, carry):
        ks = pl.ds(pl.multiple_of(k * tk, tk), tk)
        visible = ((k * tk + lane) >> CHUNK_SHIFT) <= row_chunk
        key_ref[k] = _dsa_scores_to_keys(qih_ref, wb_ref, ki_ref[ks, :], visible)
        return carry

    lax.fori_loop(0, nvis, score_tile, 0)
    thr = _dsa_threshold(key_ref, nvis, n_sel, idx_bits, tq, tk)
    for g in range(1, group):
        key_ref[i + g] = jnp.full((tq, tk), INT_MIN, I32)

    m_ref[...] = jnp.full(m_ref.shape, NEG, F32)
    acc_ref[...] = jnp.zeros(acc_ref.shape, F32)

    def attend_group(j, carry):
        k0 = j * group
        ks = pl.ds(pl.multiple_of(k0 * tk, group * tk), group * tk)
        key_tiles = [key_ref[k0 + g] for g in range(group)]
        bias_tiles = [bias_ref[jnp.clip(i - k0 - g, 0, N_BIAS_VARIANTS - 1)] for g in range(group)]
        kv_of_head = lambda n: (kv_ref[ks, _cols(n)], kv_ref[ks, KVA + HEAD_DIM * n:KVA + HEAD_DIM * (n + 1)])
        _dsa_attend_tiles(qs_ref, key_tiles, thr, bias_tiles, kv_of_head, m_ref, acc_ref, tq)
        return carry

    lax.fori_loop(0, i // group + 1, attend_group, 0)
    _dsa_finish(acc_ref, gate_ref, o_ref, tq)


def dsa_prompt(qcat, kiw, ki, kv, gcat, bias, *, n_sel):
    b, t, _ = qcat.shape
    tq = tk = LANES
    group = DSA_TILES_PER_STEP
    nk = t // tk
    assert nk % group == 0
    kern = functools.partial(_dsa_prompt_kernel, tq=tq, tk=tk, group=group, n_sel=n_sel,
                             idx_bits=int(t).bit_length())
    return pl.pallas_call(
        kern,
        grid=(b, t // tq),
        in_specs=[
            pl.BlockSpec((None, tq, QA), lambda bi, i: (bi, i, 0)),
            pl.BlockSpec((None, tq, QI), lambda bi, i: (bi, i, 2 * QA // QI)),
            pl.BlockSpec((None, tq, LANES), lambda bi, i: (bi, i, 0)),
            _resident((None, t, D_IDX), lambda bi, i: (bi, 0, 0)),
            _resident((None, t, 2 * KVA), lambda bi, i: (bi, 0, 0)),
            pl.BlockSpec((None, tq, QA), lambda bi, i: (bi, i, 0)),
            _resident((N_BIAS_VARIANTS, H_A * tq, tk), lambda bi, i: (0, 0, 0)),
        ],
        out_specs=pl.BlockSpec((None, tq, QA), lambda bi, i: (bi, i, 0)),
        out_shape=jax.ShapeDtypeStruct((b, t, QA), BF16),
        scratch_shapes=_dsa_scratch(nk + group, tq, tk),
        compiler_params=_cparams("parallel", "arbitrary"),
        name="dsa_prompt",
    )(qcat, qcat, kiw, ki, kv, gcat, bias)


def _dsa_sample_kernel(q_ref, qi_ref, w_ref, kin_ref, kvn_ref, gate_ref, bias_ref, kic_ref, kvc_ref, o_ref,
                       qs_ref, qih_ref, wb_ref, key_ref, m_ref, acc_ref, thr_ref, *, tq, p0, pt, n_sel, idx_bits):
    kt = pl.program_id(1)
    tk = TK_SAMPLE
    n_cache = p0 // tk
    sub = pt // tk

    @pl.when(kt == 0)
    def _():
        new_visible = lax.broadcasted_iota(I32, (tq, tk), 1) < tq
        _dsa_prepare(q_ref, qi_ref, w_ref, qs_ref, qih_ref, wb_ref, tq)

        def score_tile(k, carry):
            ks = pl.ds(pl.multiple_of(k * tk, tk), tk)
            key_ref[k] = _dsa_scores_to_keys(qih_ref, wb_ref, kic_ref[ks, :].astype(BF16), True)
            return carry

        lax.fori_loop(0, n_cache, score_tile, 0)
        ki_new = _pad_new_tile(kin_ref[:, :D_IDX].astype(BF16), tk)
        key_ref[n_cache] = _dsa_scores_to_keys(qih_ref, wb_ref, ki_new, new_visible)
        thr = _dsa_threshold(key_ref, n_cache + 1, n_sel, idx_bits, tq, tk)
        thr_ref[...] = jnp.broadcast_to(thr, thr_ref.shape)
        m_ref[...] = jnp.full(m_ref.shape, NEG, F32)
        acc_ref[...] = jnp.zeros(acc_ref.shape, F32)

    thr = thr_ref[...]
    k0 = kt * sub
    key_tiles = [key_ref[k0 + j] for j in range(sub)]
    bias_tiles = [bias_ref[jnp.minimum(n_cache - k0 - j, N_BIAS_VARIANTS - 1)] for j in range(sub)]
    kv_cached = lambda n: (kvc_ref[:, 0, n, :].astype(BF16), kvc_ref[:, 1, n, :].astype(BF16))
    _dsa_attend_tiles(qs_ref, key_tiles, thr, bias_tiles, kv_cached, m_ref, acc_ref, tq)

    @pl.when(kt == pl.num_programs(1) - 1)
    def _():
        kv_new = lambda n: (_pad_new_tile(kvn_ref[:, _cols(n)], tk),
                            _pad_new_tile(kvn_ref[:, KVA + HEAD_DIM * n:KVA + HEAD_DIM * (n + 1)], tk))
        _dsa_attend_tiles(qs_ref, [key_ref[n_cache]], thr, [bias_ref[0]], kv_new, m_ref, acc_ref, tq)
        _dsa_finish(acc_ref, gate_ref, o_ref, tq)


def dsa_sample(qcat, kiw, kv_new, gcat, bias, cache_kidx, cache_kv, layer, *, n_sel):
    b, tq, _ = qcat.shape
    p0 = cache_kv.shape[2]
    tk, pt = TK_SAMPLE, PT_SAMPLE
    assert p0 % pt == 0 and tq <= tk
    nk = p0 // tk + 1
    kern = functools.partial(_dsa_sample_kernel, tq=tq, p0=p0, pt=pt, n_sel=n_sel,
                             idx_bits=int(p0 + tk).bit_length())
    return pl.pallas_call(
        kern,
        grid=(b, p0 // pt),
        in_specs=[
            pl.BlockSpec((None, tq, QA), lambda bi, kt: (bi, 0, 0)),
            pl.BlockSpec((None, tq, QI), lambda bi, kt: (bi, 0, 2 * QA // QI)),
            pl.BlockSpec((None, tq, LANES), lambda bi, kt: (bi, 0, 0)),
            pl.BlockSpec((None, tq, LANES), lambda bi, kt: (bi, 0, 0)),
            pl.BlockSpec((None, tq, 2 * KVA), lambda bi, kt: (bi, 0, 0)),
            pl.BlockSpec((None, tq, QA), lambda bi, kt: (bi, 0, 0)),
            pl.BlockSpec((N_BIAS_VARIANTS, H_A * tq, tk), lambda bi, kt: (0, 0, 0)),
            pl.BlockSpec((None, None, p0, D_IDX), lambda bi, kt: (layer, bi, 0, 0)),
            pl.BlockSpec((None, None, pt, 2, KV_A, HEAD_DIM), lambda bi, kt: (layer, bi, kt, 0, 0, 0)),
        ],
        out_specs=pl.BlockSpec((None, tq, QA), lambda bi, kt: (bi, 0, 0)),
        out_shape=jax.ShapeDtypeStruct((b, tq, QA), BF16),
        scratch_shapes=_dsa_scratch(nk, tq, tk) + [pltpu.VMEM((tq, tk), I32)],
        compiler_params=_cparams("parallel", "arbitrary"),
        name="dsa_sample",
    )(qcat, qcat, kiw, kiw, kv_new, gcat, bias, cache_kidx, cache_kv)


def _diff_prompt_kernel(q_ref, k_ref, v_ref, gate_ref, bias_ref, lam_ref, subln_ref, o_ref,
                        m_ref, acc_ref, *, tq, tk, hb, lam_init):
    i = pl.program_id(2)
    row_chunk = (i * tq + lax.broadcasted_iota(I32, (tq, tk), 0)) >> CHUNK_SHIFT
    lane = lax.broadcasted_iota(I32, (tq, tk), 1)
    m_ref[...] = jnp.full(m_ref.shape, NEG, F32)
    acc_ref[...] = jnp.zeros(acc_ref.shape, F32)

    def step(k, masked):
        ks = pl.ds(pl.multiple_of(k * tk, tk), tk)
        variant = jnp.minimum(i - k, N_BIAS_VARIANTS - 1)
        madd = jnp.where(((k * tk + lane) >> CHUNK_SHIFT) <= row_chunk, 0.0, NEG) if masked else None
        loaded = []
        for h in range(hb):
            bias = bias_ref[variant, h]
            if masked:
                bias = bias + madd
            vmat = v_ref[ks, _cols(h, DV_B)]
            for c in range(2):
                col = slice(DV_B * h + HEAD_DIM * c, DV_B * h + HEAD_DIM * (c + 1))
                loaded.append((h, c, q_ref[:, col], k_ref[ks, col], vmat, bias, m_ref[h, c], acc_ref[h, c]))
        new = []
        for h, c, q, kmat, vmat, bias, m_prev, acc_prev in loaded:
            s = _qk(q, kmat) * ATT_SCALE + bias
            new.append((h, c, _softmax_tile(s, vmat, m_prev, acc_prev, mxu_rowsum=False)))
        for h, c, (m_new, acc_new) in new:
            m_ref[h, c], acc_ref[h, c] = m_new, acc_new

    def full_tile(k, carry):
        step(k, False)
        return carry

    lax.fori_loop(0, i, full_tile, 0)
    step(i, True)
    for h in range(hb):
        cs = _cols(h, DV_B)
        o_ref[:, cs] = _diff_finish(acc_ref[h, 0], acc_ref[h, 1], lam_ref, subln_ref, gate_ref[:, cs],
                                    lam_init).astype(o_ref.dtype)


def diff_prompt(qcat, kv, gcat, bias, lam_vec, subln, *, tq, hb, lam_init):
    b, t, _ = qcat.shape
    tk = tq
    w = hb * DV_B
    kern = functools.partial(_diff_prompt_kernel, tq=tq, tk=tk, hb=hb, lam_init=lam_init)
    return pl.pallas_call(
        kern,
        grid=(b, H_B // hb, t // tq),
        in_specs=[
            pl.BlockSpec((None, tq, w), lambda bi, g, i: (bi, i, QA // w + g)),
            _resident((None, t, w), lambda bi, g, i: (bi, 0, g)),
            _resident((None, t, w), lambda bi, g, i: (bi, 0, QA // w + g)),
            pl.BlockSpec((None, tq, w), lambda bi, g, i: (bi, i, QA // w + g)),
            _resident((N_BIAS_VARIANTS, hb, tq, tk), lambda bi, g, i: (0, g, 0, 0)),
            pl.BlockSpec((4, HEAD_DIM), lambda bi, g, i: (0, 0)),
            pl.BlockSpec((1, DV_B), lambda bi, g, i: (0, 0)),
        ],
        out_specs=pl.BlockSpec((None, tq, w), lambda bi, g, i: (bi, i, g)),
        out_shape=jax.ShapeDtypeStruct((b, t, QA), BF16),
        scratch_shapes=[pltpu.VMEM((hb, 2, tq, LANES), F32), pltpu.VMEM((hb, 2, tq, DV_B + LANES), F32)],
        compiler_params=_cparams("parallel", "parallel", "arbitrary"),
        name="diff_prompt",
    )(qcat, kv, kv, gcat, bias, lam_vec, subln.reshape(1, DV_B))


def _diff_sample_kernel(q_ref, kn_ref, vn_ref, gate_ref, bias_ref, lam_ref, subln_ref, kc_ref, vc_ref, o_ref,
                        m_ref, acc_ref, *, tq, p0, pt, lam_init):
    kt = pl.program_id(1)
    tk = TK_SAMPLE
    n_cache = p0 // tk
    sub = pt // tk

    @pl.when(kt == 0)
    def _():
        m_ref[...] = jnp.full(m_ref.shape, NEG, F32)
        acc_ref[...] = jnp.zeros(acc_ref.shape, F32)

    def update_all(tile_of_head):
        loaded = []
        for h in range(H_B):
            kmat2, vmat, bias = tile_of_head(h)
            for c in range(2):
                q = q_ref[:, DV_B * h + HEAD_DIM * c:DV_B * h + HEAD_DIM * (c + 1)]
                loaded.append((h, c, q, kmat2[:, _cols(c)], vmat, bias, m_ref[h, c], acc_ref[h, c]))
        new = []
        for h, c, q, kmat, vmat, bias, m_prev, acc_prev in loaded:
            s = _qk(q, kmat) * ATT_SCALE + bias
            new.append((h, c, _softmax_tile(s, vmat, m_prev, acc_prev, mxu_rowsum=False)))
        for h, c, (m_new, acc_new) in new:
            m_ref[h, c], acc_ref[h, c] = m_new, acc_new

    def cache_tile(j, carry):
        rows = pl.ds(pl.multiple_of(j * tk, tk), tk)
        variant = jnp.minimum(n_cache - (kt * sub + j), N_BIAS_VARIANTS - 1)
        update_all(lambda h: (kc_ref[rows, h, :].astype(BF16), vc_ref[rows, h, :].astype(BF16),
                              bias_ref[variant, h]))
        return carry

    lax.fori_loop(0, sub, cache_tile, 0)

    @pl.when(kt == pl.num_programs(1) - 1)
    def _():
        new_madd = jnp.where(lax.broadcasted_iota(I32, (tq, tk), 1) < tq, 0.0, NEG)
        update_all(lambda h: (_pad_new_tile(kn_ref[:, _cols(h, DV_B)], tk), _pad_new_tile(vn_ref[:, _cols(h, DV_B)], tk),
                              bias_ref[0, h] + new_madd))
        for h in range(H_B):
            cs = _cols(h, DV_B)
            o_ref[:, cs] = _diff_finish(acc_ref[h, 0], acc_ref[h, 1], lam_ref, subln_ref, gate_ref[:, cs],
                                        lam_init).astype(o_ref.dtype)


def diff_sample(qcat, kv_new, gcat, bias, lam_vec, subln, cache_kv, layer, *, lam_init):
    b, tq, _ = qcat.shape
    p0 = cache_kv.shape[2]
    tk, pt = TK_SAMPLE, PT_SAMPLE
    assert p0 % pt == 0 and tq <= tk and H_B == HB_SAMPLE
    kern = functools.partial(_diff_sample_kernel, tq=tq, p0=p0, pt=pt, lam_init=lam_init)
    cache_spec = lambda sel: pl.BlockSpec((None, None, pt, None, H_B, DV_B),
                                          lambda bi, kt: (layer, bi, kt, sel, 0, 0))
    return pl.pallas_call(
        kern,
        grid=(b, p0 // pt),
        in_specs=[
            pl.BlockSpec((None, tq, QA), lambda bi, kt: (bi, 0, 1)),
            pl.BlockSpec((None, tq, QA), lambda bi, kt: (bi, 0, 0)),
            pl.BlockSpec((None, tq, QA), lambda bi, kt: (bi, 0, 1)),
            pl.BlockSpec((None, tq, QA), lambda bi, kt: (bi, 0, 1)),
            pl.BlockSpec((N_BIAS_VARIANTS, H_B, tq, tk), lambda bi, kt: (0, 0, 0, 0)),
            pl.BlockSpec((4, HEAD_DIM), lambda bi, kt: (0, 0)),
            pl.BlockSpec((1, DV_B), lambda bi, kt: (0, 0)),
            cache_spec(0), cache_spec(1),
        ],
        out_specs=pl.BlockSpec((None, tq, QA), lambda bi, kt: (bi, 0, 0)),
        out_shape=jax.ShapeDtypeStruct((b, tq, QA), BF16),
        scratch_shapes=[pltpu.VMEM((H_B, 2, tq, LANES), F32), pltpu.VMEM((H_B, 2, tq, DV_B + LANES), F32)],
        compiler_params=_cparams("parallel", "arbitrary"),
        name="diff_sample",
    )(qcat, kv_new, kv_new, gcat, bias, lam_vec, subln.reshape(1, DV_B), cache_kv, cache_kv)


def _stick_prompt_kernel(q_ref, k_ref, v_ref, gate_ref, tri_ref, o_ref, acc_ref, carry_ref, *, tq, tk, hb):
    i = pl.program_id(2)
    before = lax.broadcasted_iota(I32, (tq, tk), 1) < lax.broadcasted_iota(I32, (tq, tk), 0)
    acc_ref[...] = jnp.zeros(acc_ref.shape, F32)
    carry_ref[...] = jnp.zeros(carry_ref.shape, F32)

    def step(k, mask):
        ks = pl.ds(pl.multiple_of(k * tk, tk), tk)
        tri = tri_ref[:tk, :] if mask is None else tri_ref[...]
        loaded = [(q_ref[:, _cols(h)], k_ref[ks, _cols(h)], v_ref[ks, _cols(h)], carry_ref[h], acc_ref[h])
                  for h in range(hb)]
        new = [(acc + pv, carry) for q, kmat, vmat, carry_prev, acc in loaded
               for pv, carry in [_stick_tile(q, kmat, vmat, tri, carry_prev, mask)]]
        for h, (acc, carry) in enumerate(new):
            acc_ref[h], carry_ref[h] = acc, carry

    step(i, before)

    def full_tile(j, carry):
        step(i - 1 - j, None)
        return carry

    lax.fori_loop(0, i, full_tile, 0)
    for h in range(hb):
        cs = _cols(h)
        o_ref[:, cs] = (acc_ref[h] * _silu(gate_ref[:, cs])).astype(o_ref.dtype)


def stick_prompt(qcat, kv, gcat, tri_ext, *, tq, hb):
    b, t, _ = qcat.shape
    tk = tq
    w = hb * HEAD_DIM
    kern = functools.partial(_stick_prompt_kernel, tq=tq, tk=tk, hb=hb)
    return pl.pallas_call(
        kern,
        grid=(b, H_C // hb, t // tq),
        in_specs=[
            pl.BlockSpec((None, tq, w), lambda bi, g, i: (bi, i, g)),
            _resident((None, t, w), lambda bi, g, i: (bi, 0, g)),
            _resident((None, t, w), lambda bi, g, i: (bi, 0, QA // w + g)),
            pl.BlockSpec((None, tq, w), lambda bi, g, i: (bi, i, g)),
            _resident((2 * tk, tk + LANES), lambda bi, g, i: (0, 0)),
        ],
        out_specs=pl.BlockSpec((None, tq, w), lambda bi, g, i: (bi, i, g)),
        out_shape=jax.ShapeDtypeStruct((b, t, QA), BF16),
        scratch_shapes=[pltpu.VMEM((hb, tq, HEAD_DIM), F32), pltpu.VMEM((hb, tq, LANES), F32)],
        compiler_params=_cparams("parallel", "parallel", "arbitrary"),
        name="stick_prompt",
    )(qcat, kv, kv, gcat, tri_ext)


def _stick_sample_kernel(q_ref, kn_ref, vn_ref, gate_ref, tri_ref, kc_ref, vc_ref, o_ref, acc_ref, carry_ref,
                         *, tq, pt):
    kt = pl.program_id(2)
    tk = TK_SAMPLE
    sub = pt // tk
    hb = HB_SAMPLE

    @pl.when(kt == 0)
    def _():
        before = lax.broadcasted_iota(I32, (tq, tk), 1) < lax.broadcasted_iota(I32, (tq, tk), 0)
        tri = tri_ref[...]
        new = [_stick_tile(q_ref[:, _cols(h)], _pad_new_tile(kn_ref[:, _cols(h)], tk),
                           _pad_new_tile(vn_ref[:, _cols(h)], tk), tri, jnp.zeros((tq, LANES), F32), before)
               for h in range(hb)]
        for h, (pv, carry) in enumerate(new):
            acc_ref[h], carry_ref[h] = pv, carry

    def cache_tile(j, carry):
        rows = pl.ds(pl.multiple_of((sub - 1 - j) * tk, tk), tk)
        tri = tri_ref[:tk, :]
        loaded = [(q_ref[:, _cols(h)], kc_ref[rows, h, :].astype(BF16), vc_ref[rows, h, :].astype(BF16),
                   carry_ref[h], acc_ref[h]) for h in range(hb)]
        new = [(acc + pv, c) for q, kmat, vmat, c_prev, acc in loaded
               for pv, c in [_stick_tile(q, kmat, vmat, tri, c_prev, None)]]
        for h, (acc, c) in enumerate(new):
            acc_ref[h], carry_ref[h] = acc, c
        return carry

    lax.fori_loop(0, sub, cache_tile, 0)

    @pl.when(kt == pl.num_programs(2) - 1)
    def _():
        for h in range(hb):
            cs = _cols(h)
            o_ref[:, cs] = (acc_ref[h] * _silu(gate_ref[:, cs])).astype(o_ref.dtype)


def stick_sample(qcat, kv_new, gcat, tri_ext, cache_kv, layer):
    b, tq, _ = qcat.shape
    p0 = cache_kv.shape[2]
    pt = PT_SAMPLE
    w = HB_SAMPLE * HEAD_DIM
    assert p0 % pt == 0 and tq <= TK_SAMPLE
    n_kt = p0 // pt
    kern = functools.partial(_stick_sample_kernel, tq=tq, pt=pt)
    cache_spec = lambda sel: pl.BlockSpec((None, None, pt, None, HB_SAMPLE, HEAD_DIM),
                                          lambda bi, g, kt: (layer, bi, n_kt - 1 - kt, sel, g, 0))
    return pl.pallas_call(
        kern,
        grid=(b, H_C // HB_SAMPLE, n_kt),
        in_specs=[
            pl.BlockSpec((None, tq, w), lambda bi, g, kt: (bi, 0, g)),
            pl.BlockSpec((None, tq, w), lambda bi, g, kt: (bi, 0, g)),
            pl.BlockSpec((None, tq, w), lambda bi, g, kt: (bi, 0, QA // w + g)),
            pl.BlockSpec((None, tq, w), lambda bi, g, kt: (bi, 0, g)),
            pl.BlockSpec((2 * TK_SAMPLE, TK_SAMPLE + LANES), lambda bi, g, kt: (0, 0)),
            cache_spec(0), cache_spec(1),
        ],
        out_specs=pl.BlockSpec((None, tq, w), lambda bi, g, kt: (bi, 0, g)),
        out_shape=jax.ShapeDtypeStruct((b, tq, QA), BF16),
        scratch_shapes=[pltpu.VMEM((HB_SAMPLE, tq, HEAD_DIM), F32), pltpu.VMEM((HB_SAMPLE, tq, LANES), F32)],
        compiler_params=_cparams("parallel", "parallel", "arbitrary"),
        name="stick_sample",
    )(qcat, kv_new, kv_new, gcat, tri_ext, cache_kv, cache_kv)


def _band_prompt_kernel(q_ref, k_ref, v_ref, gate_ref, bias_ref, o_ref, *, tq, win, hb):
    i = pl.program_id(2)
    span = win + tq
    rows = pl.ds(pl.multiple_of(i * tq, tq), span)
    row_chunk = (i * tq + lax.broadcasted_iota(I32, (tq, span), 0)) >> CHUNK_SHIFT
    key_chunk = (i * tq - win + lax.broadcasted_iota(I32, (tq, span), 1)) >> CHUNK_SHIFT
    visible = (key_chunk <= row_chunk) & (key_chunk >= jnp.maximum(row_chunk - BAND_CHUNKS, 0))
    madd = jnp.where(visible, 0.0, NEG)
    ones = jnp.ones((span, LANES), BF16)
    for h in range(hb):
        cs = _cols(h)
        s = _qk(q_ref[:, cs], k_ref[rows, cs]) * ATT_SCALE + bias_ref[h] + madd
        p = jnp.exp(s - jnp.max(s, axis=1, keepdims=True))
        pv = jnp.dot(p.astype(BF16), jnp.concatenate([v_ref[rows, cs], ones], axis=1), preferred_element_type=F32)
        o_ref[:, cs] = (_softmax_finish(pv) * _silu(gate_ref[:, cs])).astype(o_ref.dtype)


def band_prompt(qcat, kv, gcat, bias, *, hb):
    b, t, _ = qcat.shape
    tq = LANES
    win = BAND_CHUNKS * CHUNK
    w = hb * HEAD_DIM
    assert kv.shape[1] == win + t and bias.shape == (H_D, tq, win + tq)
    kern = functools.partial(_band_prompt_kernel, tq=tq, win=win, hb=hb)
    return pl.pallas_call(
        kern,
        grid=(b, H_D // hb, t // tq),
        in_specs=[
            pl.BlockSpec((None, tq, w), lambda bi, g, i: (bi, i, QA // w + g)),
            _resident((None, win + t, w), lambda bi, g, i: (bi, 0, g)),
            _resident((None, win + t, w), lambda bi, g, i: (bi, 0, QA // w + g)),
            pl.BlockSpec((None, tq, w), lambda bi, g, i: (bi, i, QA // w + g)),
            _resident((hb, tq, win + tq), lambda bi, g, i: (g, 0, 0)),
        ],
        out_specs=pl.BlockSpec((None, tq, w), lambda bi, g, i: (bi, i, g)),
        out_shape=jax.ShapeDtypeStruct((b, t, QA), BF16),
        compiler_params=_cparams("parallel", "parallel", "arbitrary"),
        name="band_prompt",
    )(qcat, kv, kv, gcat, bias)


def _band_sample_kernel(q_ref, kn_ref, vn_ref, gate_ref, bias_ref, kc_ref, vc_ref, o_ref, *, tq, win):
    tk = TK_SAMPLE
    n_cache = win // tk
    new_madd = jnp.where(lax.broadcasted_iota(I32, (tq, tk), 1) < tq, 0.0, NEG)
    for h in range(HB_SAMPLE):
        cs = _cols(h)
        q = q_ref[:, cs]
        m, acc = jnp.full((tq, LANES), NEG, F32), jnp.zeros((tq, HEAD_DIM + LANES), F32)
        for k in range(n_cache):
            rows = slice(k * tk, (k + 1) * tk)
            s = _qk(q, kc_ref[rows, h, :].astype(BF16)) * ATT_SCALE + bias_ref[
                min(n_cache - k, N_BIAS_VARIANTS - 1), h]
            m, acc = _softmax_tile(s, vc_ref[rows, h, :].astype(BF16), m, acc)
        s = _qk(q, _pad_new_tile(kn_ref[:, cs], tk)) * ATT_SCALE + bias_ref[0, h] + new_madd
        m, acc = _softmax_tile(s, _pad_new_tile(vn_ref[:, cs], tk), m, acc)
        o_ref[:, cs] = (_softmax_finish(acc) * _silu(gate_ref[:, cs])).astype(o_ref.dtype)


def band_sample(qcat, kv_new, gcat, bias, cache_kv, layer):
    b, tq, _ = qcat.shape
    win = cache_kv.shape[2]
    w = HB_SAMPLE * HEAD_DIM
    assert win == BAND_CHUNKS * CHUNK and win % TK_SAMPLE == 0 and tq == CHUNK
    kern = functools.partial(_band_sample_kernel, tq=tq, win=win)
    cache_spec = lambda sel: pl.BlockSpec((None, None, win, None, HB_SAMPLE, HEAD_DIM),
                                          lambda bi, g: (layer, bi, 0, sel, g, 0))
    return pl.pallas_call(
        kern,
        grid=(b, H_D // HB_SAMPLE),
        in_specs=[
            pl.BlockSpec((None, tq, w), lambda bi, g: (bi, 0, QA // w + g)),
            pl.BlockSpec((None, tq, w), lambda bi, g: (bi, 0, g)),
            pl.BlockSpec((None, tq, w), lambda bi, g: (bi, 0, QA // w + g)),
            pl.BlockSpec((None, tq, w), lambda bi, g: (bi, 0, QA // w + g)),
            pl.BlockSpec((N_BIAS_VARIANTS, HB_SAMPLE, tq, TK_SAMPLE), lambda bi, g: (0, g, 0, 0)),
            cache_spec(0), cache_spec(1),
        ],
        out_specs=pl.BlockSpec((None, tq, w), lambda bi, g: (bi, 0, g)),
        out_shape=jax.ShapeDtypeStruct((b, tq, QA), BF16),
        compiler_params=_cparams("parallel", "parallel"),
        name="band_sample",
    )(qcat, kv_new, kv_new, gcat, bias, cache_kv, cache_kv)


def _t5_bucket_np(rel):
    half = T5_BUCKETS // 2
    max_exact = half // 2
    n = np.abs(rel)
    nf = np.maximum(n, 1).astype(np.float64)
    large = max_exact + (np.log(nf / max_exact) / math.log(T5_MAX_DIST / max_exact)
                         * (half - max_exact)).astype(np.int32)
    large = np.minimum(large, half - 1)
    return np.where(rel < 0, half, 0) + np.where(n < max_exact, n, large)


def _band_index_np(rel):
    return np.clip(rel, -REL_CLIP, REL_CLIP) + REL_CLIP


def toeplitz_bias(table, index_of_rel, r0, n_rows, n_cols):
    rel = r0 + np.arange(n_rows)[:, None] - np.arange(n_cols)[None, :]
    idx = jnp.asarray(index_of_rel(rel).reshape(-1).astype(np.int32))
    onehot = (idx[None, :] == jnp.arange(table.shape[0], dtype=I32)[:, None]).astype(F32)
    out = jnp.einsum("bh,bn->hn", table, onehot, precision=lax.Precision.HIGHEST)
    return out.reshape(table.shape[1], n_rows, n_cols)


def bias_tiles(table, index_of_rel, tq, tk):
    far = (N_BIAS_VARIANTS - 1) * tk
    assert (index_of_rel(np.arange(far - tk + 1, far + tq)) == index_of_rel(np.array(far + tq))).all()
    return jnp.stack([toeplitz_bias(table, index_of_rel, v * tk, tq, tk) for v in range(N_BIAS_VARIANTS)], axis=0)


def _tri_ext(tk):
    tri = np.tril(np.ones((tk, tk), np.float32), -1)
    ext = np.concatenate([tri, np.ones((tk, LANES), np.float32)], axis=1)
    return jnp.asarray(np.concatenate([ext, ext], axis=0), BF16)


def _even_weights(w_in):
    c = np.cumsum((0, QA, KVA, KVA, QI, D_IDX, H_IDX, QA, QA, QA, QA, QA)).tolist()
    aq, ak, av, aqi, aki, aw, ag, bq, bk, bv, bg = (w_in[:, c[r]:c[r + 1]] for r in range(11))
    pad = jnp.zeros((w_in.shape[0], LANES - D_IDX - H_IDX), w_in.dtype)
    cat = lambda *xs: jnp.concatenate(xs, axis=1).astype(BF16)
    return cat(aq, bq, aqi), cat(ak, av), cat(bk, bv), cat(aki, aw, pad), cat(ag, bg)


def _odd_weights(w_in):
    cq, ck, cv, cg, dq, dk, dv, dg = (w_in[:, QA * r:QA * (r + 1)] for r in range(8))
    cat = lambda *xs: jnp.concatenate(xs, axis=1).astype(BF16)
    return cat(cq, dq), cat(ck, cv), cat(dk, dv), cat(cg, dg)


def _even_layer(h, caches, layer, weights, t5_tab, lam_vec, subln, lam_init):
    w_q, w_akv, w_bkv, w_kiw, w_g = weights
    b, t, d = h.shape
    h2 = h.reshape(b * t, d)
    (qcat,) = matmul(h2, w_q, (BF16,))
    akv, akv_b = matmul(h2, w_akv, (F32, BF16))
    bkv, bkv_b = matmul(h2, w_bkv, (F32, BF16))
    (kiw,) = matmul(h2, w_kiw, (F32,))
    (gcat,) = matmul(h2, w_g, (F32,))
    qcat, akv_b, bkv_b, kiw, gcat = (x.reshape(b, t, -1) for x in (qcat, akv_b, bkv_b, kiw, gcat))
    aki = kiw[..., :D_IDX]
    t5_a, t5_b = t5_tab[:, :H_A], t5_tab[:, H_A:]
    if caches is None:
        tq_b = 256
        bias_a = bias_tiles(t5_a, _t5_bucket_np, LANES, LANES).reshape(N_BIAS_VARIANTS, H_A * LANES, LANES)
        o_a = dsa_prompt(qcat, kiw, aki.astype(BF16), akv_b, gcat, bias_a, n_sel=min(TOPK_MAX, t // 4))
        o_b = diff_prompt(qcat, bkv_b, gcat, bias_tiles(t5_b, _t5_bucket_np, tq_b, tq_b), lam_vec, subln,
                          tq=tq_b, hb=4, lam_init=lam_init)
    else:
        cache_a, cache_ki, cache_b = caches
        p0 = cache_a.shape[2]
        bias_a = bias_tiles(t5_a, _t5_bucket_np, t, TK_SAMPLE).reshape(N_BIAS_VARIANTS, H_A * t, TK_SAMPLE)
        o_a = dsa_sample(qcat, kiw, akv_b, gcat, bias_a, cache_ki, cache_a, layer,
                         n_sel=min(TOPK_MAX, (p0 + t) // 4))
        o_b = diff_sample(qcat, bkv_b, gcat, bias_tiles(t5_b, _t5_bucket_np, t, TK_SAMPLE), lam_vec, subln,
                          cache_b, layer, lam_init=lam_init)
    mixed = jnp.concatenate([o_a, o_b], axis=-1).reshape(b * t, 2 * QA)
    return mixed, akv.reshape(b, t, -1), aki, bkv.reshape(b, t, -1)


def _odd_layer(h, caches, layer, weights, rel_tab):
    w_q, w_ckv, w_dkv, w_g = weights
    b, t, d = h.shape
    h2 = h.reshape(b * t, d)
    (qcat,) = matmul(h2, w_q, (BF16,))
    ckv, ckv_b = matmul(h2, w_ckv, (F32, BF16))
    dkv, dkv_b = matmul(h2, w_dkv, (F32, BF16))
    (gcat,) = matmul(h2, w_g, (F32,))
    qcat, ckv_b, dkv_b, gcat = (x.reshape(b, t, -1) for x in (qcat, ckv_b, dkv_b, gcat))
    if caches is None:
        tq_c = 256
        o_c = stick_prompt(qcat, ckv_b, gcat, _tri_ext(tq_c), tq=tq_c, hb=4)
        win = BAND_CHUNKS * CHUNK
        far_cols = win - REL_CLIP
        bias_d = jnp.concatenate(
            [jnp.broadcast_to(rel_tab[2 * REL_CLIP][:, None, None], (H_D, LANES, far_cols)),
             toeplitz_bias(rel_tab, _band_index_np, REL_CLIP, LANES, win + LANES - far_cols)], axis=-1)
        o_d = band_prompt(qcat, jnp.pad(dkv_b, ((0, 0), (win, 0), (0, 0))), gcat, bias_d, hb=4)
    else:
        cache_c, cache_d = caches
        o_c = stick_sample(qcat, ckv_b, gcat, _tri_ext(TK_SAMPLE), cache_c, layer)
        o_d = band_sample(qcat, dkv_b, gcat, bias_tiles(rel_tab, _band_index_np, t, TK_SAMPLE), cache_d, layer)
    mixed = jnp.concatenate([o_c, o_d], axis=-1).reshape(b * t, 2 * QA)
    return mixed, ckv.reshape(b, t, -1), dkv.reshape(b, t, -1)


def _finish_layer(x, mixed, w_out, p_i, g_post, w_proj, g_pl, w_gate):
    (mix,) = matmul(mixed, w_out, (F32,))
    x1, x1_b = residual_norm(x, mix, g_post)
    e = embed_norm(p_i.astype(BF16), w_proj, g_pl)
    return gate_matmul(x1_b, w_gate, x1, e)


def kernel(x_prompt, x_sample, p_prompt, p_sample, cache_a_kv, cache_a_kidx, cache_b_kv, cache_c_kv,
           cache_d_kv, norm_pre, norm_post, w_in_even, w_out_even, t5_bias, diff_lambda, diff_subln,
           w_in_odd, w_out_odd, d_rel_bias, w_pl_proj, pl_norm, w_pl_gate):
    bp, tp, d = x_prompt.shape
    bs, ts, _ = x_sample.shape
    depth = norm_pre.shape[0]
    assert d == D_MODEL and ts == CHUNK and tp % 256 == 0
    xp = x_prompt.reshape(bp * tp, d)
    xs = x_sample.reshape(bs * ts, d)
    outs = {name: [] for name in ("a_kv_p", "a_kv_s", "a_ki_p", "a_ki_s", "b_kv_p", "b_kv_s",
                                  "c_kv_p", "c_kv_s", "d_kv_p", "d_kv_s")}
    for i in range(depth):
        j = i // 2
        hp = rmsnorm_cast(xp, norm_pre[i]).reshape(bp, tp, d)
        hs = rmsnorm_cast(xs, norm_pre[i]).reshape(bs, ts, d)
        if i % 2 == 0:
            lam_init = 0.8 - 0.6 * math.exp(-0.3 * i)
            weights = _even_weights(w_in_even[j])
            w_out = w_out_even[j].astype(BF16)
            prm = (weights, t5_bias, diff_lambda[j], diff_subln[j], lam_init)
            mp, akv, aki, bkv = _even_layer(hp, None, j, *prm)
            outs["a_kv_p"].append(akv.reshape(bp, tp, 2, KV_A, HEAD_DIM))
            outs["a_ki_p"].append(aki)
            outs["b_kv_p"].append(bkv.reshape(bp, tp, 2, H_B, DV_B))
            ms, akv, aki, bkv = _even_layer(hs, (cache_a_kv, cache_a_kidx, cache_b_kv), j, *prm)
            outs["a_kv_s"].append(akv.reshape(bs, ts, 2, KV_A, HEAD_DIM))
            outs["a_ki_s"].append(aki)
            outs["b_kv_s"].append(bkv.reshape(bs, ts, 2, H_B, DV_B))
        else:
            weights = _odd_weights(w_in_odd[j])
            w_out = w_out_odd[j].astype(BF16)
            mp, ckv, dkv = _odd_layer(hp, None, j, weights, d_rel_bias[j])
            win_p = min(BAND_CHUNKS * CHUNK, tp)
            outs["c_kv_p"].append(ckv.reshape(bp, tp, 2, H_C, HEAD_DIM))
            outs["d_kv_p"].append(dkv[:, tp - win_p:].reshape(bp, win_p, 2, H_D, HEAD_DIM))
            ms, ckv, dkv = _odd_layer(hs, (cache_c_kv, cache_d_kv), j, weights, d_rel_bias[j])
            outs["c_kv_s"].append(ckv.reshape(bs, ts, 2, H_C, HEAD_DIM))
            full_d = jnp.concatenate([cache_d_kv[j], dkv.reshape(bs, ts, 2, H_D, HEAD_DIM)], axis=1)
            outs["d_kv_s"].append(full_d[:, ts:])
        fin = (norm_post[i], w_pl_proj[i].astype(BF16), pl_norm[i], w_pl_gate[i].astype(BF16))
        xp = _finish_layer(xp, mp, w_out, p_prompt[i].reshape(bp * tp, -1), *fin)
        xs = _finish_layer(xs, ms, w_out, p_sample[i].reshape(bs * ts, -1), *fin)
    st = lambda name: jnp.stack(outs[name])
    return (xp.reshape(bp, tp, d), xs.reshape(bs, ts, d), st("a_kv_p"), st("a_kv_s"), st("a_ki_p"), st("a_ki_s"),
            st("b_kv_p"), st("b_kv_s"), st("c_kv_p"), st("c_kv_s"), st("d_kv_p"), st("d_kv_s"))
```

```python
import functools
import math

import numpy as np
import jax
import jax.numpy as jnp
from jax import lax
from jax.experimental import pallas as pl
from jax.experimental.pallas import tpu as pltpu

F32 = jnp.float32
BF16 = jnp.bfloat16
I32 = jnp.int32

D_MODEL = 4096
CHUNK = 64
CHUNK_SHIFT = 6
HEAD_DIM = 128
H_A = 16
KV_A = 4
GROUP_A = H_A // KV_A
H_IDX = 16
D_IDX = 64
TOPK_MAX = 256
H_B = 8
H_C = 16
H_D = 16
BAND_CHUNKS = 8
REL_CLIP = 128
T5_BUCKETS = 32
T5_MAX_DIST = 128
RMS_EPS = 1e-6
QA = H_A * HEAD_DIM
KVA = KV_A * HEAD_DIM
QI = H_IDX * D_IDX
DV_B = 2 * HEAD_DIM
ATT_SCALE = HEAD_DIM ** -0.5
IDX_SCALE = (D_IDX ** -0.5) * (H_IDX ** -0.5)

LANES = 128
NEG = -1e30
INT_MIN = -(2 ** 31)
VMEM_LIMIT = 56 * 1024 * 1024
N_BIAS_VARIANTS = 3
TK_SAMPLE = LANES
PT_SAMPLE = 512
HB_SAMPLE = 8
DSA_TILES_PER_STEP = 4


def _cparams(*sem):
    return pltpu.CompilerParams(dimension_semantics=sem, vmem_limit_bytes=VMEM_LIMIT)


def _sigmoid(x):
    return 1.0 / (1.0 + jnp.exp(-x))


def _silu(x):
    return x * _sigmoid(x)


def _lane_tile(x, n):
    return x if n == 1 else jnp.concatenate([x] * n, axis=1)


def _cols(h, width=HEAD_DIM):
    return slice(width * h, width * (h + 1))


def _rmsnorm_cast_kernel(x_ref, g_ref, o_ref):
    x = x_ref[...]
    ms = jnp.mean(x * x, axis=-1, keepdims=True)
    o_ref[...] = (x * lax.rsqrt(ms + RMS_EPS) * g_ref[...]).astype(o_ref.dtype)


def rmsnorm_cast(x, g, bm=256):
    m, d = x.shape
    bm = min(bm, m)
    return pl.pallas_call(
        _rmsnorm_cast_kernel,
        grid=(m // bm,),
        in_specs=[pl.BlockSpec((bm, d), lambda i: (i, 0)), pl.BlockSpec((1, d), lambda i: (0, 0))],
        out_specs=pl.BlockSpec((bm, d), lambda i: (i, 0)),
        out_shape=jax.ShapeDtypeStruct((m, d), BF16),
        compiler_params=_cparams("parallel"),
        name="rmsnorm_cast",
    )(x, g.reshape(1, d))


def _residual_norm_kernel(x_ref, mix_ref, g_ref, o_ref, ob_ref):
    mix = mix_ref[...]
    ms = jnp.mean(mix * mix, axis=-1, keepdims=True)
    x1 = x_ref[...] + mix * lax.rsqrt(ms + RMS_EPS) * g_ref[...]
    o_ref[...] = x1
    ob_ref[...] = x1.astype(BF16)


def residual_norm(x, mix, g, bm=256):
    m, d = x.shape
    bm = min(bm, m)
    row = pl.BlockSpec((bm, d), lambda i: (i, 0))
    return pl.pallas_call(
        _residual_norm_kernel,
        grid=(m // bm,),
        in_specs=[row, row, pl.BlockSpec((1, d), lambda i: (0, 0))],
        out_specs=[row, row],
        out_shape=[jax.ShapeDtypeStruct((m, d), F32), jax.ShapeDtypeStruct((m, d), BF16)],
        compiler_params=_cparams("parallel"),
        name="residual_norm",
    )(x, mix, g.reshape(1, d))


def _embed_norm_kernel(p_ref, w_ref, g_ref, o_ref):
    y = jnp.dot(p_ref[...], w_ref[...], preferred_element_type=F32)
    ms = jnp.mean(y * y, axis=-1, keepdims=True)
    o_ref[...] = y * lax.rsqrt(ms + RMS_EPS) * g_ref[...]


def embed_norm(p, w, g, bm=256):
    m, k = p.shape
    d = w.shape[1]
    bm = min(bm, m)
    return pl.pallas_call(
        _embed_norm_kernel,
        grid=(m // bm,),
        in_specs=[pl.BlockSpec((bm, k), lambda i: (i, 0)), pl.BlockSpec((k, d), lambda i: (0, 0)),
                  pl.BlockSpec((1, d), lambda i: (0, 0))],
        out_specs=pl.BlockSpec((bm, d), lambda i: (i, 0)),
        out_shape=jax.ShapeDtypeStruct((m, d), F32),
        compiler_params=_cparams("parallel"),
        name="embed_norm",
    )(p, w, g.reshape(1, d))


def _mm_kernel(a_ref, w_ref, *o_refs):
    acc = jnp.dot(a_ref[...], w_ref[...], preferred_element_type=F32)
    for o_ref in o_refs:
        o_ref[...] = acc.astype(o_ref.dtype)


def matmul(a, w, out_dtypes, bm=1024, bn=1024):
    m, k = a.shape
    n = w.shape[1]
    bm, bn = min(bm, m), min(bn, n)
    o_spec = pl.BlockSpec((bm, bn), lambda i, j: (i, j))
    outs = pl.pallas_call(
        _mm_kernel,
        grid=(m // bm, n // bn),
        in_specs=[pl.BlockSpec((bm, k), lambda i, j: (i, 0)), pl.BlockSpec((k, bn), lambda i, j: (0, j))],
        out_specs=[o_spec] * len(out_dtypes),
        out_shape=[jax.ShapeDtypeStruct((m, n), dt) for dt in out_dtypes],
        compiler_params=_cparams("parallel", "parallel"),
        name="matmul",
    )(a, w)
    return outs


def _gate_mm_kernel(a_ref, w_ref, x_ref, e_ref, o_ref):
    logits = jnp.dot(a_ref[...], w_ref[...], preferred_element_type=F32)
    o_ref[...] = x_ref[...] + e_ref[...] * _sigmoid(logits)


def gate_matmul(a, w, x, e, bm=512, bn=1024):
    m, k = a.shape
    n = w.shape[1]
    bm, bn = min(bm, m), min(bn, n)
    tile = pl.BlockSpec((bm, bn), lambda i, j: (i, j))
    return pl.pallas_call(
        _gate_mm_kernel,
        grid=(m // bm, n // bn),
        in_specs=[pl.BlockSpec((bm, k), lambda i, j: (i, 0)), pl.BlockSpec((k, bn), lambda i, j: (0, j)),
                  tile, tile],
        out_specs=tile,
        out_shape=jax.ShapeDtypeStruct((m, n), F32),
        compiler_params=_cparams("parallel", "parallel"),
        name="gate_matmul",
    )(a, w, x, e)


def _resident(block_shape, index_map):
    return pl.BlockSpec(block_shape, index_map, pipeline_mode=pl.Buffered(1))


def _qk(q, kmat):
    return lax.dot_general(q, kmat, (((1,), (1,)), ((), ())), preferred_element_type=F32)


def _softmax_tile(s, vmat, m_prev, acc_prev, mxu_rowsum=True):
    tk = s.shape[1]
    m_new = jnp.maximum(m_prev, jnp.max(s, axis=1, keepdims=True))
    alpha = jnp.exp(m_prev - m_new)
    p = jnp.exp(s - _lane_tile(m_new, tk // LANES))
    if mxu_rowsum:
        v_ext = jnp.concatenate([vmat, jnp.ones((tk, LANES), BF16)], axis=1)
        pv = jnp.dot(p.astype(BF16), v_ext, preferred_element_type=F32)
    else:
        row = jnp.broadcast_to(jnp.sum(p, axis=1, keepdims=True), m_prev.shape)
        pv = jnp.concatenate([jnp.dot(p.astype(BF16), vmat, preferred_element_type=F32), row], axis=1)
    return m_new, _lane_tile(alpha, pv.shape[1] // LANES) * acc_prev + pv


def _softmax_tiles(chains, mxu_rowsum=True):
    tk = chains[0][0].shape[1]
    m_new = [jnp.maximum(m_prev, jnp.max(s, axis=1, keepdims=True)) for s, _, m_prev, _ in chains]
    alpha = [jnp.exp(m_prev - m) for (_, _, m_prev, _), m in zip(chains, m_new)]
    p = [jnp.exp(s - _lane_tile(m, tk // LANES)) for (s, _, _, _), m in zip(chains, m_new)]
    if mxu_rowsum:
        ones = jnp.ones((tk, LANES), BF16)
        pv = [jnp.dot(pi.astype(BF16), jnp.concatenate([vmat, ones], axis=1), preferred_element_type=F32)
              for pi, (_, vmat, _, _) in zip(p, chains)]
    else:
        rows = [jnp.broadcast_to(jnp.sum(pi, axis=1, keepdims=True), m.shape) for pi, m in zip(p, m_new)]
        pv = [jnp.concatenate([jnp.dot(pi.astype(BF16), vmat, preferred_element_type=F32), row], axis=1)
              for pi, (_, vmat, _, _), row in zip(p, chains, rows)]
    acc = [_lane_tile(a, x.shape[1] // LANES) * acc_prev + x for a, x, (_, _, _, acc_prev) in zip(alpha, pv, chains)]
    return list(zip(m_new, acc))


def _softmax_finish(acc):
    dv = acc.shape[1] - LANES
    return acc[:, :dv] / _lane_tile(acc[:, dv:], dv // LANES)


def _stick_tile(q, kmat, vmat, tri, carry, before):
    tk = kmat.shape[0]
    z = _qk(q, kmat) * ATT_SCALE
    softplus = jnp.maximum(z, 0.0) + jnp.log(1.0 + jnp.exp(-jnp.abs(z)))
    log_fail = -softplus
    hi = log_fail.astype(BF16)
    if before is None:
        sums = jnp.dot(hi, tri, preferred_element_type=F32)
    else:
        log_fail = jnp.where(before, log_fail, 0.0)
        hi = log_fail.astype(BF16)
        lo = (log_fail - hi.astype(F32)).astype(BF16)
        sums = jnp.dot(jnp.concatenate([hi, lo], axis=1), tri, preferred_element_type=F32)
    w = jnp.exp((z - softplus) + sums[:, :tk] + _lane_tile(carry, tk // LANES))
    if before is not None:
        w = jnp.where(before, w, 0.0)
    pv = jnp.dot(w.astype(BF16), vmat, preferred_element_type=F32)
    return pv, carry + sums[:, tk:]


def _stick_tiles(chains, tri, before):
    tk = chains[0][1].shape[0]
    z = [_qk(q, kmat) * ATT_SCALE for q, kmat, _, _ in chains]
    softplus = [jnp.maximum(x, 0.0) + jnp.log(1.0 + jnp.exp(-jnp.abs(x))) for x in z]
    if before is None:
        sums = [jnp.dot((-sp).astype(BF16), tri, preferred_element_type=F32) for sp in softplus]
    else:
        log_fail = [jnp.where(before, -sp, 0.0) for sp in softplus]
        hi = [lf.astype(BF16) for lf in log_fail]
        lo = [(lf - h.astype(F32)).astype(BF16) for lf, h in zip(log_fail, hi)]
        sums = [jnp.dot(jnp.concatenate([h, l], axis=1), tri, preferred_element_type=F32) for h, l in zip(hi, lo)]
    w = [jnp.exp((x - sp) + s[:, :tk] + _lane_tile(carry, tk // LANES))
         for x, sp, s, (_, _, _, carry) in zip(z, softplus, sums, chains)]
    if before is not None:
        w = [jnp.where(before, x, 0.0) for x in w]
    pv = [jnp.dot(x.astype(BF16), vmat, preferred_element_type=F32) for x, (_, _, vmat, _) in zip(w, chains)]
    return [(p, carry + s[:, tk:]) for p, s, (_, _, _, carry) in zip(pv, sums, chains)]


def _diff_finish(acc0, acc1, lam_ref, subln_ref, gate, lam_init):
    lam_q1, lam_k1, lam_q2, lam_k2 = (lam_ref[r:r + 1, :] for r in range(4))
    lam = (jnp.exp(jnp.sum(lam_q1 * lam_k1, axis=1, keepdims=True))
           - jnp.exp(jnp.sum(lam_q2 * lam_k2, axis=1, keepdims=True)) + lam_init)
    o = _softmax_finish(acc0) - lam * _softmax_finish(acc1)
    ms = jnp.mean(o * o, axis=-1, keepdims=True)
    y = o * lax.rsqrt(ms + RMS_EPS) * subln_ref[...] * (1.0 - lam_init)
    return y * _silu(gate)


def _pad_new_tile(x, tk):
    return jnp.concatenate([x, jnp.zeros((tk - x.shape[0], x.shape[1]), x.dtype)], axis=0)


def _dsa_scores_to_keys(qih_ref, wb_ref, ki_tile, visible):
    tq = wb_ref.shape[1]
    s_all = _qk(qih_ref[...], ki_tile)
    acc = jnp.zeros((tq, ki_tile.shape[0]), F32)
    for h in range(H_IDX):
        acc = acc + jnp.maximum(s_all[h * tq:(h + 1) * tq], 0.0) * wb_ref[h]
    bits = pltpu.bitcast(acc, I32)
    key = bits ^ ((bits >> 31) & 0x7FFFFFFF)
    return jnp.where(visible, key, INT_MIN)


def _dsa_prepare(q_ref, qi_ref, w_ref, qs_ref, qih_ref, wb_ref, tq):
    wv = w_ref[:, D_IDX:D_IDX + H_IDX] * IDX_SCALE
    for h in range(H_IDX):
        qih_ref[h * tq:(h + 1) * tq, :] = qi_ref[:, _cols(h, D_IDX)]
        wb_ref[h] = jnp.broadcast_to(wv[:, h:h + 1], (tq, LANES))
    for n in range(KV_A):
        for g in range(GROUP_A):
            qs_ref[n, g * tq:(g + 1) * tq, :] = q_ref[:, _cols(GROUP_A * n + g)]


def _dsa_threshold(key_ref, nvis, n_sel, idx_bits, tq, tk):
    lane = lax.broadcasted_iota(I32, (tq, tk), 1)

    def count_where(pred):
        def body(k, cnt):
            return cnt + jnp.where(pred(k, key_ref[k]), 1.0, 0.0)
        cnt = lax.fori_loop(0, nvis, body, jnp.zeros((tq, tk), F32))
        return jnp.sum(cnt, axis=1, keepdims=True)

    def count_ge(cand):
        return count_where(lambda k, key: key >= cand)

    def thr_bit(b, t):
        cand = t + lax.shift_left(jnp.int32(1), 31 - b)
        return jnp.where(count_ge(cand) >= n_sel, cand, t)

    thr = lax.fori_loop(0, 32, thr_bit, jnp.full((tq, 1), INT_MIN, I32))
    thr = jnp.maximum(thr, INT_MIN + 1)

    @pl.when(jnp.max(count_ge(thr)) > n_sel)
    def _():
        need = n_sel - count_ge(thr + 1)

        def count_eq_below(c):
            return count_where(lambda k, key: (key == thr) & ((k * tk + lane) < c))

        def idx_bit(b, c):
            cand = c + lax.shift_left(jnp.int32(1), idx_bits - 1 - b)
            return jnp.where(count_eq_below(cand) <= need, cand, c)

        cut = lax.fori_loop(0, idx_bits, idx_bit, jnp.zeros((tq, 1), I32))

        def demote(k, carry):
            key = key_ref[k]
            key_ref[k] = jnp.where((key == thr) & ((k * tk + lane) >= cut), key - 1, key)
            return carry

        lax.fori_loop(0, nvis, demote, 0)

    return thr


def _dsa_attend_tiles(qs_ref, key_tiles, thr, bias_tiles, kv_of_head, m_ref, acc_ref, tq):
    rows4 = GROUP_A * tq
    madd = jnp.concatenate([jnp.where(key_tile >= thr, 0.0, NEG) for key_tile in key_tiles], axis=1)
    madd4 = jnp.concatenate([madd] * GROUP_A, axis=0)
    loaded = [(qs_ref[n], *kv_of_head(n), m_ref[n], acc_ref[n]) for n in range(KV_A)]
    scores = []
    for n, (q, kmat, _, _, _) in enumerate(loaded):
        bias = jnp.concatenate([b[rows4 * n:rows4 * (n + 1), :] for b in bias_tiles], axis=1)
        scores.append(_qk(q, kmat) * ATT_SCALE + bias + madd4)
    new = _softmax_tiles([(s, vmat, m_prev, acc_prev) for s, (_, _, vmat, m_prev, acc_prev) in zip(scores, loaded)])
    for n, (m_new, acc_new) in enumerate(new):
        m_ref[n], acc_ref[n] = m_new, acc_new


def _dsa_finish(acc_ref, gate_ref, o_ref, tq):
    for n in range(KV_A):
        o = _softmax_finish(acc_ref[n])
        for g in range(GROUP_A):
            cs = _cols(GROUP_A * n + g)
            o_ref[:, cs] = (o[g * tq:(g + 1) * tq] * _silu(gate_ref[:, cs])).astype(o_ref.dtype)


def _dsa_scratch(nk, tq, tk):
    rows4 = GROUP_A * tq
    return [
        pltpu.VMEM((KV_A, rows4, HEAD_DIM), BF16),
        pltpu.VMEM((H_IDX * tq, D_IDX), BF16),
        pltpu.VMEM((H_IDX, tq, LANES), F32),
        pltpu.VMEM((nk, tq, tk), I32),
        pltpu.VMEM((KV_A, rows4, LANES), F32),
        pltpu.VMEM((KV_A, rows4, HEAD_DIM + LANES), F32),
    ]


def _dsa_prompt_kernel(q_ref, qi_ref, w_ref, ki_ref, kv_ref, gate_ref, bias_ref, o_ref,
                       qs_ref, qih_ref, wb_ref, key_ref, m_ref, acc_ref, *, tq, tk, group, n_sel, idx_bits):
    i = pl.program_id(1)
    nvis = i + 1
    row_chunk = (i * tq + lax.broadcasted_iota(I32, (tq, tk), 0)) >> CHUNK_SHIFT
    lane = lax.broadcasted_iota(I32, (tq, tk), 1)
    _dsa_prepare(q_ref, qi_ref, w_ref, qs_ref, qih_ref, wb_ref, tq)

    def score_tile(k, carry):
        ks = pl.ds(pl.multiple_of(k * tk, tk), tk)
        visible = ((k * tk + lane) >> CHUNK_SHIFT) <= row_chunk
        key_ref[k] = _dsa_scores_to_keys(qih_ref, wb_ref, ki_ref[ks, :], visible)
        return carry

    lax.fori_loop(0, nvis, score_tile, 0)
    thr = _dsa_threshold(key_ref, nvis, n_sel, idx_bits, tq, tk)
    for g in range(1, group):
        key_ref[i + g] = jnp.full((tq, tk), INT_MIN, I32)

    m_ref[...] = jnp.full(m_ref.shape, NEG, F32)
    acc_ref[...] = jnp.zeros(acc_ref.shape, F32)

    def attend_group(j, carry):
        k0 = j * group
        ks = pl.ds(pl.multiple_of(k0 * tk, group * tk), group * tk)
        key_tiles = [key_ref[k0 + g] for g in range(group)]
        bias_tiles = [bias_ref[jnp.clip(i - k0 - g, 0, N_BIAS_VARIANTS - 1)] for g in range(group)]
        kv_of_head = lambda n: (kv_ref[ks, _cols(n)], kv_ref[ks, KVA + HEAD_DIM * n:KVA + HEAD_DIM * (n + 1)])
        _dsa_attend_tiles(qs_ref, key_tiles, thr, bias_tiles, kv_of_head, m_ref, acc_ref, tq)
        return carry

    lax.fori_loop(0, i // group + 1, attend_group, 0)
    _dsa_finish(acc_ref, gate_ref, o_ref, tq)


def dsa_prompt(qcat, kiw, ki, kv, gcat, bias, *, n_sel):
    b, t, _ = qcat.shape
    tq = tk = LANES
    group = DSA_TILES_PER_STEP
    nk = t // tk
    assert nk % group == 0
    kern = functools.partial(_dsa_prompt_kernel, tq=tq, tk=tk, group=group, n_sel=n_sel,
                             idx_bits=int(t).bit_length())
    return pl.pallas_call(
        kern,
        grid=(b, t // tq),
        in_specs=[
            pl.BlockSpec((None, tq, QA), lambda bi, i: (bi, i, 0)),
            pl.BlockSpec((None, tq, QI), lambda bi, i: (bi, i, 2 * QA // QI)),
            pl.BlockSpec((None, tq, LANES), lambda bi, i: (bi, i, 0)),
            _resident((None, t, D_IDX), lambda bi, i: (bi, 0, 0)),
            _resident((None, t, 2 * KVA), lambda bi, i: (bi, 0, 0)),
            pl.BlockSpec((None, tq, QA), lambda bi, i: (bi, i, 0)),
            _resident((N_BIAS_VARIANTS, H_A * tq, tk), lambda bi, i: (0, 0, 0)),
        ],
        out_specs=pl.BlockSpec((None, tq, QA), lambda bi, i: (bi, i, 0)),
        out_shape=jax.ShapeDtypeStruct((b, t, QA), BF16),
        scratch_shapes=_dsa_scratch(nk + group, tq, tk),
        compiler_params=_cparams("parallel", "arbitrary"),
        name="dsa_prompt",
    )(qcat, qcat, kiw, ki, kv, gcat, bias)


def _dsa_sample_kernel(q_ref, qi_ref, w_ref, kin_ref, kvn_ref, gate_ref, bias_ref, kic_ref, kvc_ref, o_ref,
                       qs_ref, qih_ref, wb_ref, key_ref, m_ref, acc_ref, thr_ref, *, tq, p0, pt, n_sel, idx_bits):
    kt = pl.program_id(1)
    tk = TK_SAMPLE
    n_cache = p0 // tk
    sub = pt // tk

    @pl.when(kt == 0)
    def _():
        new_visible = lax.broadcasted_iota(I32, (tq, tk), 1) < tq
        _dsa_prepare(q_ref, qi_ref, w_ref, qs_ref, qih_ref, wb_ref, tq)

        def score_tile(k, carry):
            ks = pl.ds(pl.multiple_of(k * tk, tk), tk)
            key_ref[k] = _dsa_scores_to_keys(qih_ref, wb_ref, kic_ref[ks, :].astype(BF16), True)
            return carry

        lax.fori_loop(0, n_cache, score_tile, 0)
        ki_new = _pad_new_tile(kin_ref[:, :D_IDX].astype(BF16), tk)
        key_ref[n_cache] = _dsa_scores_to_keys(qih_ref, wb_ref, ki_new, new_visible)
        thr = _dsa_threshold(key_ref, n_cache + 1, n_sel, idx_bits, tq, tk)
        thr_ref[...] = jnp.broadcast_to(thr, thr_ref.shape)
        m_ref[...] = jnp.full(m_ref.shape, NEG, F32)
        acc_ref[...] = jnp.zeros(acc_ref.shape, F32)

    thr = thr_ref[...]
    k0 = kt * sub
    key_tiles = [key_ref[k0 + j] for j in range(sub)]
    bias_tiles = [bias_ref[jnp.minimum(n_cache - k0 - j, N_BIAS_VARIANTS - 1)] for j in range(sub)]
    kv_cached = lambda n: (kvc_ref[:, 0, n, :].astype(BF16), kvc_ref[:, 1, n, :].astype(BF16))
    _dsa_attend_tiles(qs_ref, key_tiles, thr, bias_tiles, kv_cached, m_ref, acc_ref, tq)

    @pl.when(kt == pl.num_programs(1) - 1)
    def _():
        kv_new = lambda n: (_pad_new_tile(kvn_ref[:, _cols(n)], tk),
                            _pad_new_tile(kvn_ref[:, KVA + HEAD_DIM * n:KVA + HEAD_DIM * (n + 1)], tk))
        _dsa_attend_tiles(qs_ref, [key_ref[n_cache]], thr, [bias_ref[0]], kv_new, m_ref, acc_ref, tq)
        _dsa_finish(acc_ref, gate_ref, o_ref, tq)


def dsa_sample(qcat, kiw, kv_new, gcat, bias, cache_kidx, cache_kv, layer, *, n_sel):
    b, tq, _ = qcat.shape
    p0 = cache_kv.shape[2]
    tk, pt = TK_SAMPLE, PT_SAMPLE
    assert p0 % pt == 0 and tq <= tk
    nk = p0 // tk + 1
    kern = functools.partial(_dsa_sample_kernel, tq=tq, p0=p0, pt=pt, n_sel=n_sel,
                             idx_bits=int(p0 + tk).bit_length())
    return pl.pallas_call(
        kern,
        grid=(b, p0 // pt),
        in_specs=[
            pl.BlockSpec((None, tq, QA), lambda bi, kt: (bi, 0, 0)),
            pl.BlockSpec((None, tq, QI), lambda bi, kt: (bi, 0, 2 * QA // QI)),
            pl.BlockSpec((None, tq, LANES), lambda bi, kt: (bi, 0, 0)),
            pl.BlockSpec((None, tq, LANES), lambda bi, kt: (bi, 0, 0)),
            pl.BlockSpec((None, tq, 2 * KVA), lambda bi, kt: (bi, 0, 0)),
            pl.BlockSpec((None, tq, QA), lambda bi, kt: (bi, 0, 0)),
            pl.BlockSpec((N_BIAS_VARIANTS, H_A * tq, tk), lambda bi, kt: (0, 0, 0)),
            pl.BlockSpec((None, None, p0, D_IDX), lambda bi, kt: (layer, bi, 0, 0)),
            pl.BlockSpec((None, None, pt, 2, KV_A, HEAD_DIM), lambda bi, kt: (layer, bi, kt, 0, 0, 0)),
        ],
        out_specs=pl.BlockSpec((None, tq, QA), lambda bi, kt: (bi, 0, 0)),
        out_shape=jax.ShapeDtypeStruct((b, tq, QA), BF16),
        scratch_shapes=_dsa_scratch(nk, tq, tk) + [pltpu.VMEM((tq, tk), I32)],
        compiler_params=_cparams("parallel", "arbitrary"),
        name="dsa_sample",
    )(qcat, qcat, kiw, kiw, kv_new, gcat, bias, cache_kidx, cache_kv)


def _diff_prompt_kernel(q_ref, k_ref, v_ref, gate_ref, bias_ref, lam_ref, subln_ref, o_ref,
                        m_ref, acc_ref, *, tq, tk, hb, lam_init):
    i = pl.program_id(2)
    row_chunk = (i * tq + lax.broadcasted_iota(I32, (tq, tk), 0)) >> CHUNK_SHIFT
    lane = lax.broadcasted_iota(I32, (tq, tk), 1)
    m_ref[...] = jnp.full(m_ref.shape, NEG, F32)
    acc_ref[...] = jnp.zeros(acc_ref.shape, F32)

    def step(k, masked):
        ks = pl.ds(pl.multiple_of(k * tk, tk), tk)
        variant = jnp.minimum(i - k, N_BIAS_VARIANTS - 1)
        madd = jnp.where(((k * tk + lane) >> CHUNK_SHIFT) <= row_chunk, 0.0, NEG) if masked else None
        loaded = []
        for h in range(hb):
            bias = bias_ref[variant, h]
            if masked:
                bias = bias + madd
            vmat = v_ref[ks, _cols(h, DV_B)]
            for c in range(2):
                col = slice(DV_B * h + HEAD_DIM * c, DV_B * h + HEAD_DIM * (c + 1))
                loaded.append((h, c, q_ref[:, col], k_ref[ks, col], vmat, bias, m_ref[h, c], acc_ref[h, c]))
        scores = [_qk(q, kmat) * ATT_SCALE + bias for _, _, q, kmat, _, bias, _, _ in loaded]
        new = _softmax_tiles([(s, vmat, m_prev, acc_prev)
                              for s, (_, _, _, _, vmat, _, m_prev, acc_prev) in zip(scores, loaded)], mxu_rowsum=False)
        for (h, c, *_), (m_new, acc_new) in zip(loaded, new):
            m_ref[h, c], acc_ref[h, c] = m_new, acc_new

    def full_tile(k, carry):
        step(k, False)
        return carry

    lax.fori_loop(0, i, full_tile, 0)
    step(i, True)
    for h in range(hb):
        cs = _cols(h, DV_B)
        o_ref[:, cs] = _diff_finish(acc_ref[h, 0], acc_ref[h, 1], lam_ref, subln_ref, gate_ref[:, cs],
                                    lam_init).astype(o_ref.dtype)


def diff_prompt(qcat, kv, gcat, bias, lam_vec, subln, *, tq, hb, lam_init):
    b, t, _ = qcat.shape
    tk = tq
    w = hb * DV_B
    kern = functools.partial(_diff_prompt_kernel, tq=tq, tk=tk, hb=hb, lam_init=lam_init)
    return pl.pallas_call(
        kern,
        grid=(b, H_B // hb, t // tq),
        in_specs=[
            pl.BlockSpec((None, tq, w), lambda bi, g, i: (bi, i, QA // w + g)),
            _resident((None, t, w), lambda bi, g, i: (bi, 0, g)),
            _resident((None, t, w), lambda bi, g, i: (bi, 0, QA // w + g)),
            pl.BlockSpec((None, tq, w), lambda bi, g, i: (bi, i, QA // w + g)),
            _resident((N_BIAS_VARIANTS, hb, tq, tk), lambda bi, g, i: (0, g, 0, 0)),
            pl.BlockSpec((4, HEAD_DIM), lambda bi, g, i: (0, 0)),
            pl.BlockSpec((1, DV_B), lambda bi, g, i: (0, 0)),
        ],
        out_specs=pl.BlockSpec((None, tq, w), lambda bi, g, i: (bi, i, g)),
        out_shape=jax.ShapeDtypeStruct((b, t, QA), BF16),
        scratch_shapes=[pltpu.VMEM((hb, 2, tq, LANES), F32), pltpu.VMEM((hb, 2, tq, DV_B + LANES), F32)],
        compiler_params=_cparams("parallel", "parallel", "arbitrary"),
        name="diff_prompt",
    )(qcat, kv, kv, gcat, bias, lam_vec, subln.reshape(1, DV_B))


def _cache_rows_2d(cache):
    n, b, p, two, h, d = cache.shape
    assert two == 2 and h % 8 == 0
    return cache.reshape(n, b, p * two * h, d)


def _head_rows(c_ref, key0, n_keys, n_heads, sel, h):
    period = 2 * n_heads
    return c_ref[pl.ds(key0 * period + sel * n_heads + h, n_keys, stride=period), :]


def _diff_sample_kernel(q_ref, kn_ref, vn_ref, gate_ref, bias_ref, lam_ref, subln_ref, c0_ref, c1_ref, o_ref,
                        m_ref, acc_ref, *, tq, p0, pt, lam_init):
    kt = pl.program_id(1)
    tk = TK_SAMPLE
    n_cache = p0 // tk
    sub = pt // tk

    @pl.when(kt == 0)
    def _():
        m_ref[...] = jnp.full(m_ref.shape, NEG, F32)
        acc_ref[...] = jnp.zeros(acc_ref.shape, F32)

    def update_all(tile_of_head):
        loaded = []
        for h in range(H_B):
            kmat2, vmat, bias = tile_of_head(h)
            for c in range(2):
                q = q_ref[:, DV_B * h + HEAD_DIM * c:DV_B * h + HEAD_DIM * (c + 1)]
                loaded.append((h, c, q, kmat2[:, _cols(c)], vmat, bias, m_ref[h, c], acc_ref[h, c]))
        scores = [_qk(q, kmat) * ATT_SCALE + bias for _, _, q, kmat, _, bias, _, _ in loaded]
        new = _softmax_tiles([(s, vmat, m_prev, acc_prev)
                              for s, (_, _, _, _, vmat, _, m_prev, acc_prev) in zip(scores, loaded)], mxu_rowsum=False)
        for (h, c, *_), (m_new, acc_new) in zip(loaded, new):
            m_ref[h, c], acc_ref[h, c] = m_new, acc_new

    def cache_tile(j, carry):
        variant = jnp.minimum(n_cache - (kt * sub + j), N_BIAS_VARIANTS - 1)
        halves = lambda sel, h: jnp.concatenate(
            [_head_rows(c_ref, j * tk, tk, H_B, sel, h).astype(BF16) for c_ref in (c0_ref, c1_ref)], axis=1)
        update_all(lambda h: (halves(0, h), halves(1, h), bias_ref[variant, h]))
        return carry

    lax.fori_loop(0, sub, cache_tile, 0)

    @pl.when(kt == pl.num_programs(1) - 1)
    def _():
        new_madd = jnp.where(lax.broadcasted_iota(I32, (tq, tk), 1) < tq, 0.0, NEG)
        update_all(lambda h: (_pad_new_tile(kn_ref[:, _cols(h, DV_B)], tk), _pad_new_tile(vn_ref[:, _cols(h, DV_B)], tk),
                              bias_ref[0, h] + new_madd))
        for h in range(H_B):
            cs = _cols(h, DV_B)
            o_ref[:, cs] = _diff_finish(acc_ref[h, 0], acc_ref[h, 1], lam_ref, subln_ref, gate_ref[:, cs],
                                        lam_init).astype(o_ref.dtype)


def diff_sample(qcat, kv_new, gcat, bias, lam_vec, subln, cache_kv, layer, *, lam_init):
    b, tq, _ = qcat.shape
    p0 = cache_kv.shape[2]
    tk, pt = TK_SAMPLE, PT_SAMPLE
    assert p0 % pt == 0 and tq <= tk
    kern = functools.partial(_diff_sample_kernel, tq=tq, p0=p0, pt=pt, lam_init=lam_init)
    cache_spec = lambda c: pl.BlockSpec((None, None, pt * 2 * H_B, LANES), lambda bi, kt: (layer, bi, kt, c))
    cache_2d = _cache_rows_2d(cache_kv)
    return pl.pallas_call(
        kern,
        grid=(b, p0 // pt),
        in_specs=[
            pl.BlockSpec((None, tq, QA), lambda bi, kt: (bi, 0, 1)),
            pl.BlockSpec((None, tq, QA), lambda bi, kt: (bi, 0, 0)),
            pl.BlockSpec((None, tq, QA), lambda bi, kt: (bi, 0, 1)),
            pl.BlockSpec((None, tq, QA), lambda bi, kt: (bi, 0, 1)),
            pl.BlockSpec((N_BIAS_VARIANTS, H_B, tq, tk), lambda bi, kt: (0, 0, 0, 0)),
            pl.BlockSpec((4, HEAD_DIM), lambda bi, kt: (0, 0)),
            pl.BlockSpec((1, DV_B), lambda bi, kt: (0, 0)),
            cache_spec(0), cache_spec(1),
        ],
        out_specs=pl.BlockSpec((None, tq, QA), lambda bi, kt: (bi, 0, 0)),
        out_shape=jax.ShapeDtypeStruct((b, tq, QA), BF16),
        scratch_shapes=[pltpu.VMEM((H_B, 2, tq, LANES), F32), pltpu.VMEM((H_B, 2, tq, DV_B + LANES), F32)],
        compiler_params=_cparams("parallel", "arbitrary"),
        name="diff_sample",
    )(qcat, kv_new, kv_new, gcat, bias, lam_vec, subln.reshape(1, DV_B), cache_2d, cache_2d)


def _stick_prompt_kernel(q_ref, k_ref, v_ref, gate_ref, tri_ref, o_ref, acc_ref, carry_ref, *, tq, tk, hb):
    i = pl.program_id(2)
    before = lax.broadcasted_iota(I32, (tq, tk), 1) < lax.broadcasted_iota(I32, (tq, tk), 0)
    acc_ref[...] = jnp.zeros(acc_ref.shape, F32)
    carry_ref[...] = jnp.zeros(carry_ref.shape, F32)

    def step(k, mask):
        ks = pl.ds(pl.multiple_of(k * tk, tk), tk)
        tri = tri_ref[:tk, :] if mask is None else tri_ref[...]
        loaded = [(q_ref[:, _cols(h)], k_ref[ks, _cols(h)], v_ref[ks, _cols(h)], carry_ref[h], acc_ref[h])
                  for h in range(hb)]
        new = _stick_tiles([(q, kmat, vmat, carry_prev) for q, kmat, vmat, carry_prev, _ in loaded], tri, mask)
        for h, ((pv, carry), (_, _, _, _, acc)) in enumerate(zip(new, loaded)):
            acc_ref[h], carry_ref[h] = acc + pv, carry

    step(i, before)

    def full_tile(j, carry):
        step(i - 1 - j, None)
        return carry

    lax.fori_loop(0, i, full_tile, 0)
    for h in range(hb):
        cs = _cols(h)
        o_ref[:, cs] = (acc_ref[h] * _silu(gate_ref[:, cs])).astype(o_ref.dtype)


def stick_prompt(qcat, kv, gcat, tri_ext, *, tq, hb):
    b, t, _ = qcat.shape
    tk = tq
    w = hb * HEAD_DIM
    kern = functools.partial(_stick_prompt_kernel, tq=tq, tk=tk, hb=hb)
    return pl.pallas_call(
        kern,
        grid=(b, H_C // hb, t // tq),
        in_specs=[
            pl.BlockSpec((None, tq, w), lambda bi, g, i: (bi, i, g)),
            _resident((None, t, w), lambda bi, g, i: (bi, 0, g)),
            _resident((None, t, w), lambda bi, g, i: (bi, 0, QA // w + g)),
            pl.BlockSpec((None, tq, w), lambda bi, g, i: (bi, i, g)),
            _resident((2 * tk, tk + LANES), lambda bi, g, i: (0, 0)),
        ],
        out_specs=pl.BlockSpec((None, tq, w), lambda bi, g, i: (bi, i, g)),
        out_shape=jax.ShapeDtypeStruct((b, t, QA), BF16),
        scratch_shapes=[pltpu.VMEM((hb, tq, HEAD_DIM), F32), pltpu.VMEM((hb, tq, LANES), F32)],
        compiler_params=_cparams("parallel", "parallel", "arbitrary"),
        name="stick_prompt",
    )(qcat, kv, kv, gcat, tri_ext)


def _stick_sample_kernel(q_ref, kn_ref, vn_ref, gate_ref, tri_ref, c_ref, o_ref, acc_ref, carry_ref, *, tq, pt):
    kt = pl.program_id(1)
    tk = TK_SAMPLE
    sub = pt // tk
    head_groups = [range(g * HB_SAMPLE, (g + 1) * HB_SAMPLE) for g in range(H_C // HB_SAMPLE)]

    @pl.when(kt == 0)
    def _():
        before = lax.broadcasted_iota(I32, (tq, tk), 1) < lax.broadcasted_iota(I32, (tq, tk), 0)
        tri = tri_ref[...]
        for heads in head_groups:
            new = _stick_tiles([(q_ref[:, _cols(h)], _pad_new_tile(kn_ref[:, _cols(h)], tk),
                                 _pad_new_tile(vn_ref[:, _cols(h)], tk), jnp.zeros((tq, LANES), F32))
                                for h in heads], tri, before)
            for h, (pv, carry) in zip(heads, new):
                acc_ref[h], carry_ref[h] = pv, carry

    def cache_tile(j, carry):
        key0 = (sub - 1 - j) * tk
        tri = tri_ref[:tk, :]
        for heads in head_groups:
            loaded = [(q_ref[:, _cols(h)], _head_rows(c_ref, key0, tk, H_C, 0, h).astype(BF16),
                       _head_rows(c_ref, key0, tk, H_C, 1, h).astype(BF16), carry_ref[h], acc_ref[h]) for h in heads]
            new = _stick_tiles([(q, kmat, vmat, c_prev) for q, kmat, vmat, c_prev, _ in loaded], tri, None)
            for h, (pv, c), (_, _, _, _, acc) in zip(heads, new, loaded):
                acc_ref[h], carry_ref[h] = acc + pv, c
        return carry

    lax.fori_loop(0, sub, cache_tile, 0)

    @pl.when(kt == pl.num_programs(1) - 1)
    def _():
        for h in range(H_C):
            cs = _cols(h)
            o_ref[:, cs] = (acc_ref[h] * _silu(gate_ref[:, cs])).astype(o_ref.dtype)


def stick_sample(qcat, kv_new, gcat, tri_ext, cache_kv, layer):
    b, tq, _ = qcat.shape
    p0 = cache_kv.shape[2]
    pt = PT_SAMPLE
    assert p0 % pt == 0 and tq <= TK_SAMPLE
    n_kt = p0 // pt
    kern = functools.partial(_stick_sample_kernel, tq=tq, pt=pt)
    return pl.pallas_call(
        kern,
        grid=(b, n_kt),
        in_specs=[
            pl.BlockSpec((None, tq, QA), lambda bi, kt: (bi, 0, 0)),
            pl.BlockSpec((None, tq, QA), lambda bi, kt: (bi, 0, 0)),
            pl.BlockSpec((None, tq, QA), lambda bi, kt: (bi, 0, 1)),
            pl.BlockSpec((None, tq, QA), lambda bi, kt: (bi, 0, 0)),
            pl.BlockSpec((2 * TK_SAMPLE, TK_SAMPLE + LANES), lambda bi, kt: (0, 0)),
            pl.BlockSpec((None, None, pt * 2 * H_C, HEAD_DIM), lambda bi, kt: (layer, bi, n_kt - 1 - kt, 0)),
        ],
        out_specs=pl.BlockSpec((None, tq, QA), lambda bi, kt: (bi, 0, 0)),
        out_shape=jax.ShapeDtypeStruct((b, tq, QA), BF16),
        scratch_shapes=[pltpu.VMEM((H_C, tq, HEAD_DIM), F32), pltpu.VMEM((H_C, tq, LANES), F32)],
        compiler_params=_cparams("parallel", "arbitrary"),
        name="stick_sample",
    )(qcat, kv_new, kv_new, gcat, tri_ext, _cache_rows_2d(cache_kv))


def _band_prompt_kernel(q_ref, k_ref, v_ref, gate_ref, bias_ref, o_ref, *, tq, win, hb):
    i = pl.program_id(2)
    span = win + tq
    rows = pl.ds(pl.multiple_of(i * tq, tq), span)
    row_chunk = (i * tq + lax.broadcasted_iota(I32, (tq, span), 0)) >> CHUNK_SHIFT
    key_chunk = (i * tq - win + lax.broadcasted_iota(I32, (tq, span), 1)) >> CHUNK_SHIFT
    visible = (key_chunk <= row_chunk) & (key_chunk >= jnp.maximum(row_chunk - BAND_CHUNKS, 0))
    madd = jnp.where(visible, 0.0, NEG)
    ones = jnp.ones((span, LANES), BF16)
    for h in range(hb):
        cs = _cols(h)
        s = _qk(q_ref[:, cs], k_ref[rows, cs]) * ATT_SCALE + bias_ref[h] + madd
        p = jnp.exp(s - jnp.max(s, axis=1, keepdims=True))
        pv = jnp.dot(p.astype(BF16), jnp.concatenate([v_ref[rows, cs], ones], axis=1), preferred_element_type=F32)
        o_ref[:, cs] = (_softmax_finish(pv) * _silu(gate_ref[:, cs])).astype(o_ref.dtype)


def band_prompt(qcat, kv, gcat, bias, *, hb):
    b, t, _ = qcat.shape
    tq = LANES
    win = BAND_CHUNKS * CHUNK
    w = hb * HEAD_DIM
    assert kv.shape[1] == win + t and bias.shape == (H_D, tq, win + tq)
    kern = functools.partial(_band_prompt_kernel, tq=tq, win=win, hb=hb)
    return pl.pallas_call(
        kern,
        grid=(b, H_D // hb, t // tq),
        in_specs=[
            pl.BlockSpec((None, tq, w), lambda bi, g, i: (bi, i, QA // w + g)),
            _resident((None, win + t, w), lambda bi, g, i: (bi, 0, g)),
            _resident((None, win + t, w), lambda bi, g, i: (bi, 0, QA // w + g)),
            pl.BlockSpec((None, tq, w), lambda bi, g, i: (bi, i, QA // w + g)),
            _resident((hb, tq, win + tq), lambda bi, g, i: (g, 0, 0)),
        ],
        out_specs=pl.BlockSpec((None, tq, w), lambda bi, g, i: (bi, i, g)),
        out_shape=jax.ShapeDtypeStruct((b, t, QA), BF16),
        compiler_params=_cparams("parallel", "parallel", "arbitrary"),
        name="band_prompt",
    )(qcat, kv, kv, gcat, bias)


def _band_sample_kernel(q_ref, kn_ref, vn_ref, gate_ref, bias_ref, c_ref, o_ref, *, tq, win):
    g = pl.program_id(1)
    tk = TK_SAMPLE
    lane = lax.broadcasted_iota(I32, (tq, win + tk), 1)
    madd = jnp.where(lane < win + tq, 0.0, NEG)
    ones = jnp.ones((win + tk, LANES), BF16)
    for h in range(HB_SAMPLE):
        cs = _cols(h)
        head = g * HB_SAMPLE + h
        kmat = jnp.concatenate([_head_rows(c_ref, 0, win, H_D, 0, head).astype(BF16),
                                _pad_new_tile(kn_ref[:, cs], tk)], axis=0)
        vmat = jnp.concatenate([_head_rows(c_ref, 0, win, H_D, 1, head).astype(BF16),
                                _pad_new_tile(vn_ref[:, cs], tk)], axis=0)
        s = _qk(q_ref[:, cs], kmat) * ATT_SCALE + bias_ref[h] + madd
        p = jnp.exp(s - jnp.max(s, axis=1, keepdims=True))
        pv = jnp.dot(p.astype(BF16), jnp.concatenate([vmat, ones], axis=1), preferred_element_type=F32)
        o_ref[:, cs] = (_softmax_finish(pv) * _silu(gate_ref[:, cs])).astype(o_ref.dtype)


def band_sample(qcat, kv_new, gcat, bias, cache_kv, layer):
    b, tq, _ = qcat.shape
    win = cache_kv.shape[2]
    w = HB_SAMPLE * HEAD_DIM
    assert win == BAND_CHUNKS * CHUNK and tq == CHUNK and bias.shape == (H_D, tq, win + TK_SAMPLE)
    kern = functools.partial(_band_sample_kernel, tq=tq, win=win)
    return pl.pallas_call(
        kern,
        grid=(b, H_D // HB_SAMPLE),
        in_specs=[
            pl.BlockSpec((None, tq, w), lambda bi, g: (bi, 0, QA // w + g)),
            pl.BlockSpec((None, tq, w), lambda bi, g: (bi, 0, g)),
            pl.BlockSpec((None, tq, w), lambda bi, g: (bi, 0, QA // w + g)),
            pl.BlockSpec((None, tq, w), lambda bi, g: (bi, 0, QA // w + g)),
            pl.BlockSpec((HB_SAMPLE, tq, win + TK_SAMPLE), lambda bi, g: (g, 0, 0)),
            pl.BlockSpec((None, None, win * 2 * H_D, HEAD_DIM), lambda bi, g: (layer, bi, 0, 0)),
        ],
        out_specs=pl.BlockSpec((None, tq, w), lambda bi, g: (bi, 0, g)),
        out_shape=jax.ShapeDtypeStruct((b, tq, QA), BF16),
        compiler_params=_cparams("parallel", "arbitrary"),
        name="band_sample",
    )(qcat, kv_new, kv_new, gcat, bias, _cache_rows_2d(cache_kv))


def _t5_bucket_np(rel):
    half = T5_BUCKETS // 2
    max_exact = half // 2
    n = np.abs(rel)
    nf = np.maximum(n, 1).astype(np.float64)
    large = max_exact + (np.log(nf / max_exact) / math.log(T5_MAX_DIST / max_exact)
                         * (half - max_exact)).astype(np.int32)
    large = np.minimum(large, half - 1)
    return np.where(rel < 0, half, 0) + np.where(n < max_exact, n, large)


def _band_index_np(rel):
    return np.clip(rel, -REL_CLIP, REL_CLIP) + REL_CLIP


def toeplitz_bias(table, index_of_rel, r0, n_rows, n_cols):
    rel = r0 + np.arange(n_rows)[:, None] - np.arange(n_cols)[None, :]
    idx = jnp.asarray(index_of_rel(rel).reshape(-1).astype(np.int32))
    onehot = (idx[None, :] == jnp.arange(table.shape[0], dtype=I32)[:, None]).astype(F32)
    out = jnp.einsum("bh,bn->hn", table, onehot, precision=lax.Precision.HIGHEST)
    return out.reshape(table.shape[1], n_rows, n_cols)


def bias_tiles(table, index_of_rel, tq, tk):
    far = (N_BIAS_VARIANTS - 1) * tk
    assert (index_of_rel(np.arange(far - tk + 1, far + tq)) == index_of_rel(np.array(far + tq))).all()
    return jnp.stack([toeplitz_bias(table, index_of_rel, v * tk, tq, tk) for v in range(N_BIAS_VARIANTS)], axis=0)


def band_window_bias(rel_tab, tq, win):
    far_cols = win - REL_CLIP
    return jnp.concatenate(
        [jnp.broadcast_to(rel_tab[2 * REL_CLIP][:, None, None], (H_D, tq, far_cols)),
         toeplitz_bias(rel_tab, _band_index_np, REL_CLIP, tq, win + LANES - far_cols)], axis=-1)


def _tri_ext(tk):
    tri = np.tril(np.ones((tk, tk), np.float32), -1)
    ext = np.concatenate([tri, np.ones((tk, LANES), np.float32)], axis=1)
    return jnp.asarray(np.concatenate([ext, ext], axis=0), BF16)


def _even_weights(w_in):
    c = np.cumsum((0, QA, KVA, KVA, QI, D_IDX, H_IDX, QA, QA, QA, QA, QA)).tolist()
    aq, ak, av, aqi, aki, aw, ag, bq, bk, bv, bg = (w_in[:, c[r]:c[r + 1]] for r in range(11))
    pad = jnp.zeros((w_in.shape[0], LANES - D_IDX - H_IDX), w_in.dtype)
    cat = lambda *xs: jnp.concatenate(xs, axis=1).astype(BF16)
    return cat(aq, bq, aqi), cat(ak, av), cat(bk, bv), cat(aki, aw, pad), cat(ag, bg)


def _odd_weights(w_in):
    cq, ck, cv, cg, dq, dk, dv, dg = (w_in[:, QA * r:QA * (r + 1)] for r in range(8))
    cat = lambda *xs: jnp.concatenate(xs, axis=1).astype(BF16)
    return cat(cq, dq), cat(ck, cv), cat(dk, dv), cat(cg, dg)


def _even_layer(h, caches, layer, weights, t5_tab, lam_vec, subln, lam_init):
    w_q, w_akv, w_bkv, w_kiw, w_g = weights
    b, t, d = h.shape
    h2 = h.reshape(b * t, d)
    (qcat,) = matmul(h2, w_q, (BF16,))
    akv, akv_b = matmul(h2, w_akv, (F32, BF16))
    bkv, bkv_b = matmul(h2, w_bkv, (F32, BF16))
    (kiw,) = matmul(h2, w_kiw, (F32,))
    (gcat,) = matmul(h2, w_g, (F32,))
    qcat, akv_b, bkv_b, kiw, gcat = (x.reshape(b, t, -1) for x in (qcat, akv_b, bkv_b, kiw, gcat))
    aki = kiw[..., :D_IDX]
    t5_a, t5_b = t5_tab[:, :H_A], t5_tab[:, H_A:]
    if caches is None:
        tq_b = 256
        bias_a = bias_tiles(t5_a, _t5_bucket_np, LANES, LANES).reshape(N_BIAS_VARIANTS, H_A * LANES, LANES)
        o_a = dsa_prompt(qcat, kiw, aki.astype(BF16), akv_b, gcat, bias_a, n_sel=min(TOPK_MAX, t // 4))
        o_b = diff_prompt(qcat, bkv_b, gcat, bias_tiles(t5_b, _t5_bucket_np, tq_b, tq_b), lam_vec, subln,
                          tq=tq_b, hb=4, lam_init=lam_init)
    else:
        cache_a, cache_ki, cache_b = caches
        p0 = cache_a.shape[2]
        bias_a = bias_tiles(t5_a, _t5_bucket_np, t, TK_SAMPLE).reshape(N_BIAS_VARIANTS, H_A * t, TK_SAMPLE)
        o_a = dsa_sample(qcat, kiw, akv_b, gcat, bias_a, cache_ki, cache_a, layer,
                         n_sel=min(TOPK_MAX, (p0 + t) // 4))
        o_b = diff_sample(qcat, bkv_b, gcat, bias_tiles(t5_b, _t5_bucket_np, t, TK_SAMPLE), lam_vec, subln,
                          cache_b, layer, lam_init=lam_init)
    mixed = jnp.concatenate([o_a, o_b], axis=-1).reshape(b * t, 2 * QA)
    return mixed, akv.reshape(b, t, -1), aki, bkv.reshape(b, t, -1)


def _odd_layer(h, caches, layer, weights, rel_tab):
    w_q, w_ckv, w_dkv, w_g = weights
    b, t, d = h.shape
    h2 = h.reshape(b * t, d)
    (qcat,) = matmul(h2, w_q, (BF16,))
    ckv, ckv_b = matmul(h2, w_ckv, (F32, BF16))
    dkv, dkv_b = matmul(h2, w_dkv, (F32, BF16))
    (gcat,) = matmul(h2, w_g, (F32,))
    qcat, ckv_b, dkv_b, gcat = (x.reshape(b, t, -1) for x in (qcat, ckv_b, dkv_b, gcat))
    win = BAND_CHUNKS * CHUNK
    if caches is None:
        tq_c = 256
        o_c = stick_prompt(qcat, ckv_b, gcat, _tri_ext(tq_c), tq=tq_c, hb=4)
        o_d = band_prompt(qcat, jnp.pad(dkv_b, ((0, 0), (win, 0), (0, 0))), gcat,
                          band_window_bias(rel_tab, LANES, win), hb=4)
    else:
        cache_c, cache_d = caches
        o_c = stick_sample(qcat, ckv_b, gcat, _tri_ext(TK_SAMPLE), cache_c, layer)
        o_d = band_sample(qcat, dkv_b, gcat, band_window_bias(rel_tab, t, win), cache_d, layer)
    mixed = jnp.concatenate([o_c, o_d], axis=-1).reshape(b * t, 2 * QA)
    return mixed, ckv.reshape(b, t, -1), dkv.reshape(b, t, -1)


def _finish_layer(x, mixed, w_out, p_i, g_post, w_proj, g_pl, w_gate):
    (mix,) = matmul(mixed, w_out, (F32,))
    x1, x1_b = residual_norm(x, mix, g_post)
    e = embed_norm(p_i.astype(BF16), w_proj, g_pl)
    return gate_matmul(x1_b, w_gate, x1, e)


def kernel(x_prompt, x_sample, p_prompt, p_sample, cache_a_kv, cache_a_kidx, cache_b_kv, cache_c_kv,
           cache_d_kv, norm_pre, norm_post, w_in_even, w_out_even, t5_bias, diff_lambda, diff_subln,
           w_in_odd, w_out_odd, d_rel_bias, w_pl_proj, pl_norm, w_pl_gate):
    bp, tp, d = x_prompt.shape
    bs, ts, _ = x_sample.shape
    depth = norm_pre.shape[0]
    assert d == D_MODEL and ts == CHUNK and tp % 256 == 0
    xp = x_prompt.reshape(bp * tp, d)
    xs = x_sample.reshape(bs * ts, d)
    outs = {name: [] for name in ("a_kv_p", "a_kv_s", "a_ki_p", "a_ki_s", "b_kv_p", "b_kv_s",
                                  "c_kv_p", "c_kv_s", "d_kv_p", "d_kv_s")}
    for i in range(depth):
        j = i // 2
        hp = rmsnorm_cast(xp, norm_pre[i]).reshape(bp, tp, d)
        hs = rmsnorm_cast(xs, norm_pre[i]).reshape(bs, ts, d)
        if i % 2 == 0:
            lam_init = 0.8 - 0.6 * math.exp(-0.3 * i)
            weights = _even_weights(w_in_even[j])
            w_out = w_out_even[j].astype(BF16)
            prm = (weights, t5_bias, diff_lambda[j], diff_subln[j], lam_init)
            mp, akv, aki, bkv = _even_layer(hp, None, j, *prm)
            outs["a_kv_p"].append(akv.reshape(bp, tp, 2, KV_A, HEAD_DIM))
            outs["a_ki_p"].append(aki)
            outs["b_kv_p"].append(bkv.reshape(bp, tp, 2, H_B, DV_B))
            ms, akv, aki, bkv = _even_layer(hs, (cache_a_kv, cache_a_kidx, cache_b_kv), j, *prm)
            outs["a_kv_s"].append(akv.reshape(bs, ts, 2, KV_A, HEAD_DIM))
            outs["a_ki_s"].append(aki)
            outs["b_kv_s"].append(bkv.reshape(bs, ts, 2, H_B, DV_B))
        else:
            weights = _odd_weights(w_in_odd[j])
            w_out = w_out_odd[j].astype(BF16)
            mp, ckv, dkv = _odd_layer(hp, None, j, weights, d_rel_bias[j])
            win_p = min(BAND_CHUNKS * CHUNK, tp)
            outs["c_kv_p"].append(ckv.reshape(bp, tp, 2, H_C, HEAD_DIM))
            outs["d_kv_p"].append(dkv[:, tp - win_p:].reshape(bp, win_p, 2, H_D, HEAD_DIM))
            ms, ckv, dkv = _odd_layer(hs, (cache_c_kv, cache_d_kv), j, weights, d_rel_bias[j])
            outs["c_kv_s"].append(ckv.reshape(bs, ts, 2, H_C, HEAD_DIM))
            full_d = jnp.concatenate([cache_d_kv[j], dkv.reshape(bs, ts, 2, H_D, HEAD_DIM)], axis=1)
            outs["d_kv_s"].append(full_d[:, ts:])
        fin = (norm_post[i], w_pl_proj[i].astype(BF16), pl_norm[i], w_pl_gate[i].astype(BF16))
        xp = _finish_layer(xp, mp, w_out, p_prompt[i].reshape(bp * tp, -1), *fin)
        xs = _finish_layer(xs, ms, w_out, p_sample[i].reshape(bs * ts, -1), *fin)
    st = lambda name: jnp.stack(outs[name])
    return (xp.reshape(bp, tp, d), xs.reshape(bs, ts, d), st("a_kv_p"), st("a_kv_s"), st("a_ki_p"), st("a_ki_s"),
            st("b_kv_p"), st("b_kv_s"), st("c_kv_p"), st("c_kv_s"), st("d_kv_p"), st("d_kv_s"))
```

```python
import functools
import math

import numpy as np
import jax
import jax.numpy as jnp
from jax import lax
from jax.experimental import pallas as pl
from jax.experimental.pallas import tpu as pltpu

F32 = jnp.float32
BF16 = jnp.bfloat16
I32 = jnp.int32

D_MODEL = 4096
CHUNK = 64
CHUNK_SHIFT = 6
HEAD_DIM = 128
H_A = 16
KV_A = 4
GROUP_A = H_A // KV_A
H_IDX = 16
D_IDX = 64
TOPK_MAX = 256
H_B = 8
H_C = 16
H_D = 16
BAND_CHUNKS = 8
REL_CLIP = 128
T5_BUCKETS = 32
T5_MAX_DIST = 128
RMS_EPS = 1e-6
QA = H_A * HEAD_DIM
KVA = KV_A * HEAD_DIM
QI = H_IDX * D_IDX
DV_B = 2 * HEAD_DIM
ATT_SCALE = HEAD_DIM ** -0.5
IDX_SCALE = (D_IDX ** -0.5) * (H_IDX ** -0.5)
LOG2E = math.log2(math.e)
QK_SCALE_BITS = ATT_SCALE * LOG2E

LANES = 128
NEG = -1e30
INT_MIN = -(2 ** 31)
VMEM_LIMIT = 56 * 1024 * 1024
N_BIAS_VARIANTS = 3
TK_SAMPLE = LANES
PT_SAMPLE = 512
HB_SAMPLE = 8
DSA_TILES_PER_STEP = 4
DSA_CHAINS_ABREAST = 4
DIFF_CHAINS_ABREAST = 4


def _cparams(*sem):
    return pltpu.CompilerParams(dimension_semantics=sem, vmem_limit_bytes=VMEM_LIMIT)


def _sigmoid(x):
    return 1.0 / (1.0 + jnp.exp(-x))


def _silu(x):
    return x * _sigmoid(x)


def _lane_tile(x, n):
    return x if n == 1 else jnp.concatenate([x] * n, axis=1)


def _cols(h, width=HEAD_DIM):
    return slice(width * h, width * (h + 1))


def _rmsnorm_cast_kernel(x_ref, g_ref, o_ref):
    x = x_ref[...]
    ms = jnp.mean(x * x, axis=-1, keepdims=True)
    o_ref[...] = (x * lax.rsqrt(ms + RMS_EPS) * g_ref[...]).astype(o_ref.dtype)


def rmsnorm_cast(x, g, bm=256):
    m, d = x.shape
    bm = min(bm, m)
    return pl.pallas_call(
        _rmsnorm_cast_kernel,
        grid=(m // bm,),
        in_specs=[pl.BlockSpec((bm, d), lambda i: (i, 0)), pl.BlockSpec((1, d), lambda i: (0, 0))],
        out_specs=pl.BlockSpec((bm, d), lambda i: (i, 0)),
        out_shape=jax.ShapeDtypeStruct((m, d), BF16),
        compiler_params=_cparams("parallel"),
        name="rmsnorm_cast",
    )(x, g.reshape(1, d))


def _residual_norm_kernel(x_ref, mix_ref, g_ref, o_ref, ob_ref):
    mix = mix_ref[...]
    ms = jnp.mean(mix * mix, axis=-1, keepdims=True)
    x1 = x_ref[...] + mix * lax.rsqrt(ms + RMS_EPS) * g_ref[...]
    o_ref[...] = x1
    ob_ref[...] = x1.astype(BF16)


def residual_norm(x, mix, g, bm=256):
    m, d = x.shape
    bm = min(bm, m)
    row = pl.BlockSpec((bm, d), lambda i: (i, 0))
    return pl.pallas_call(
        _residual_norm_kernel,
        grid=(m // bm,),
        in_specs=[row, row, pl.BlockSpec((1, d), lambda i: (0, 0))],
        out_specs=[row, row],
        out_shape=[jax.ShapeDtypeStruct((m, d), F32), jax.ShapeDtypeStruct((m, d), BF16)],
        compiler_params=_cparams("parallel"),
        name="residual_norm",
    )(x, mix, g.reshape(1, d))


def _embed_norm_kernel(p_ref, w_ref, g_ref, o_ref):
    y = jnp.dot(p_ref[...], w_ref[...], preferred_element_type=F32)
    ms = jnp.mean(y * y, axis=-1, keepdims=True)
    o_ref[...] = y * lax.rsqrt(ms + RMS_EPS) * g_ref[...]


def embed_norm(p, w, g, bm=256):
    m, k = p.shape
    d = w.shape[1]
    bm = min(bm, m)
    return pl.pallas_call(
        _embed_norm_kernel,
        grid=(m // bm,),
        in_specs=[pl.BlockSpec((bm, k), lambda i: (i, 0)), pl.BlockSpec((k, d), lambda i: (0, 0)),
                  pl.BlockSpec((1, d), lambda i: (0, 0))],
        out_specs=pl.BlockSpec((bm, d), lambda i: (i, 0)),
        out_shape=jax.ShapeDtypeStruct((m, d), F32),
        compiler_params=_cparams("parallel"),
        name="embed_norm",
    )(p, w, g.reshape(1, d))


def _mm_kernel(a_ref, w_ref, *o_refs):
    acc = jnp.dot(a_ref[...], w_ref[...], preferred_element_type=F32)
    for o_ref in o_refs:
        o_ref[...] = acc.astype(o_ref.dtype)


def matmul(a, w, out_dtypes, bm=1024, bn=1024):
    m, k = a.shape
    n = w.shape[1]
    bm, bn = min(bm, m), min(bn, n)
    o_spec = pl.BlockSpec((bm, bn), lambda i, j: (i, j))
    outs = pl.pallas_call(
        _mm_kernel,
        grid=(m // bm, n // bn),
        in_specs=[pl.BlockSpec((bm, k), lambda i, j: (i, 0)), pl.BlockSpec((k, bn), lambda i, j: (0, j))],
        out_specs=[o_spec] * len(out_dtypes),
        out_shape=[jax.ShapeDtypeStruct((m, n), dt) for dt in out_dtypes],
        compiler_params=_cparams("parallel", "parallel"),
        name="matmul",
    )(a, w)
    return outs


def _gate_mm_kernel(a_ref, w_ref, x_ref, e_ref, o_ref):
    logits = jnp.dot(a_ref[...], w_ref[...], preferred_element_type=F32)
    o_ref[...] = x_ref[...] + e_ref[...] * _sigmoid(logits)


def gate_matmul(a, w, x, e, bm=1024, bn=512):
    m, k = a.shape
    n = w.shape[1]
    bm, bn = min(bm, m), min(bn, n)
    tile = pl.BlockSpec((bm, bn), lambda i, j: (i, j))
    return pl.pallas_call(
        _gate_mm_kernel,
        grid=(m // bm, n // bn),
        in_specs=[pl.BlockSpec((bm, k), lambda i, j: (i, 0)), pl.BlockSpec((k, bn), lambda i, j: (0, j)),
                  tile, tile],
        out_specs=tile,
        out_shape=jax.ShapeDtypeStruct((m, n), F32),
        compiler_params=_cparams("parallel", "parallel"),
        name="gate_matmul",
    )(a, w, x, e)


def _resident(block_shape, index_map):
    return pl.BlockSpec(block_shape, index_map, pipeline_mode=pl.Buffered(1))


def _qk(q, kmat):
    return lax.dot_general(q, kmat, (((1,), (1,)), ((), ())), preferred_element_type=F32)


def _softmax_tiles(chains, mxu_rowsum=True, width=None):
    width = width or len(chains)
    if width < len(chains):
        return [r for i in range(0, len(chains), width) for r in _softmax_tiles(chains[i:i + width], mxu_rowsum)]
    chains = [(fn(), vmat, m_prev, acc_prev) for fn, vmat, m_prev, acc_prev in chains]
    tk = chains[0][0].shape[1]
    m_new = [jnp.maximum(m_prev, jnp.max(s, axis=1, keepdims=True)) for s, _, m_prev, _ in chains]
    alpha = [jnp.exp2(m_prev - m) for (_, _, m_prev, _), m in zip(chains, m_new)]
    p = [jnp.exp2(s - _lane_tile(m, tk // LANES)) for (s, _, _, _), m in zip(chains, m_new)]
    if mxu_rowsum:
        ones = jnp.ones((tk, LANES), BF16)
        pv = [jnp.dot(pi.astype(BF16), jnp.concatenate([vmat, ones], axis=1), preferred_element_type=F32)
              for pi, (_, vmat, _, _) in zip(p, chains)]
    else:
        rows = [jnp.broadcast_to(jnp.sum(pi, axis=1, keepdims=True), m.shape) for pi, m in zip(p, m_new)]
        pv = [jnp.concatenate([jnp.dot(pi.astype(BF16), vmat, preferred_element_type=F32), row], axis=1)
              for pi, (_, vmat, _, _), row in zip(p, chains, rows)]
    acc = [_lane_tile(a, x.shape[1] // LANES) * acc_prev + x for a, x, (_, _, _, acc_prev) in zip(alpha, pv, chains)]
    return list(zip(m_new, acc))


def _softmax_finish(acc):
    dv = acc.shape[1] - LANES
    return acc[:, :dv] / _lane_tile(acc[:, dv:], dv // LANES)


def _stick_tiles(chains, tri, before):
    tk = chains[0][1].shape[0]
    z = [_qk(q, kmat) * ATT_SCALE for q, kmat, _, _ in chains]
    softplus = [jnp.maximum(x, 0.0) + jnp.log(1.0 + jnp.exp(-jnp.abs(x))) for x in z]
    if before is None:
        sums = [jnp.dot(sp.astype(BF16), tri, preferred_element_type=F32) for sp in softplus]
    else:
        masked = [jnp.where(before, sp, 0.0) for sp in softplus]
        hi = [x.astype(BF16) for x in masked]
        lo = [(x - h.astype(F32)).astype(BF16) for x, h in zip(masked, hi)]
        sums = [jnp.dot(jnp.concatenate([h, l], axis=1), tri, preferred_element_type=F32) for h, l in zip(hi, lo)]
    w = [jnp.exp((x - sp) + s[:, :tk] + _lane_tile(carry, tk // LANES))
         for x, sp, s, (_, _, _, carry) in zip(z, softplus, sums, chains)]
    if before is not None:
        w = [jnp.where(before, x, 0.0) for x in w]
    pv = [jnp.dot(x.astype(BF16), vmat, preferred_element_type=F32) for x, (_, _, vmat, _) in zip(w, chains)]
    return [(p, carry + s[:, tk:]) for p, s, (_, _, _, carry) in zip(pv, sums, chains)]


def _diff_finish(acc0, acc1, lam_ref, subln_ref, gate, lam_init):
    lam_q1, lam_k1, lam_q2, lam_k2 = (lam_ref[r:r + 1, :] for r in range(4))
    lam = (jnp.exp(jnp.sum(lam_q1 * lam_k1, axis=1, keepdims=True))
           - jnp.exp(jnp.sum(lam_q2 * lam_k2, axis=1, keepdims=True)) + lam_init)
    o = _softmax_finish(acc0) - lam * _softmax_finish(acc1)
    ms = jnp.mean(o * o, axis=-1, keepdims=True)
    y = o * lax.rsqrt(ms + RMS_EPS) * subln_ref[...] * (1.0 - lam_init)
    return y * _silu(gate)


def _pad_new_tile(x, tk):
    return jnp.concatenate([x, jnp.zeros((tk - x.shape[0], x.shape[1]), x.dtype)], axis=0)


def _dsa_scores_to_keys(qih_ref, wb_ref, ki_tile, visible):
    tq = wb_ref.shape[1]
    s_all = _qk(qih_ref[...], ki_tile)
    acc = jnp.zeros((tq, ki_tile.shape[0]), F32)
    for h in range(H_IDX):
        acc = acc + jnp.maximum(s_all[h * tq:(h + 1) * tq], 0.0) * wb_ref[h]
    bits = pltpu.bitcast(acc, I32)
    key = bits ^ ((bits >> 31) & 0x7FFFFFFF)
    return jnp.where(visible, key, INT_MIN)


def _dsa_prepare(q_ref, qi_ref, w_ref, qs_ref, qih_ref, wb_ref, tq):
    wv = w_ref[:, D_IDX:D_IDX + H_IDX] * IDX_SCALE
    for h in range(H_IDX):
        qih_ref[h * tq:(h + 1) * tq, :] = qi_ref[:, _cols(h, D_IDX)]
        wb_ref[h] = jnp.broadcast_to(wv[:, h:h + 1], (tq, LANES))
    for n in range(KV_A):
        for g in range(GROUP_A):
            qs_ref[n, g * tq:(g + 1) * tq, :] = q_ref[:, _cols(GROUP_A * n + g)]


def _dsa_threshold(key_ref, nvis, n_sel, idx_bits, tq, tk):
    lane = lax.broadcasted_iota(I32, (tq, tk), 1)

    def count_where(pred):
        def body(k, cnt):
            return cnt + jnp.where(pred(k, key_ref[k]), 1.0, 0.0)
        cnt = lax.fori_loop(0, nvis, body, jnp.zeros((tq, tk), F32))
        return jnp.sum(cnt, axis=1, keepdims=True)

    def count_ge(cand):
        return count_where(lambda k, key: key >= cand)

    def thr_bit(b, t):
        cand = t + lax.shift_left(jnp.int32(1), 31 - b)
        return jnp.where(count_ge(cand) >= n_sel, cand, t)

    thr = lax.fori_loop(0, 32, thr_bit, jnp.full((tq, 1), INT_MIN, I32))
    thr = jnp.maximum(thr, INT_MIN + 1)

    @pl.when(jnp.max(count_ge(thr)) > n_sel)
    def _():
        need = n_sel - count_ge(thr + 1)

        def count_eq_below(c):
            return count_where(lambda k, key: (key == thr) & ((k * tk + lane) < c))

        def idx_bit(b, c):
            cand = c + lax.shift_left(jnp.int32(1), idx_bits - 1 - b)
            return jnp.where(count_eq_below(cand) <= need, cand, c)

        cut = lax.fori_loop(0, idx_bits, idx_bit, jnp.zeros((tq, 1), I32))

        def demote(k, carry):
            key = key_ref[k]
            key_ref[k] = jnp.where((key == thr) & ((k * tk + lane) >= cut), key - 1, key)
            return carry

        lax.fori_loop(0, nvis, demote, 0)

    return thr


def _dsa_attend_tiles(qs_ref, key_tiles, thr, bias_tiles, kv_of_head, m_ref, acc_ref, tq):
    rows4 = GROUP_A * tq
    madd = jnp.concatenate([jnp.where(key_tile >= thr, 0.0, NEG) for key_tile in key_tiles], axis=1)
    madd4 = jnp.concatenate([madd] * GROUP_A, axis=0)
    loaded = [(qs_ref[n], *kv_of_head(n), m_ref[n], acc_ref[n]) for n in range(KV_A)]
    def score_fn(n, q, kmat):
        def fn():
            bias = jnp.concatenate([b[rows4 * n:rows4 * (n + 1), :] for b in bias_tiles], axis=1)
            return _qk(q, kmat) * QK_SCALE_BITS + bias + madd4
        return fn

    new = _softmax_tiles([(score_fn(n, q, kmat), vmat, m_prev, acc_prev)
                          for n, (q, kmat, vmat, m_prev, acc_prev) in enumerate(loaded)], width=DSA_CHAINS_ABREAST)
    for n, (m_new, acc_new) in enumerate(new):
        m_ref[n], acc_ref[n] = m_new, acc_new


def _dsa_finish(acc_ref, gate_ref, o_ref, tq):
    for n in range(KV_A):
        o = _softmax_finish(acc_ref[n])
        for g in range(GROUP_A):
            cs = _cols(GROUP_A * n + g)
            o_ref[:, cs] = (o[g * tq:(g + 1) * tq] * _silu(gate_ref[:, cs])).astype(o_ref.dtype)


def _dsa_scratch(nk, tq, tk):
    rows4 = GROUP_A * tq
    return [
        pltpu.VMEM((KV_A, rows4, HEAD_DIM), BF16),
        pltpu.VMEM((H_IDX * tq, D_IDX), BF16),
        pltpu.VMEM((H_IDX, tq, LANES), F32),
        pltpu.VMEM((nk, tq, tk), I32),
        pltpu.VMEM((KV_A, rows4, LANES), F32),
        pltpu.VMEM((KV_A, rows4, HEAD_DIM + LANES), F32),
    ]


def _dsa_prompt_kernel(q_ref, qi_ref, w_ref, ki_ref, kv_ref, gate_ref, bias_ref, o_ref,
                       qs_ref, qih_ref, wb_ref, key_ref, m_ref, acc_ref, *, tq, tk, group, n_sel, idx_bits):
    i = pl.program_id(1)
    nvis = i + 1
    row_chunk = (i * tq + lax.broadcasted_iota(I32, (tq, tk), 0)) >> CHUNK_SHIFT
    lane = lax.broadcasted_iota(I32, (tq, tk), 1)
    _dsa_prepare(q_ref, qi_ref, w_ref, qs_ref, qih_ref, wb_ref, tq)

    def score_tile(k, carry):
        ks = pl.ds(pl.multiple_of(k * tk, tk), tk)
        visible = ((k * tk + lane) >> CHUNK_SHIFT) <= row_chunk
        key_ref[k] = _dsa_scores_to_keys(qih_ref, wb_ref, ki_ref[ks, :], visible)
        return carry

    lax.fori_loop(0, nvis, score_tile, 0)
    thr = _dsa_threshold(key_ref, nvis, n_sel, idx_bits, tq, tk)
    for g in range(1, group):
        key_ref[i + g] = jnp.full((tq, tk), INT_MIN, I32)

    m_ref[...] = jnp.full(m_ref.shape, NEG, F32)
    acc_ref[...] = jnp.zeros(acc_ref.shape, F32)

    def attend_group(j, carry):
        k0 = j * group
        ks = pl.ds(pl.multiple_of(k0 * tk, group * tk), group * tk)
        key_tiles = [key_ref[k0 + g] for g in range(group)]
        bias_tiles = [bias_ref[jnp.clip(i - k0 - g, 0, N_BIAS_VARIANTS - 1)] for g in range(group)]
        kv_of_head = lambda n: (kv_ref[ks, _cols(n)], kv_ref[ks, KVA + HEAD_DIM * n:KVA + HEAD_DIM * (n + 1)])
        _dsa_attend_tiles(qs_ref, key_tiles, thr, bias_tiles, kv_of_head, m_ref, acc_ref, tq)
        return carry

    lax.fori_loop(0, i // group + 1, attend_group, 0)
    _dsa_finish(acc_ref, gate_ref, o_ref, tq)


def dsa_prompt(qcat, kiw, ki, kv, gcat, bias, *, n_sel):
    b, t, _ = qcat.shape
    tq = tk = LANES
    group = DSA_TILES_PER_STEP
    nk = t // tk
    assert nk % group == 0
    kern = functools.partial(_dsa_prompt_kernel, tq=tq, tk=tk, group=group, n_sel=n_sel,
                             idx_bits=int(t).bit_length())
    return pl.pallas_call(
        kern,
        grid=(b, t // tq),
        in_specs=[
            pl.BlockSpec((None, tq, QA), lambda bi, i: (bi, i, 0)),
            pl.BlockSpec((None, tq, QI), lambda bi, i: (bi, i, 2 * QA // QI)),
            pl.BlockSpec((None, tq, LANES), lambda bi, i: (bi, i, 0)),
            _resident((None, t, D_IDX), lambda bi, i: (bi, 0, 0)),
            _resident((None, t, 2 * KVA), lambda bi, i: (bi, 0, 0)),
            pl.BlockSpec((None, tq, QA), lambda bi, i: (bi, i, 0)),
            _resident((N_BIAS_VARIANTS, H_A * tq, tk), lambda bi, i: (0, 0, 0)),
        ],
        out_specs=pl.BlockSpec((None, tq, QA), lambda bi, i: (bi, i, 0)),
        out_shape=jax.ShapeDtypeStruct((b, t, QA), BF16),
        scratch_shapes=_dsa_scratch(nk + group, tq, tk),
        compiler_params=_cparams("parallel", "arbitrary"),
        name="dsa_prompt",
    )(qcat, qcat, kiw, ki, kv, gcat, bias)


def _dsa_sample_kernel(q_ref, qi_ref, w_ref, kin_ref, kvn_ref, gate_ref, bias_ref, kic_ref, kvc_ref, o_ref,
                       qs_ref, qih_ref, wb_ref, key_ref, m_ref, acc_ref, thr_ref, *, tq, p0, pt, n_sel, idx_bits):
    kt = pl.program_id(1)
    tk = TK_SAMPLE
    n_cache = p0 // tk
    sub = pt // tk

    @pl.when(kt == 0)
    def _():
        new_visible = lax.broadcasted_iota(I32, (tq, tk), 1) < tq
        _dsa_prepare(q_ref, qi_ref, w_ref, qs_ref, qih_ref, wb_ref, tq)

        def score_tile(k, carry):
            ks = pl.ds(pl.multiple_of(k * tk, tk), tk)
            key_ref[k] = _dsa_scores_to_keys(qih_ref, wb_ref, kic_ref[ks, :].astype(BF16), True)
            return carry

        lax.fori_loop(0, n_cache, score_tile, 0)
        ki_new = _pad_new_tile(kin_ref[:, :D_IDX].astype(BF16), tk)
        key_ref[n_cache] = _dsa_scores_to_keys(qih_ref, wb_ref, ki_new, new_visible)
        thr = _dsa_threshold(key_ref, n_cache + 1, n_sel, idx_bits, tq, tk)
        thr_ref[...] = jnp.broadcast_to(thr, thr_ref.shape)
        m_ref[...] = jnp.full(m_ref.shape, NEG, F32)
        acc_ref[...] = jnp.zeros(acc_ref.shape, F32)

    thr = thr_ref[...]
    k0 = kt * sub
    key_tiles = [key_ref[k0 + j] for j in range(sub)]
    bias_tiles = [bias_ref[jnp.minimum(n_cache - k0 - j, N_BIAS_VARIANTS - 1)] for j in range(sub)]
    kv_cached = lambda n: (kvc_ref[:, 0, n, :].astype(BF16), kvc_ref[:, 1, n, :].astype(BF16))
    _dsa_attend_tiles(qs_ref, key_tiles, thr, bias_tiles, kv_cached, m_ref, acc_ref, tq)

    @pl.when(kt == pl.num_programs(1) - 1)
    def _():
        kv_new = lambda n: (_pad_new_tile(kvn_ref[:, _cols(n)], tk),
                            _pad_new_tile(kvn_ref[:, KVA + HEAD_DIM * n:KVA + HEAD_DIM * (n + 1)], tk))
        _dsa_attend_tiles(qs_ref, [key_ref[n_cache]], thr, [bias_ref[0]], kv_new, m_ref, acc_ref, tq)
        _dsa_finish(acc_ref, gate_ref, o_ref, tq)


def dsa_sample(qcat, kiw, kv_new, gcat, bias, cache_kidx, cache_kv, layer, *, n_sel):
    b, tq, _ = qcat.shape
    p0 = cache_kv.shape[2]
    tk, pt = TK_SAMPLE, PT_SAMPLE
    assert p0 % pt == 0 and tq <= tk
    nk = p0 // tk + 1
    kern = functools.partial(_dsa_sample_kernel, tq=tq, p0=p0, pt=pt, n_sel=n_sel,
                             idx_bits=int(p0 + tk).bit_length())
    return pl.pallas_call(
        kern,
        grid=(b, p0 // pt),
        in_specs=[
            pl.BlockSpec((None, tq, QA), lambda bi, kt: (bi, 0, 0)),
            pl.BlockSpec((None, tq, QI), lambda bi, kt: (bi, 0, 2 * QA // QI)),
            pl.BlockSpec((None, tq, LANES), lambda bi, kt: (bi, 0, 0)),
            pl.BlockSpec((None, tq, LANES), lambda bi, kt: (bi, 0, 0)),
            pl.BlockSpec((None, tq, 2 * KVA), lambda bi, kt: (bi, 0, 0)),
            pl.BlockSpec((None, tq, QA), lambda bi, kt: (bi, 0, 0)),
            pl.BlockSpec((N_BIAS_VARIANTS, H_A * tq, tk), lambda bi, kt: (0, 0, 0)),
            pl.BlockSpec((None, None, p0, D_IDX), lambda bi, kt: (layer, bi, 0, 0)),
            pl.BlockSpec((None, None, pt, 2, KV_A, HEAD_DIM), lambda bi, kt: (layer, bi, kt, 0, 0, 0)),
        ],
        out_specs=pl.BlockSpec((None, tq, QA), lambda bi, kt: (bi, 0, 0)),
        out_shape=jax.ShapeDtypeStruct((b, tq, QA), BF16),
        scratch_shapes=_dsa_scratch(nk, tq, tk) + [pltpu.VMEM((tq, tk), I32)],
        compiler_params=_cparams("parallel", "arbitrary"),
        name="dsa_sample",
    )(qcat, qcat, kiw, kiw, kv_new, gcat, bias, cache_kidx, cache_kv)


def _diff_prompt_kernel(q_ref, k_ref, v_ref, gate_ref, bias_ref, lam_ref, subln_ref, o_ref,
                        m_ref, acc_ref, *, tq, tk, hb, lam_init):
    i = pl.program_id(2)
    row_chunk = (i * tq + lax.broadcasted_iota(I32, (tq, tk), 0)) >> CHUNK_SHIFT
    lane = lax.broadcasted_iota(I32, (tq, tk), 1)
    m_ref[...] = jnp.full(m_ref.shape, NEG, F32)
    acc_ref[...] = jnp.zeros(acc_ref.shape, F32)

    def step(k, masked):
        ks = pl.ds(pl.multiple_of(k * tk, tk), tk)
        variant = jnp.minimum(i - k, N_BIAS_VARIANTS - 1)
        madd = jnp.where(((k * tk + lane) >> CHUNK_SHIFT) <= row_chunk, 0.0, NEG) if masked else None
        loaded = []
        for h in range(hb):
            bias = bias_ref[variant, h]
            if masked:
                bias = bias + madd
            vmat = v_ref[ks, _cols(h, DV_B)]
            for c in range(2):
                col = slice(DV_B * h + HEAD_DIM * c, DV_B * h + HEAD_DIM * (c + 1))
                loaded.append((h, c, q_ref[:, col], k_ref[ks, col], vmat, bias, m_ref[h, c], acc_ref[h, c]))
        score_fn = lambda q, kmat, bias: lambda: _qk(q, kmat) * QK_SCALE_BITS + bias
        new = _softmax_tiles([(score_fn(q, kmat, bias), vmat, m_prev, acc_prev)
                              for _, _, q, kmat, vmat, bias, m_prev, acc_prev in loaded],
                             mxu_rowsum=False, width=DIFF_CHAINS_ABREAST)
        for (h, c, *_), (m_new, acc_new) in zip(loaded, new):
            m_ref[h, c], acc_ref[h, c] = m_new, acc_new

    def full_tile(k, carry):
        step(k, False)
        return carry

    lax.fori_loop(0, i, full_tile, 0)
    step(i, True)
    for h in range(hb):
        cs = _cols(h, DV_B)
        o_ref[:, cs] = _diff_finish(acc_ref[h, 0], acc_ref[h, 1], lam_ref, subln_ref, gate_ref[:, cs],
                                    lam_init).astype(o_ref.dtype)


def diff_prompt(qcat, kv, gcat, bias, lam_vec, subln, *, tq, hb, lam_init):
    b, t, _ = qcat.shape
    tk = tq
    w = hb * DV_B
    kern = functools.partial(_diff_prompt_kernel, tq=tq, tk=tk, hb=hb, lam_init=lam_init)
    return pl.pallas_call(
        kern,
        grid=(b, H_B // hb, t // tq),
        in_specs=[
            pl.BlockSpec((None, tq, w), lambda bi, g, i: (bi, i, QA // w + g)),
            _resident((None, t, w), lambda bi, g, i: (bi, 0, g)),
            _resident((None, t, w), lambda bi, g, i: (bi, 0, QA // w + g)),
            pl.BlockSpec((None, tq, w), lambda bi, g, i: (bi, i, QA // w + g)),
            _resident((N_BIAS_VARIANTS, hb, tq, tk), lambda bi, g, i: (0, g, 0, 0)),
            pl.BlockSpec((4, HEAD_DIM), lambda bi, g, i: (0, 0)),
            pl.BlockSpec((1, DV_B), lambda bi, g, i: (0, 0)),
        ],
        out_specs=pl.BlockSpec((None, tq, w), lambda bi, g, i: (bi, i, g)),
        out_shape=jax.ShapeDtypeStruct((b, t, QA), BF16),
        scratch_shapes=[pltpu.VMEM((hb, 2, tq, LANES), F32), pltpu.VMEM((hb, 2, tq, DV_B + LANES), F32)],
        compiler_params=_cparams("parallel", "parallel", "arbitrary"),
        name="diff_prompt",
    )(qcat, kv, kv, gcat, bias, lam_vec, subln.reshape(1, DV_B))


def _cache_rows_2d(cache):
    n, b, p, two, h, d = cache.shape
    assert two == 2 and h % 8 == 0
    return cache.reshape(n, b, p * two * h, d)


def _head_rows(c_ref, key0, n_keys, n_heads, sel, h):
    period = 2 * n_heads
    return c_ref[pl.ds(key0 * period + sel * n_heads + h, n_keys, stride=period), :]


def _diff_sample_kernel(q_ref, kn_ref, vn_ref, gate_ref, bias_ref, lam_ref, subln_ref, c0_ref, c1_ref, o_ref,
                        m_ref, acc_ref, *, tq, p0, pt, lam_init):
    kt = pl.program_id(1)
    tk = TK_SAMPLE
    n_cache = p0 // tk
    sub = pt // tk

    @pl.when(kt == 0)
    def _():
        m_ref[...] = jnp.full(m_ref.shape, NEG, F32)
        acc_ref[...] = jnp.zeros(acc_ref.shape, F32)

    def update_all(tile_of_head):
        loaded = []
        for h in range(H_B):
            kmat2, vmat, bias = tile_of_head(h)
            for c in range(2):
                q = q_ref[:, DV_B * h + HEAD_DIM * c:DV_B * h + HEAD_DIM * (c + 1)]
                loaded.append((h, c, q, kmat2[:, _cols(c)], vmat, bias, m_ref[h, c], acc_ref[h, c]))
        score_fn = lambda q, kmat, bias: lambda: _qk(q, kmat) * QK_SCALE_BITS + bias
        new = _softmax_tiles([(score_fn(q, kmat, bias), vmat, m_prev, acc_prev)
                              for _, _, q, kmat, vmat, bias, m_prev, acc_prev in loaded], mxu_rowsum=False)
        for (h, c, *_), (m_new, acc_new) in zip(loaded, new):
            m_ref[h, c], acc_ref[h, c] = m_new, acc_new

    def cache_tile(j, carry):
        variant = jnp.minimum(n_cache - (kt * sub + j), N_BIAS_VARIANTS - 1)
        halves = lambda sel, h: jnp.concatenate(
            [_head_rows(c_ref, j * tk, tk, H_B, sel, h).astype(BF16) for c_ref in (c0_ref, c1_ref)], axis=1)
        update_all(lambda h: (halves(0, h), halves(1, h), bias_ref[variant, h]))
        return carry

    lax.fori_loop(0, sub, cache_tile, 0)

    @pl.when(kt == pl.num_programs(1) - 1)
    def _():
        new_madd = jnp.where(lax.broadcasted_iota(I32, (tq, tk), 1) < tq, 0.0, NEG)
        update_all(lambda h: (_pad_new_tile(kn_ref[:, _cols(h, DV_B)], tk), _pad_new_tile(vn_ref[:, _cols(h, DV_B)], tk),
                              bias_ref[0, h] + new_madd))
        for h in range(H_B):
            cs = _cols(h, DV_B)
            o_ref[:, cs] = _diff_finish(acc_ref[h, 0], acc_ref[h, 1], lam_ref, subln_ref, gate_ref[:, cs],
                                        lam_init).astype(o_ref.dtype)


def diff_sample(qcat, kv_new, gcat, bias, lam_vec, subln, cache_kv, layer, *, lam_init):
    b, tq, _ = qcat.shape
    p0 = cache_kv.shape[2]
    tk, pt = TK_SAMPLE, PT_SAMPLE
    assert p0 % pt == 0 and tq <= tk
    kern = functools.partial(_diff_sample_kernel, tq=tq, p0=p0, pt=pt, lam_init=lam_init)
    cache_spec = lambda c: pl.BlockSpec((None, None, pt * 2 * H_B, LANES), lambda bi, kt: (layer, bi, kt, c))
    cache_2d = _cache_rows_2d(cache_kv)
    return pl.pallas_call(
        kern,
        grid=(b, p0 // pt),
        in_specs=[
            pl.BlockSpec((None, tq, QA), lambda bi, kt: (bi, 0, 1)),
            pl.BlockSpec((None, tq, QA), lambda bi, kt: (bi, 0, 0)),
            pl.BlockSpec((None, tq, QA), lambda bi, kt: (bi, 0, 1)),
            pl.BlockSpec((None, tq, QA), lambda bi, kt: (bi, 0, 1)),
            pl.BlockSpec((N_BIAS_VARIANTS, H_B, tq, tk), lambda bi, kt: (0, 0, 0, 0)),
            pl.BlockSpec((4, HEAD_DIM), lambda bi, kt: (0, 0)),
            pl.BlockSpec((1, DV_B), lambda bi, kt: (0, 0)),
            cache_spec(0), cache_spec(1),
        ],
        out_specs=pl.BlockSpec((None, tq, QA), lambda bi, kt: (bi, 0, 0)),
        out_shape=jax.ShapeDtypeStruct((b, tq, QA), BF16),
        scratch_shapes=[pltpu.VMEM((H_B, 2, tq, LANES), F32), pltpu.VMEM((H_B, 2, tq, DV_B + LANES), F32)],
        compiler_params=_cparams("parallel", "arbitrary"),
        name="diff_sample",
    )(qcat, kv_new, kv_new, gcat, bias, lam_vec, subln.reshape(1, DV_B), cache_2d, cache_2d)


def _stick_prompt_kernel(q_ref, k_ref, v_ref, gate_ref, tri_ref, o_ref, acc_ref, carry_ref, *, tq, tk, hb):
    i = pl.program_id(2)
    before = lax.broadcasted_iota(I32, (tq, tk), 1) < lax.broadcasted_iota(I32, (tq, tk), 0)
    acc_ref[...] = jnp.zeros(acc_ref.shape, F32)
    carry_ref[...] = jnp.zeros(carry_ref.shape, F32)

    def step(k, mask):
        ks = pl.ds(pl.multiple_of(k * tk, tk), tk)
        tri = tri_ref[:tk, :] if mask is None else tri_ref[...]
        loaded = [(q_ref[:, _cols(h)], k_ref[ks, _cols(h)], v_ref[ks, _cols(h)], carry_ref[h], acc_ref[h])
                  for h in range(hb)]
        new = _stick_tiles([(q, kmat, vmat, carry_prev) for q, kmat, vmat, carry_prev, _ in loaded], tri, mask)
        for h, ((pv, carry), (_, _, _, _, acc)) in enumerate(zip(new, loaded)):
            acc_ref[h], carry_ref[h] = acc + pv, carry

    step(i, before)

    def full_tile(j, carry):
        step(i - 1 - j, None)
        return carry

    lax.fori_loop(0, i, full_tile, 0)
    for h in range(hb):
        cs = _cols(h)
        o_ref[:, cs] = (acc_ref[h] * _silu(gate_ref[:, cs])).astype(o_ref.dtype)


def stick_prompt(qcat, kv, gcat, tri_ext, *, tq, hb):
    b, t, _ = qcat.shape
    tk = tq
    w = hb * HEAD_DIM
    kern = functools.partial(_stick_prompt_kernel, tq=tq, tk=tk, hb=hb)
    return pl.pallas_call(
        kern,
        grid=(b, H_C // hb, t // tq),
        in_specs=[
            pl.BlockSpec((None, tq, w), lambda bi, g, i: (bi, i, g)),
            _resident((None, t, w), lambda bi, g, i: (bi, 0, g)),
            _resident((None, t, w), lambda bi, g, i: (bi, 0, QA // w + g)),
            pl.BlockSpec((None, tq, w), lambda bi, g, i: (bi, i, g)),
            _resident((2 * tk, tk + LANES), lambda bi, g, i: (0, 0)),
        ],
        out_specs=pl.BlockSpec((None, tq, w), lambda bi, g, i: (bi, i, g)),
        out_shape=jax.ShapeDtypeStruct((b, t, QA), BF16),
        scratch_shapes=[pltpu.VMEM((hb, tq, HEAD_DIM), F32), pltpu.VMEM((hb, tq, LANES), F32)],
        compiler_params=_cparams("parallel", "parallel", "arbitrary"),
        name="stick_prompt",
    )(qcat, kv, kv, gcat, tri_ext)


def _stick_sample_kernel(q_ref, kn_ref, vn_ref, gate_ref, tri_ref, c_ref, o_ref, acc_ref, carry_ref, *, tq, pt):
    kt = pl.program_id(1)
    tk = TK_SAMPLE
    sub = pt // tk
    head_groups = [range(g * HB_SAMPLE, (g + 1) * HB_SAMPLE) for g in range(H_C // HB_SAMPLE)]

    @pl.when(kt == 0)
    def _():
        before = lax.broadcasted_iota(I32, (tq, tk), 1) < lax.broadcasted_iota(I32, (tq, tk), 0)
        tri = tri_ref[...]
        for heads in head_groups:
            new = _stick_tiles([(q_ref[:, _cols(h)], _pad_new_tile(kn_ref[:, _cols(h)], tk),
                                 _pad_new_tile(vn_ref[:, _cols(h)], tk), jnp.zeros((tq, LANES), F32))
                                for h in heads], tri, before)
            for h, (pv, carry) in zip(heads, new):
                acc_ref[h], carry_ref[h] = pv, carry

    def cache_tile(j, carry):
        key0 = (sub - 1 - j) * tk
        tri = tri_ref[:tk, :]
        for heads in head_groups:
            loaded = [(q_ref[:, _cols(h)], _head_rows(c_ref, key0, tk, H_C, 0, h).astype(BF16),
                       _head_rows(c_ref, key0, tk, H_C, 1, h).astype(BF16), carry_ref[h], acc_ref[h]) for h in heads]
            new = _stick_tiles([(q, kmat, vmat, c_prev) for q, kmat, vmat, c_prev, _ in loaded], tri, None)
            for h, (pv, c), (_, _, _, _, acc) in zip(heads, new, loaded):
                acc_ref[h], carry_ref[h] = acc + pv, c
        return carry

    lax.fori_loop(0, sub, cache_tile, 0)

    @pl.when(kt == pl.num_programs(1) - 1)
    def _():
        for h in range(H_C):
            cs = _cols(h)
            o_ref[:, cs] = (acc_ref[h] * _silu(gate_ref[:, cs])).astype(o_ref.dtype)


def stick_sample(qcat, kv_new, gcat, tri_ext, cache_kv, layer):
    b, tq, _ = qcat.shape
    p0 = cache_kv.shape[2]
    pt = PT_SAMPLE
    assert p0 % pt == 0 and tq <= TK_SAMPLE
    n_kt = p0 // pt
    kern = functools.partial(_stick_sample_kernel, tq=tq, pt=pt)
    return pl.pallas_call(
        kern,
        grid=(b, n_kt),
        in_specs=[
            pl.BlockSpec((None, tq, QA), lambda bi, kt: (bi, 0, 0)),
            pl.BlockSpec((None, tq, QA), lambda bi, kt: (bi, 0, 0)),
            pl.BlockSpec((None, tq, QA), lambda bi, kt: (bi, 0, 1)),
            pl.BlockSpec((None, tq, QA), lambda bi, kt: (bi, 0, 0)),
            pl.BlockSpec((2 * TK_SAMPLE, TK_SAMPLE + LANES), lambda bi, kt: (0, 0)),
            pl.BlockSpec((None, None, pt * 2 * H_C, HEAD_DIM), lambda bi, kt: (layer, bi, n_kt - 1 - kt, 0)),
        ],
        out_specs=pl.BlockSpec((None, tq, QA), lambda bi, kt: (bi, 0, 0)),
        out_shape=jax.ShapeDtypeStruct((b, tq, QA), BF16),
        scratch_shapes=[pltpu.VMEM((H_C, tq, HEAD_DIM), F32), pltpu.VMEM((H_C, tq, LANES), F32)],
        compiler_params=_cparams("parallel", "arbitrary"),
        name="stick_sample",
    )(qcat, kv_new, kv_new, gcat, tri_ext, _cache_rows_2d(cache_kv))


def _band_prompt_kernel(q_ref, k_ref, v_ref, gate_ref, bias_ref, o_ref, *, tq, win, hb):
    i = pl.program_id(2)
    span = win + tq
    rows = pl.ds(pl.multiple_of(i * tq, tq), span)
    row_chunk = (i * tq + lax.broadcasted_iota(I32, (tq, span), 0)) >> CHUNK_SHIFT
    key_chunk = (i * tq - win + lax.broadcasted_iota(I32, (tq, span), 1)) >> CHUNK_SHIFT
    visible = (key_chunk <= row_chunk) & (key_chunk >= jnp.maximum(row_chunk - BAND_CHUNKS, 0))
    madd = jnp.where(visible, 0.0, NEG)
    ones = jnp.ones((span, LANES), BF16)
    for h in range(hb):
        cs = _cols(h)
        s = _qk(q_ref[:, cs], k_ref[rows, cs]) * QK_SCALE_BITS + bias_ref[h] + madd
        p = jnp.exp2(s - jnp.max(s, axis=1, keepdims=True))
        pv = jnp.dot(p.astype(BF16), jnp.concatenate([v_ref[rows, cs], ones], axis=1), preferred_element_type=F32)
        o_ref[:, cs] = (_softmax_finish(pv) * _silu(gate_ref[:, cs])).astype(o_ref.dtype)


def band_prompt(qcat, kv, gcat, bias, *, hb):
    b, t, _ = qcat.shape
    tq = LANES
    win = BAND_CHUNKS * CHUNK
    w = hb * HEAD_DIM
    assert kv.shape[1] == win + t and bias.shape == (H_D, tq, win + tq)
    kern = functools.partial(_band_prompt_kernel, tq=tq, win=win, hb=hb)
    return pl.pallas_call(
        kern,
        grid=(b, H_D // hb, t // tq),
        in_specs=[
            pl.BlockSpec((None, tq, w), lambda bi, g, i: (bi, i, QA // w + g)),
            _resident((None, win + t, w), lambda bi, g, i: (bi, 0, g)),
            _resident((None, win + t, w), lambda bi, g, i: (bi, 0, QA // w + g)),
            pl.BlockSpec((None, tq, w), lambda bi, g, i: (bi, i, QA // w + g)),
            _resident((hb, tq, win + tq), lambda bi, g, i: (g, 0, 0)),
        ],
        out_specs=pl.BlockSpec((None, tq, w), lambda bi, g, i: (bi, i, g)),
        out_shape=jax.ShapeDtypeStruct((b, t, QA), BF16),
        compiler_params=_cparams("parallel", "parallel", "arbitrary"),
        name="band_prompt",
    )(qcat, kv, kv, gcat, bias)


def _band_sample_kernel(q_ref, kn_ref, vn_ref, gate_ref, bias_ref, c_ref, o_ref, *, tq, win):
    g = pl.program_id(1)
    tk = TK_SAMPLE
    lane = lax.broadcasted_iota(I32, (tq, win + tk), 1)
    madd = jnp.where(lane < win + tq, 0.0, NEG)
    ones = jnp.ones((win + tk, LANES), BF16)
    for h in range(HB_SAMPLE):
        cs = _cols(h)
        head = g * HB_SAMPLE + h
        kmat = jnp.concatenate([_head_rows(c_ref, 0, win, H_D, 0, head).astype(BF16),
                                _pad_new_tile(kn_ref[:, cs], tk)], axis=0)
        vmat = jnp.concatenate([_head_rows(c_ref, 0, win, H_D, 1, head).astype(BF16),
                                _pad_new_tile(vn_ref[:, cs], tk)], axis=0)
        s = _qk(q_ref[:, cs], kmat) * QK_SCALE_BITS + bias_ref[h] + madd
        p = jnp.exp2(s - jnp.max(s, axis=1, keepdims=True))
        pv = jnp.dot(p.astype(BF16), jnp.concatenate([vmat, ones], axis=1), preferred_element_type=F32)
        o_ref[:, cs] = (_softmax_finish(pv) * _silu(gate_ref[:, cs])).astype(o_ref.dtype)


def band_sample(qcat, kv_new, gcat, bias, cache_kv, layer):
    b, tq, _ = qcat.shape
    win = cache_kv.shape[2]
    w = HB_SAMPLE * HEAD_DIM
    assert win == BAND_CHUNKS * CHUNK and tq == CHUNK and bias.shape == (H_D, tq, win + TK_SAMPLE)
    kern = functools.partial(_band_sample_kernel, tq=tq, win=win)
    return pl.pallas_call(
        kern,
        grid=(b, H_D // HB_SAMPLE),
        in_specs=[
            pl.BlockSpec((None, tq, w), lambda bi, g: (bi, 0, QA // w + g)),
            pl.BlockSpec((None, tq, w), lambda bi, g: (bi, 0, g)),
            pl.BlockSpec((None, tq, w), lambda bi, g: (bi, 0, QA // w + g)),
            pl.BlockSpec((None, tq, w), lambda bi, g: (bi, 0, QA // w + g)),
            pl.BlockSpec((HB_SAMPLE, tq, win + TK_SAMPLE), lambda bi, g: (g, 0, 0)),
            pl.BlockSpec((None, None, win * 2 * H_D, HEAD_DIM), lambda bi, g: (layer, bi, 0, 0)),
        ],
        out_specs=pl.BlockSpec((None, tq, w), lambda bi, g: (bi, 0, g)),
        out_shape=jax.ShapeDtypeStruct((b, tq, QA), BF16),
        compiler_params=_cparams("parallel", "arbitrary"),
        name="band_sample",
    )(qcat, kv_new, kv_new, gcat, bias, _cache_rows_2d(cache_kv))


def _t5_bucket_np(rel):
    half = T5_BUCKETS // 2
    max_exact = half // 2
    n = np.abs(rel)
    nf = np.maximum(n, 1).astype(np.float64)
    large = max_exact + (np.log(nf / max_exact) / math.log(T5_MAX_DIST / max_exact)
                         * (half - max_exact)).astype(np.int32)
    large = np.minimum(large, half - 1)
    return np.where(rel < 0, half, 0) + np.where(n < max_exact, n, large)


def _band_index_np(rel):
    return np.clip(rel, -REL_CLIP, REL_CLIP) + REL_CLIP


def toeplitz_bias(table, index_of_rel, r0, n_rows, n_cols):
    rel = r0 + np.arange(n_rows)[:, None] - np.arange(n_cols)[None, :]
    idx = jnp.asarray(index_of_rel(rel).reshape(-1).astype(np.int32))
    onehot = (idx[None, :] == jnp.arange(table.shape[0], dtype=I32)[:, None]).astype(F32)
    out = jnp.einsum("bh,bn->hn", table, onehot, precision=lax.Precision.HIGHEST)
    return out.reshape(table.shape[1], n_rows, n_cols)


def bias_tiles(table, index_of_rel, tq, tk):
    far = (N_BIAS_VARIANTS - 1) * tk
    assert (index_of_rel(np.arange(far - tk + 1, far + tq)) == index_of_rel(np.array(far + tq))).all()
    return jnp.stack([toeplitz_bias(table, index_of_rel, v * tk, tq, tk) for v in range(N_BIAS_VARIANTS)], axis=0)


def band_window_bias(rel_tab, tq, win):
    far_cols = win - REL_CLIP
    return jnp.concatenate(
        [jnp.broadcast_to(rel_tab[2 * REL_CLIP][:, None, None], (H_D, tq, far_cols)),
         toeplitz_bias(rel_tab, _band_index_np, REL_CLIP, tq, win + LANES - far_cols)], axis=-1)


def _tri_ext(tk):
    tri = -np.tril(np.ones((tk, tk), np.float32), -1)
    ext = np.concatenate([tri, -np.ones((tk, LANES), np.float32)], axis=1)
    return jnp.asarray(np.concatenate([ext, ext], axis=0), BF16)


def _even_weights(w_in):
    c = np.cumsum((0, QA, KVA, KVA, QI, D_IDX, H_IDX, QA, QA, QA, QA, QA)).tolist()
    aq, ak, av, aqi, aki, aw, ag, bq, bk, bv, bg = (w_in[:, c[r]:c[r + 1]] for r in range(11))
    pad = jnp.zeros((w_in.shape[0], LANES - D_IDX - H_IDX), w_in.dtype)
    cat = lambda *xs: jnp.concatenate(xs, axis=1).astype(BF16)
    return cat(aq, bq, aqi), cat(ak, av), cat(bk, bv), cat(aki, aw, pad), cat(ag, bg)


def _odd_weights(w_in):
    cq, ck, cv, cg, dq, dk, dv, dg = (w_in[:, QA * r:QA * (r + 1)] for r in range(8))
    cat = lambda *xs: jnp.concatenate(xs, axis=1).astype(BF16)
    return cat(cq, dq), cat(ck, cv), cat(dk, dv), cat(cg, dg)


def _even_layer(h, caches, layer, weights, t5_tab, lam_vec, subln, lam_init):
    w_q, w_akv, w_bkv, w_kiw, w_g = weights
    b, t, d = h.shape
    h2 = h.reshape(b * t, d)
    (qcat,) = matmul(h2, w_q, (BF16,))
    akv, akv_b = matmul(h2, w_akv, (F32, BF16))
    bkv, bkv_b = matmul(h2, w_bkv, (F32, BF16))
    (kiw,) = matmul(h2, w_kiw, (F32,))
    (gcat,) = matmul(h2, w_g, (F32,))
    qcat, akv_b, bkv_b, kiw, gcat = (x.reshape(b, t, -1) for x in (qcat, akv_b, bkv_b, kiw, gcat))
    aki = kiw[..., :D_IDX]
    t5_a, t5_b = t5_tab[:, :H_A] * LOG2E, t5_tab[:, H_A:] * LOG2E
    if caches is None:
        tq_b = 256
        bias_a = bias_tiles(t5_a, _t5_bucket_np, LANES, LANES).reshape(N_BIAS_VARIANTS, H_A * LANES, LANES)
        o_a = dsa_prompt(qcat, kiw, aki.astype(BF16), akv_b, gcat, bias_a, n_sel=min(TOPK_MAX, t // 4))
        o_b = diff_prompt(qcat, bkv_b, gcat, bias_tiles(t5_b, _t5_bucket_np, tq_b, tq_b), lam_vec, subln,
                          tq=tq_b, hb=4, lam_init=lam_init)
    else:
        cache_a, cache_ki, cache_b = caches
        p0 = cache_a.shape[2]
        bias_a = bias_tiles(t5_a, _t5_bucket_np, t, TK_SAMPLE).reshape(N_BIAS_VARIANTS, H_A * t, TK_SAMPLE)
        o_a = dsa_sample(qcat, kiw, akv_b, gcat, bias_a, cache_ki, cache_a, layer,
                         n_sel=min(TOPK_MAX, (p0 + t) // 4))
        o_b = diff_sample(qcat, bkv_b, gcat, bias_tiles(t5_b, _t5_bucket_np, t, TK_SAMPLE), lam_vec, subln,
                          cache_b, layer, lam_init=lam_init)
    mixed = jnp.concatenate([o_a, o_b], axis=-1).reshape(b * t, 2 * QA)
    return mixed, akv.reshape(b, t, -1), aki, bkv.reshape(b, t, -1)


def _odd_layer(h, caches, layer, weights, rel_tab):
    w_q, w_ckv, w_dkv, w_g = weights
    b, t, d = h.shape
    h2 = h.reshape(b * t, d)
    (qcat,) = matmul(h2, w_q, (BF16,))
    ckv, ckv_b = matmul(h2, w_ckv, (F32, BF16))
    dkv, dkv_b = matmul(h2, w_dkv, (F32, BF16))
    (gcat,) = matmul(h2, w_g, (F32,))
    qcat, ckv_b, dkv_b, gcat = (x.reshape(b, t, -1) for x in (qcat, ckv_b, dkv_b, gcat))
    win = BAND_CHUNKS * CHUNK
    rel_bits = rel_tab * LOG2E
    if caches is None:
        tq_c = 256
        o_c = stick_prompt(qcat, ckv_b, gcat, _tri_ext(tq_c), tq=tq_c, hb=4)
        o_d = band_prompt(qcat, jnp.pad(dkv_b, ((0, 0), (win, 0), (0, 0))), gcat,
                          band_window_bias(rel_bits, LANES, win), hb=4)
    else:
        cache_c, cache_d = caches
        o_c = stick_sample(qcat, ckv_b, gcat, _tri_ext(TK_SAMPLE), cache_c, layer)
        o_d = band_sample(qcat, dkv_b, gcat, band_window_bias(rel_bits, t, win), cache_d, layer)
    mixed = jnp.concatenate([o_c, o_d], axis=-1).reshape(b * t, 2 * QA)
    return mixed, ckv.reshape(b, t, -1), dkv.reshape(b, t, -1)


def _finish_layer(x, mixed, w_out, p_i, g_post, w_proj, g_pl, w_gate):
    (mix,) = matmul(mixed, w_out, (F32,))
    x1, x1_b = residual_norm(x, mix, g_post)
    e = embed_norm(p_i.astype(BF16), w_proj, g_pl)
    return gate_matmul(x1_b, w_gate, x1, e)


def kernel(x_prompt, x_sample, p_prompt, p_sample, cache_a_kv, cache_a_kidx, cache_b_kv, cache_c_kv,
           cache_d_kv, norm_pre, norm_post, w_in_even, w_out_even, t5_bias, diff_lambda, diff_subln,
           w_in_odd, w_out_odd, d_rel_bias, w_pl_proj, pl_norm, w_pl_gate):
    bp, tp, d = x_prompt.shape
    bs, ts, _ = x_sample.shape
    depth = norm_pre.shape[0]
    assert d == D_MODEL and ts == CHUNK and tp % 256 == 0
    xp = x_prompt.reshape(bp * tp, d)
    xs = x_sample.reshape(bs * ts, d)
    outs = {name: [] for name in ("a_kv_p", "a_kv_s", "a_ki_p", "a_ki_s", "b_kv_p", "b_kv_s",
                                  "c_kv_p", "c_kv_s", "d_kv_p", "d_kv_s")}
    for i in range(depth):
        j = i // 2
        hp = rmsnorm_cast(xp, norm_pre[i]).reshape(bp, tp, d)
        hs = rmsnorm_cast(xs, norm_pre[i]).reshape(bs, ts, d)
        if i % 2 == 0:
            lam_init = 0.8 - 0.6 * math.exp(-0.3 * i)
            weights = _even_weights(w_in_even[j])
            w_out = w_out_even[j].astype(BF16)
            prm = (weights, t5_bias, diff_lambda[j], diff_subln[j], lam_init)
            mp, akv, aki, bkv = _even_layer(hp, None, j, *prm)
            outs["a_kv_p"].append(akv.reshape(bp, tp, 2, KV_A, HEAD_DIM))
            outs["a_ki_p"].append(aki)
            outs["b_kv_p"].append(bkv.reshape(bp, tp, 2, H_B, DV_B))
            ms, akv, aki, bkv = _even_layer(hs, (cache_a_kv, cache_a_kidx, cache_b_kv), j, *prm)
            outs["a_kv_s"].append(akv.reshape(bs, ts, 2, KV_A, HEAD_DIM))
            outs["a_ki_s"].append(aki)
            outs["b_kv_s"].append(bkv.reshape(bs, ts, 2, H_B, DV_B))
        else:
            weights = _odd_weights(w_in_odd[j])
            w_out = w_out_odd[j].astype(BF16)
            mp, ckv, dkv = _odd_layer(hp, None, j, weights, d_rel_bias[j])
            win_p = min(BAND_CHUNKS * CHUNK, tp)
            outs["c_kv_p"].append(ckv.reshape(bp, tp, 2, H_C, HEAD_DIM))
            outs["d_kv_p"].append(dkv[:, tp - win_p:].reshape(bp, win_p, 2, H_D, HEAD_DIM))
            ms, ckv, dkv = _odd_layer(hs, (cache_c_kv, cache_d_kv), j, weights, d_rel_bias[j])
            outs["c_kv_s"].append(ckv.reshape(bs, ts, 2, H_C, HEAD_DIM))
            full_d = jnp.concatenate([cache_d_kv[j], dkv.reshape(bs, ts, 2, H_D, HEAD_DIM)], axis=1)
            outs["d_kv_s"].append(full_d[:, ts:])
        fin = (norm_post[i], w_pl_proj[i].astype(BF16), pl_norm[i], w_pl_gate[i].astype(BF16))
        xp = _finish_layer(xp, mp, w_out, p_prompt[i].reshape(bp * tp, -1), *fin)
        xs = _finish_layer(xs, ms, w_out, p_sample[i].reshape(bs * ts, -1), *fin)
    st = lambda name: jnp.stack(outs[name])
    return (xp.reshape(bp, tp, d), xs.reshape(bs, ts, d), st("a_kv_p"), st("a_kv_s"), st("a_ki_p"), st("a_ki_s"),
            st("b_kv_p"), st("b_kv_s"), st("c_kv_p"), st("c_kv_s"), st("d_kv_p"), st("d_kv_s"))
```

```python
import functools
import math

import numpy as np
import jax
import jax.numpy as jnp
from jax import lax
from jax.experimental import pallas as pl
from jax.experimental.pallas import tpu as pltpu

F32 = jnp.float32
BF16 = jnp.bfloat16
I32 = jnp.int32

D_MODEL = 4096
CHUNK = 64
CHUNK_SHIFT = 6
HEAD_DIM = 128
H_A = 16
KV_A = 4
GROUP_A = H_A // KV_A
H_IDX = 16
D_IDX = 64
TOPK_MAX = 256
H_B = 8
H_C = 16
H_D = 16
BAND_CHUNKS = 8
REL_CLIP = 128
T5_BUCKETS = 32
T5_MAX_DIST = 128
RMS_EPS = 1e-6
QA = H_A * HEAD_DIM
KVA = KV_A * HEAD_DIM
QI = H_IDX * D_IDX
DV_B = 2 * HEAD_DIM
ATT_SCALE = HEAD_DIM ** -0.5
IDX_SCALE = (D_IDX ** -0.5) * (H_IDX ** -0.5)
LOG2E = math.log2(math.e)
QK_SCALE_BITS = ATT_SCALE * LOG2E

LANES = 128
NEG = -1e30
INT_MIN = -(2 ** 31)
VMEM_LIMIT = 56 * 1024 * 1024
N_BIAS_VARIANTS = 3
TK_SAMPLE = LANES
PT_SAMPLE = 512
HB_SAMPLE = 8
DSA_TILES_PER_STEP = 4
DSA_CHAINS_ABREAST = 4
DIFF_CHAINS_ABREAST = 4


def _cparams(*sem):
    return pltpu.CompilerParams(dimension_semantics=sem, vmem_limit_bytes=VMEM_LIMIT)


def _sigmoid(x):
    return 1.0 / (1.0 + jnp.exp(-x))


def _silu(x):
    return x * _sigmoid(x)


def _lane_tile(x, n):
    return x if n == 1 else jnp.concatenate([x] * n, axis=1)


def _cols(h, width=HEAD_DIM):
    return slice(width * h, width * (h + 1))


def _rmsnorm_cast_kernel(x_ref, g_ref, o_ref):
    x = x_ref[...]
    ms = jnp.mean(x * x, axis=-1, keepdims=True)
    o_ref[...] = (x * lax.rsqrt(ms + RMS_EPS) * g_ref[...]).astype(o_ref.dtype)


def rmsnorm_cast(x, g, bm=256):
    m, d = x.shape
    bm = min(bm, m)
    return pl.pallas_call(
        _rmsnorm_cast_kernel,
        grid=(m // bm,),
        in_specs=[pl.BlockSpec((bm, d), lambda i: (i, 0)), pl.BlockSpec((1, d), lambda i: (0, 0))],
        out_specs=pl.BlockSpec((bm, d), lambda i: (i, 0)),
        out_shape=jax.ShapeDtypeStruct((m, d), BF16),
        compiler_params=_cparams("parallel"),
        name="rmsnorm_cast",
    )(x, g.reshape(1, d))


def _residual_norm_kernel(x_ref, mix_ref, g_ref, o_ref, ob_ref):
    mix = mix_ref[...]
    ms = jnp.mean(mix * mix, axis=-1, keepdims=True)
    x1 = x_ref[...] + mix * lax.rsqrt(ms + RMS_EPS) * g_ref[...]
    o_ref[...] = x1
    ob_ref[...] = x1.astype(BF16)


def residual_norm(x, mix, g, bm=256):
    m, d = x.shape
    bm = min(bm, m)
    row = pl.BlockSpec((bm, d), lambda i: (i, 0))
    return pl.pallas_call(
        _residual_norm_kernel,
        grid=(m // bm,),
        in_specs=[row, row, pl.BlockSpec((1, d), lambda i: (0, 0))],
        out_specs=[row, row],
        out_shape=[jax.ShapeDtypeStruct((m, d), F32), jax.ShapeDtypeStruct((m, d), BF16)],
        compiler_params=_cparams("parallel"),
        name="residual_norm",
    )(x, mix, g.reshape(1, d))


def _embed_norm_kernel(p_ref, w_ref, g_ref, o_ref):
    y = jnp.dot(p_ref[...], w_ref[...], preferred_element_type=F32)
    ms = jnp.mean(y * y, axis=-1, keepdims=True)
    o_ref[...] = y * lax.rsqrt(ms + RMS_EPS) * g_ref[...]


def embed_norm(p, w, g, bm=256):
    m, k = p.shape
    d = w.shape[1]
    bm = min(bm, m)
    return pl.pallas_call(
        _embed_norm_kernel,
        grid=(m // bm,),
        in_specs=[pl.BlockSpec((bm, k), lambda i: (i, 0)), pl.BlockSpec((k, d), lambda i: (0, 0)),
                  pl.BlockSpec((1, d), lambda i: (0, 0))],
        out_specs=pl.BlockSpec((bm, d), lambda i: (i, 0)),
        out_shape=jax.ShapeDtypeStruct((m, d), F32),
        compiler_params=_cparams("parallel"),
        name="embed_norm",
    )(p, w, g.reshape(1, d))


def _mm_kernel(a_ref, w_ref, *o_refs):
    acc = jnp.dot(a_ref[...], w_ref[...], preferred_element_type=F32)
    for o_ref in o_refs:
        o_ref[...] = acc.astype(o_ref.dtype)


def matmul(a, w, out_dtypes, bm=1024, bn=1024):
    m, k = a.shape
    n = w.shape[1]
    bm, bn = min(bm, m), min(bn, n)
    o_spec = pl.BlockSpec((bm, bn), lambda i, j: (i, j))
    outs = pl.pallas_call(
        _mm_kernel,
        grid=(m // bm, n // bn),
        in_specs=[pl.BlockSpec((bm, k), lambda i, j: (i, 0)), pl.BlockSpec((k, bn), lambda i, j: (0, j))],
        out_specs=[o_spec] * len(out_dtypes),
        out_shape=[jax.ShapeDtypeStruct((m, n), dt) for dt in out_dtypes],
        compiler_params=_cparams("parallel", "parallel"),
        name="matmul",
    )(a, w)
    return outs


def _mm_pair_kernel(a1_ref, a2_ref, w1_ref, w2_ref, o_ref):
    o_ref[...] = (jnp.dot(a1_ref[...], w1_ref[...], preferred_element_type=F32)
                  + jnp.dot(a2_ref[...], w2_ref[...], preferred_element_type=F32))


def matmul_pair(a1, a2, w, bm=1024, bn=1024):
    m, k = a1.shape
    n = w.shape[1]
    assert a2.shape == (m, k) and w.shape[0] == 2 * k
    bm, bn = min(bm, m), min(bn, n)
    return pl.pallas_call(
        _mm_pair_kernel,
        grid=(m // bm, n // bn),
        in_specs=[pl.BlockSpec((bm, k), lambda i, j: (i, 0)), pl.BlockSpec((bm, k), lambda i, j: (i, 0)),
                  pl.BlockSpec((k, bn), lambda i, j: (0, j)), pl.BlockSpec((k, bn), lambda i, j: (1, j))],
        out_specs=pl.BlockSpec((bm, bn), lambda i, j: (i, j)),
        out_shape=jax.ShapeDtypeStruct((m, n), F32),
        compiler_params=_cparams("parallel", "parallel"),
        name="matmul_pair",
    )(a1, a2, w, w)


def _gate_mm_kernel(a_ref, w_ref, x_ref, e_ref, o_ref):
    logits = jnp.dot(a_ref[...], w_ref[...], preferred_element_type=F32)
    o_ref[...] = x_ref[...] + e_ref[...] * _sigmoid(logits)


def gate_matmul(a, w, x, e, bm=1024, bn=512):
    m, k = a.shape
    n = w.shape[1]
    bm, bn = min(bm, m), min(bn, n)
    tile = pl.BlockSpec((bm, bn), lambda i, j: (i, j))
    return pl.pallas_call(
        _gate_mm_kernel,
        grid=(m // bm, n // bn),
        in_specs=[pl.BlockSpec((bm, k), lambda i, j: (i, 0)), pl.BlockSpec((k, bn), lambda i, j: (0, j)),
                  tile, tile],
        out_specs=tile,
        out_shape=jax.ShapeDtypeStruct((m, n), F32),
        compiler_params=_cparams("parallel", "parallel"),
        name="gate_matmul",
    )(a, w, x, e)


def _resident(block_shape, index_map):
    return pl.BlockSpec(block_shape, index_map, pipeline_mode=pl.Buffered(1))


def _qk(q, kmat):
    return lax.dot_general(q, kmat, (((1,), (1,)), ((), ())), preferred_element_type=F32)


def _softmax_tiles(chains, mxu_rowsum=True, width=None):
    width = width or len(chains)
    if width < len(chains):
        return [r for i in range(0, len(chains), width) for r in _softmax_tiles(chains[i:i + width], mxu_rowsum)]
    chains = [(fn(), vmat, m_prev, acc_prev) for fn, vmat, m_prev, acc_prev in chains]
    tk = chains[0][0].shape[1]
    m_new = [jnp.maximum(m_prev, jnp.max(s, axis=1, keepdims=True)) for s, _, m_prev, _ in chains]
    alpha = [jnp.exp2(m_prev - m) for (_, _, m_prev, _), m in zip(chains, m_new)]
    p = [jnp.exp2(s - _lane_tile(m, tk // LANES)) for (s, _, _, _), m in zip(chains, m_new)]
    if mxu_rowsum:
        ones = jnp.ones((tk, LANES), BF16)
        pv = [jnp.dot(pi.astype(BF16), jnp.concatenate([vmat, ones], axis=1), preferred_element_type=F32)
              for pi, (_, vmat, _, _) in zip(p, chains)]
    else:
        rows = [jnp.broadcast_to(jnp.sum(pi, axis=1, keepdims=True), m.shape) for pi, m in zip(p, m_new)]
        pv = [jnp.concatenate([jnp.dot(pi.astype(BF16), vmat, preferred_element_type=F32), row], axis=1)
              for pi, (_, vmat, _, _), row in zip(p, chains, rows)]
    acc = [_lane_tile(a, x.shape[1] // LANES) * acc_prev + x for a, x, (_, _, _, acc_prev) in zip(alpha, pv, chains)]
    return list(zip(m_new, acc))


def _softmax_finish(acc):
    dv = acc.shape[1] - LANES
    return acc[:, :dv] / _lane_tile(acc[:, dv:], dv // LANES)


def _stick_tiles(chains, tri, before):
    tk = chains[0][1].shape[0]
    z = [_qk(q, kmat) * ATT_SCALE for q, kmat, _, _ in chains]
    softplus = [jnp.maximum(x, 0.0) + jnp.log(1.0 + jnp.exp(-jnp.abs(x))) for x in z]
    if before is None:
        sums = [jnp.dot(sp.astype(BF16), tri, preferred_element_type=F32) for sp in softplus]
    else:
        masked = [jnp.where(before, sp, 0.0) for sp in softplus]
        hi = [x.astype(BF16) for x in masked]
        lo = [(x - h.astype(F32)).astype(BF16) for x, h in zip(masked, hi)]
        sums = [jnp.dot(jnp.concatenate([h, l], axis=1), tri, preferred_element_type=F32) for h, l in zip(hi, lo)]
    w = [jnp.exp((x - sp) + s[:, :tk] + _lane_tile(carry, tk // LANES))
         for x, sp, s, (_, _, _, carry) in zip(z, softplus, sums, chains)]
    if before is not None:
        w = [jnp.where(before, x, 0.0) for x in w]
    pv = [jnp.dot(x.astype(BF16), vmat, preferred_element_type=F32) for x, (_, _, vmat, _) in zip(w, chains)]
    return [(p, carry + s[:, tk:]) for p, s, (_, _, _, carry) in zip(pv, sums, chains)]


def _diff_finish(acc0, acc1, lam_ref, subln_ref, gate, lam_init):
    lam_q1, lam_k1, lam_q2, lam_k2 = (lam_ref[r:r + 1, :] for r in range(4))
    lam = (jnp.exp(jnp.sum(lam_q1 * lam_k1, axis=1, keepdims=True))
           - jnp.exp(jnp.sum(lam_q2 * lam_k2, axis=1, keepdims=True)) + lam_init)
    o = _softmax_finish(acc0) - lam * _softmax_finish(acc1)
    ms = jnp.mean(o * o, axis=-1, keepdims=True)
    y = o * lax.rsqrt(ms + RMS_EPS) * subln_ref[...] * (1.0 - lam_init)
    return y * _silu(gate)


def _pad_new_tile(x, tk):
    return jnp.concatenate([x, jnp.zeros((tk - x.shape[0], x.shape[1]), x.dtype)], axis=0)


def _dsa_scores_to_keys(qih_ref, wb_ref, ki_rows, visible):
    tq = wb_ref.shape[1]
    n_keys = ki_rows.shape[0]
    s_all = _qk(qih_ref[...], ki_rows)
    acc = jnp.zeros((tq, n_keys), F32)
    for h in range(H_IDX):
        acc = acc + jnp.maximum(s_all[h * tq:(h + 1) * tq], 0.0) * _lane_tile(wb_ref[h], n_keys // LANES)
    bits = pltpu.bitcast(acc, I32)
    key = bits ^ ((bits >> 31) & 0x7FFFFFFF)
    return jnp.where(visible, key, INT_MIN)


def _dsa_prepare(q_ref, qi_ref, w_ref, qs_ref, qih_ref, wb_ref, tq):
    wv = w_ref[:, D_IDX:D_IDX + H_IDX] * IDX_SCALE
    for h in range(H_IDX):
        qih_ref[h * tq:(h + 1) * tq, :] = qi_ref[:, _cols(h, D_IDX)]
        wb_ref[h] = jnp.broadcast_to(wv[:, h:h + 1], (tq, LANES))
    for n in range(KV_A):
        for g in range(GROUP_A):
            qs_ref[n, g * tq:(g + 1) * tq, :] = q_ref[:, _cols(GROUP_A * n + g)]


def _dsa_threshold(key_ref, npairs, n_sel, idx_bits, tq, tk):
    lane = lax.broadcasted_iota(I32, (tq, tk), 1)

    def count_where(pred):
        def body(p, cnt):
            k = 2 * p
            return (cnt + jnp.where(pred(k, key_ref[k]), 1.0, 0.0)
                    + jnp.where(pred(k + 1, key_ref[k + 1]), 1.0, 0.0))
        cnt = lax.fori_loop(0, npairs, body, jnp.zeros((tq, tk), F32))
        return jnp.sum(cnt, axis=1, keepdims=True)

    def count_ge(cand):
        return count_where(lambda k, key: key >= cand)

    def thr_bit(b, t):
        cand = t + lax.shift_left(jnp.int32(1), 31 - b)
        return jnp.where(count_ge(cand) >= n_sel, cand, t)

    thr = lax.fori_loop(0, 32, thr_bit, jnp.full((tq, 1), INT_MIN, I32))
    thr = jnp.maximum(thr, INT_MIN + 1)

    @pl.when(jnp.max(count_ge(thr)) > n_sel)
    def _():
        need = n_sel - count_ge(thr + 1)

        def count_eq_below(c):
            return count_where(lambda k, key: (key == thr) & ((k * tk + lane) < c))

        def idx_bit(b, c):
            cand = c + lax.shift_left(jnp.int32(1), idx_bits - 1 - b)
            return jnp.where(count_eq_below(cand) <= need, cand, c)

        cut = lax.fori_loop(0, idx_bits, idx_bit, jnp.zeros((tq, 1), I32))

        def demote(k, carry):
            key = key_ref[k]
            key_ref[k] = jnp.where((key == thr) & ((k * tk + lane) >= cut), key - 1, key)
            return carry

        lax.fori_loop(0, 2 * npairs, demote, 0)

    return thr


def _dsa_attend_tiles(qs_ref, key_tiles, thr, bias_tiles, kv_of_head, m_ref, acc_ref, tq):
    rows4 = GROUP_A * tq
    madd = jnp.concatenate([jnp.where(key_tile >= thr, 0.0, NEG) for key_tile in key_tiles], axis=1)
    madd4 = jnp.concatenate([madd] * GROUP_A, axis=0)
    loaded = [(qs_ref[n], *kv_of_head(n), m_ref[n], acc_ref[n]) for n in range(KV_A)]
    def score_fn(n, q, kmat):
        def fn():
            bias = jnp.concatenate([b[rows4 * n:rows4 * (n + 1), :] for b in bias_tiles], axis=1)
            return _qk(q, kmat) * QK_SCALE_BITS + bias + madd4
        return fn

    new = _softmax_tiles([(score_fn(n, q, kmat), vmat, m_prev, acc_prev)
                          for n, (q, kmat, vmat, m_prev, acc_prev) in enumerate(loaded)], width=DSA_CHAINS_ABREAST)
    for n, (m_new, acc_new) in enumerate(new):
        m_ref[n], acc_ref[n] = m_new, acc_new


def _dsa_finish(acc_ref, gate_ref, o_ref, tq):
    for n in range(KV_A):
        o = _softmax_finish(acc_ref[n])
        for g in range(GROUP_A):
            cs = _cols(GROUP_A * n + g)
            o_ref[:, cs] = (o[g * tq:(g + 1) * tq] * _silu(gate_ref[:, cs])).astype(o_ref.dtype)


def _dsa_scratch(nk, tq, tk):
    rows4 = GROUP_A * tq
    return [
        pltpu.VMEM((KV_A, rows4, HEAD_DIM), BF16),
        pltpu.VMEM((H_IDX * tq, D_IDX), BF16),
        pltpu.VMEM((H_IDX, tq, LANES), F32),
        pltpu.VMEM((nk, tq, tk), I32),
        pltpu.VMEM((KV_A, rows4, LANES), F32),
        pltpu.VMEM((KV_A, rows4, HEAD_DIM + LANES), F32),
    ]


def _dsa_prompt_kernel(q_ref, qi_ref, w_ref, ki_ref, kv_ref, gate_ref, bias_ref, o_ref,
                       qs_ref, qih_ref, wb_ref, key_ref, m_ref, acc_ref, *, tq, tk, group, n_sel, idx_bits):
    i = pl.program_id(1)
    npairs = i // 2 + 1
    row_chunk = (i * tq + lax.broadcasted_iota(I32, (tq, 2 * tk), 0)) >> CHUNK_SHIFT
    lane = lax.broadcasted_iota(I32, (tq, 2 * tk), 1)
    _dsa_prepare(q_ref, qi_ref, w_ref, qs_ref, qih_ref, wb_ref, tq)

    def score_pair(p, carry):
        ks = pl.ds(pl.multiple_of(p * 2 * tk, 2 * tk), 2 * tk)
        visible = ((p * 2 * tk + lane) >> CHUNK_SHIFT) <= row_chunk
        keys = _dsa_scores_to_keys(qih_ref, wb_ref, ki_ref[ks, :], visible)
        key_ref[2 * p], key_ref[2 * p + 1] = keys[:, :tk], keys[:, tk:]
        return carry

    lax.fori_loop(0, npairs, score_pair, 0)
    for g in range(group - 2):
        key_ref[2 * npairs + g] = jnp.full((tq, tk), INT_MIN, I32)
    thr = _dsa_threshold(key_ref, npairs, n_sel, idx_bits, tq, tk)

    m_ref[...] = jnp.full(m_ref.shape, NEG, F32)
    acc_ref[...] = jnp.zeros(acc_ref.shape, F32)

    def attend_group(j, carry):
        k0 = j * group
        ks = pl.ds(pl.multiple_of(k0 * tk, group * tk), group * tk)
        key_tiles = [key_ref[k0 + g] for g in range(group)]
        bias_tiles = [bias_ref[jnp.clip(i - k0 - g, 0, N_BIAS_VARIANTS - 1)] for g in range(group)]
        kv_of_head = lambda n: (kv_ref[ks, _cols(n)], kv_ref[ks, KVA + HEAD_DIM * n:KVA + HEAD_DIM * (n + 1)])
        _dsa_attend_tiles(qs_ref, key_tiles, thr, bias_tiles, kv_of_head, m_ref, acc_ref, tq)
        return carry

    lax.fori_loop(0, i // group + 1, attend_group, 0)
    _dsa_finish(acc_ref, gate_ref, o_ref, tq)


def dsa_prompt(qcat, kiw, ki, kv, gcat, bias, *, n_sel):
    b, t, _ = qcat.shape
    tq = tk = LANES
    group = DSA_TILES_PER_STEP
    nk = t // tk
    assert nk % group == 0 and group % 2 == 0
    kern = functools.partial(_dsa_prompt_kernel, tq=tq, tk=tk, group=group, n_sel=n_sel,
                             idx_bits=int(t).bit_length())
    return pl.pallas_call(
        kern,
        grid=(b, t // tq),
        in_specs=[
            pl.BlockSpec((None, tq, QA), lambda bi, i: (bi, i, 0)),
            pl.BlockSpec((None, tq, QI), lambda bi, i: (bi, i, 2 * QA // QI)),
            pl.BlockSpec((None, tq, LANES), lambda bi, i: (bi, i, 0)),
            _resident((None, t, D_IDX), lambda bi, i: (bi, 0, 0)),
            _resident((None, t, 2 * KVA), lambda bi, i: (bi, 0, 0)),
            pl.BlockSpec((None, tq, QA), lambda bi, i: (bi, i, 0)),
            _resident((N_BIAS_VARIANTS, H_A * tq, tk), lambda bi, i: (0, 0, 0)),
        ],
        out_specs=pl.BlockSpec((None, tq, QA), lambda bi, i: (bi, i, 0)),
        out_shape=jax.ShapeDtypeStruct((b, t, QA), BF16),
        scratch_shapes=_dsa_scratch(nk + group, tq, tk),
        compiler_params=_cparams("parallel", "arbitrary"),
        name="dsa_prompt",
    )(qcat, qcat, kiw, ki, kv, gcat, bias)


def _dsa_sample_kernel(q_ref, qi_ref, w_ref, kin_ref, kvn_ref, gate_ref, bias_ref, kic_ref, kvc_ref, o_ref,
                       qs_ref, qih_ref, wb_ref, key_ref, m_ref, acc_ref, thr_ref, *, tq, p0, pt, n_sel, idx_bits):
    kt = pl.program_id(1)
    tk = TK_SAMPLE
    n_cache = p0 // tk
    sub = pt // tk

    @pl.when(kt == 0)
    def _():
        new_visible = lax.broadcasted_iota(I32, (tq, tk), 1) < tq
        _dsa_prepare(q_ref, qi_ref, w_ref, qs_ref, qih_ref, wb_ref, tq)

        def score_pair(p, carry):
            ks = pl.ds(pl.multiple_of(p * 2 * tk, 2 * tk), 2 * tk)
            keys = _dsa_scores_to_keys(qih_ref, wb_ref, kic_ref[ks, :].astype(BF16), True)
            key_ref[2 * p], key_ref[2 * p + 1] = keys[:, :tk], keys[:, tk:]
            return carry

        lax.fori_loop(0, n_cache // 2, score_pair, 0)
        ki_new = _pad_new_tile(kin_ref[:, :D_IDX].astype(BF16), tk)
        key_ref[n_cache] = _dsa_scores_to_keys(qih_ref, wb_ref, ki_new, new_visible)
        key_ref[n_cache + 1] = jnp.full((tq, tk), INT_MIN, I32)
        thr = _dsa_threshold(key_ref, n_cache // 2 + 1, n_sel, idx_bits, tq, tk)
        thr_ref[...] = jnp.broadcast_to(thr, thr_ref.shape)
        m_ref[...] = jnp.full(m_ref.shape, NEG, F32)
        acc_ref[...] = jnp.zeros(acc_ref.shape, F32)

    thr = thr_ref[...]
    k0 = kt * sub
    key_tiles = [key_ref[k0 + j] for j in range(sub)]
    bias_tiles = [bias_ref[jnp.minimum(n_cache - k0 - j, N_BIAS_VARIANTS - 1)] for j in range(sub)]
    kv_cached = lambda n: (kvc_ref[:, 0, n, :].astype(BF16), kvc_ref[:, 1, n, :].astype(BF16))
    _dsa_attend_tiles(qs_ref, key_tiles, thr, bias_tiles, kv_cached, m_ref, acc_ref, tq)

    @pl.when(kt == pl.num_programs(1) - 1)
    def _():
        kv_new = lambda n: (_pad_new_tile(kvn_ref[:, _cols(n)], tk),
                            _pad_new_tile(kvn_ref[:, KVA + HEAD_DIM * n:KVA + HEAD_DIM * (n + 1)], tk))
        _dsa_attend_tiles(qs_ref, [key_ref[n_cache]], thr, [bias_ref[0]], kv_new, m_ref, acc_ref, tq)
        _dsa_finish(acc_ref, gate_ref, o_ref, tq)


def dsa_sample(qcat, kiw, kv_new, gcat, bias, cache_kidx, cache_kv, layer, *, n_sel):
    b, tq, _ = qcat.shape
    p0 = cache_kv.shape[2]
    tk, pt = TK_SAMPLE, PT_SAMPLE
    assert p0 % pt == 0 and tq <= tk and (p0 // tk) % 2 == 0
    nk = p0 // tk + 2
    kern = functools.partial(_dsa_sample_kernel, tq=tq, p0=p0, pt=pt, n_sel=n_sel,
                             idx_bits=int(p0 + tk).bit_length())
    return pl.pallas_call(
        kern,
        grid=(b, p0 // pt),
        in_specs=[
            pl.BlockSpec((None, tq, QA), lambda bi, kt: (bi, 0, 0)),
            pl.BlockSpec((None, tq, QI), lambda bi, kt: (bi, 0, 2 * QA // QI)),
            pl.BlockSpec((None, tq, LANES), lambda bi, kt: (bi, 0, 0)),
            pl.BlockSpec((None, tq, LANES), lambda bi, kt: (bi, 0, 0)),
            pl.BlockSpec((None, tq, 2 * KVA), lambda bi, kt: (bi, 0, 0)),
            pl.BlockSpec((None, tq, QA), lambda bi, kt: (bi, 0, 0)),
            pl.BlockSpec((N_BIAS_VARIANTS, H_A * tq, tk), lambda bi, kt: (0, 0, 0)),
            pl.BlockSpec((None, None, p0, D_IDX), lambda bi, kt: (layer, bi, 0, 0)),
            pl.BlockSpec((None, None, pt, 2, KV_A, HEAD_DIM), lambda bi, kt: (layer, bi, kt, 0, 0, 0)),
        ],
        out_specs=pl.BlockSpec((None, tq, QA), lambda bi, kt: (bi, 0, 0)),
        out_shape=jax.ShapeDtypeStruct((b, tq, QA), BF16),
        scratch_shapes=_dsa_scratch(nk, tq, tk) + [pltpu.VMEM((tq, tk), I32)],
        compiler_params=_cparams("parallel", "arbitrary"),
        name="dsa_sample",
    )(qcat, qcat, kiw, kiw, kv_new, gcat, bias, cache_kidx, cache_kv)


def _diff_prompt_kernel(q_ref, k_ref, v_ref, gate_ref, bias_ref, lam_ref, subln_ref, o_ref,
                        m_ref, acc_ref, *, tq, tk, hb, lam_init):
    i = pl.program_id(2)
    row_chunk = (i * tq + lax.broadcasted_iota(I32, (tq, tk), 0)) >> CHUNK_SHIFT
    lane = lax.broadcasted_iota(I32, (tq, tk), 1)
    m_ref[...] = jnp.full(m_ref.shape, NEG, F32)
    acc_ref[...] = jnp.zeros(acc_ref.shape, F32)

    def step(k, masked):
        ks = pl.ds(pl.multiple_of(k * tk, tk), tk)
        variant = jnp.minimum(i - k, N_BIAS_VARIANTS - 1)
        madd = jnp.where(((k * tk + lane) >> CHUNK_SHIFT) <= row_chunk, 0.0, NEG) if masked else None
        loaded = []
        for h in range(hb):
            bias = bias_ref[variant, h]
            if masked:
                bias = bias + madd
            vmat = v_ref[ks, _cols(h, DV_B)]
            for c in range(2):
                col = slice(DV_B * h + HEAD_DIM * c, DV_B * h + HEAD_DIM * (c + 1))
                loaded.append((h, c, q_ref[:, col], k_ref[ks, col], vmat, bias, m_ref[h, c], acc_ref[h, c]))
        score_fn = lambda q, kmat, bias: lambda: _qk(q, kmat) * QK_SCALE_BITS + bias
        new = _softmax_tiles([(score_fn(q, kmat, bias), vmat, m_prev, acc_prev)
                              for _, _, q, kmat, vmat, bias, m_prev, acc_prev in loaded],
                             mxu_rowsum=False, width=DIFF_CHAINS_ABREAST)
        for (h, c, *_), (m_new, acc_new) in zip(loaded, new):
            m_ref[h, c], acc_ref[h, c] = m_new, acc_new

    def full_tile(k, carry):
        step(k, False)
        return carry

    lax.fori_loop(0, i, full_tile, 0)
    step(i, True)
    for h in range(hb):
        cs = _cols(h, DV_B)
        o_ref[:, cs] = _diff_finish(acc_ref[h, 0], acc_ref[h, 1], lam_ref, subln_ref, gate_ref[:, cs],
                                    lam_init).astype(o_ref.dtype)


def diff_prompt(qcat, kv, gcat, bias, lam_vec, subln, *, tq, hb, lam_init):
    b, t, _ = qcat.shape
    tk = tq
    w = hb * DV_B
    kern = functools.partial(_diff_prompt_kernel, tq=tq, tk=tk, hb=hb, lam_init=lam_init)
    return pl.pallas_call(
        kern,
        grid=(b, H_B // hb, t // tq),
        in_specs=[
            pl.BlockSpec((None, tq, w), lambda bi, g, i: (bi, i, QA // w + g)),
            _resident((None, t, w), lambda bi, g, i: (bi, 0, g)),
            _resident((None, t, w), lambda bi, g, i: (bi, 0, QA // w + g)),
            pl.BlockSpec((None, tq, w), lambda bi, g, i: (bi, i, QA // w + g)),
            _resident((N_BIAS_VARIANTS, hb, tq, tk), lambda bi, g, i: (0, g, 0, 0)),
            pl.BlockSpec((4, HEAD_DIM), lambda bi, g, i: (0, 0)),
            pl.BlockSpec((1, DV_B), lambda bi, g, i: (0, 0)),
        ],
        out_specs=pl.BlockSpec((None, tq, w), lambda bi, g, i: (bi, i, g)),
        out_shape=jax.ShapeDtypeStruct((b, t, QA), BF16),
        scratch_shapes=[pltpu.VMEM((hb, 2, tq, LANES), F32), pltpu.VMEM((hb, 2, tq, DV_B + LANES), F32)],
        compiler_params=_cparams("parallel", "parallel", "arbitrary"),
        name="diff_prompt",
    )(qcat, kv, kv, gcat, bias, lam_vec, subln.reshape(1, DV_B))


def _cache_rows_2d(cache):
    n, b, p, two, h, d = cache.shape
    assert two == 2 and h % 8 == 0
    return cache.reshape(n, b, p * two * h, d)


def _head_rows(c_ref, key0, n_keys, n_heads, sel, h):
    period = 2 * n_heads
    return c_ref[pl.ds(key0 * period + sel * n_heads + h, n_keys, stride=period), :]


def _diff_sample_kernel(q_ref, kn_ref, vn_ref, gate_ref, bias_ref, lam_ref, subln_ref, c0_ref, c1_ref, o_ref,
                        m_ref, acc_ref, *, tq, p0, pt, lam_init):
    kt = pl.program_id(1)
    tk = TK_SAMPLE
    n_cache = p0 // tk
    sub = pt // tk

    @pl.when(kt == 0)
    def _():
        m_ref[...] = jnp.full(m_ref.shape, NEG, F32)
        acc_ref[...] = jnp.zeros(acc_ref.shape, F32)

    def update_all(tile_of_head):
        loaded = []
        for h in range(H_B):
            kmat2, vmat, bias = tile_of_head(h)
            for c in range(2):
                q = q_ref[:, DV_B * h + HEAD_DIM * c:DV_B * h + HEAD_DIM * (c + 1)]
                loaded.append((h, c, q, kmat2[:, _cols(c)], vmat, bias, m_ref[h, c], acc_ref[h, c]))
        score_fn = lambda q, kmat, bias: lambda: _qk(q, kmat) * QK_SCALE_BITS + bias
        new = _softmax_tiles([(score_fn(q, kmat, bias), vmat, m_prev, acc_prev)
                              for _, _, q, kmat, vmat, bias, m_prev, acc_prev in loaded], mxu_rowsum=False)
        for (h, c, *_), (m_new, acc_new) in zip(loaded, new):
            m_ref[h, c], acc_ref[h, c] = m_new, acc_new

    def cache_tile(j, carry):
        variant = jnp.minimum(n_cache - (kt * sub + j), N_BIAS_VARIANTS - 1)
        halves = lambda sel, h: jnp.concatenate(
            [_head_rows(c_ref, j * tk, tk, H_B, sel, h).astype(BF16) for c_ref in (c0_ref, c1_ref)], axis=1)
        update_all(lambda h: (halves(0, h), halves(1, h), bias_ref[variant, h]))
        return carry

    lax.fori_loop(0, sub, cache_tile, 0)

    @pl.when(kt == pl.num_programs(1) - 1)
    def _():
        new_madd = jnp.where(lax.broadcasted_iota(I32, (tq, tk), 1) < tq, 0.0, NEG)
        update_all(lambda h: (_pad_new_tile(kn_ref[:, _cols(h, DV_B)], tk), _pad_new_tile(vn_ref[:, _cols(h, DV_B)], tk),
                              bias_ref[0, h] + new_madd))
        for h in range(H_B):
            cs = _cols(h, DV_B)
            o_ref[:, cs] = _diff_finish(acc_ref[h, 0], acc_ref[h, 1], lam_ref, subln_ref, gate_ref[:, cs],
                                        lam_init).astype(o_ref.dtype)


def diff_sample(qcat, kv_new, gcat, bias, lam_vec, subln, cache_kv, layer, *, lam_init):
    b, tq, _ = qcat.shape
    p0 = cache_kv.shape[2]
    tk, pt = TK_SAMPLE, PT_SAMPLE
    assert p0 % pt == 0 and tq <= tk
    kern = functools.partial(_diff_sample_kernel, tq=tq, p0=p0, pt=pt, lam_init=lam_init)
    cache_spec = lambda c: pl.BlockSpec((None, None, pt * 2 * H_B, LANES), lambda bi, kt: (layer, bi, kt, c))
    cache_2d = _cache_rows_2d(cache_kv)
    return pl.pallas_call(
        kern,
        grid=(b, p0 // pt),
        in_specs=[
            pl.BlockSpec((None, tq, QA), lambda bi, kt: (bi, 0, 1)),
            pl.BlockSpec((None, tq, QA), lambda bi, kt: (bi, 0, 0)),
            pl.BlockSpec((None, tq, QA), lambda bi, kt: (bi, 0, 1)),
            pl.BlockSpec((None, tq, QA), lambda bi, kt: (bi, 0, 1)),
            pl.BlockSpec((N_BIAS_VARIANTS, H_B, tq, tk), lambda bi, kt: (0, 0, 0, 0)),
            pl.BlockSpec((4, HEAD_DIM), lambda bi, kt: (0, 0)),
            pl.BlockSpec((1, DV_B), lambda bi, kt: (0, 0)),
            cache_spec(0), cache_spec(1),
        ],
        out_specs=pl.BlockSpec((None, tq, QA), lambda bi, kt: (bi, 0, 0)),
        out_shape=jax.ShapeDtypeStruct((b, tq, QA), BF16),
        scratch_shapes=[pltpu.VMEM((H_B, 2, tq, LANES), F32), pltpu.VMEM((H_B, 2, tq, DV_B + LANES), F32)],
        compiler_params=_cparams("parallel", "arbitrary"),
        name="diff_sample",
    )(qcat, kv_new, kv_new, gcat, bias, lam_vec, subln.reshape(1, DV_B), cache_2d, cache_2d)


def _stick_prompt_kernel(q_ref, k_ref, v_ref, gate_ref, tri_ref, o_ref, acc_ref, carry_ref, *, tq, tk, hb):
    i = pl.program_id(2)
    before = lax.broadcasted_iota(I32, (tq, tk), 1) < lax.broadcasted_iota(I32, (tq, tk), 0)
    acc_ref[...] = jnp.zeros(acc_ref.shape, F32)
    carry_ref[...] = jnp.zeros(carry_ref.shape, F32)

    def step(k, mask):
        ks = pl.ds(pl.multiple_of(k * tk, tk), tk)
        tri = tri_ref[:tk, :] if mask is None else tri_ref[...]
        loaded = [(q_ref[:, _cols(h)], k_ref[ks, _cols(h)], v_ref[ks, _cols(h)], carry_ref[h], acc_ref[h])
                  for h in range(hb)]
        new = _stick_tiles([(q, kmat, vmat, carry_prev) for q, kmat, vmat, carry_prev, _ in loaded], tri, mask)
        for h, ((pv, carry), (_, _, _, _, acc)) in enumerate(zip(new, loaded)):
            acc_ref[h], carry_ref[h] = acc + pv, carry

    step(i, before)

    def full_tile(j, carry):
        step(i - 1 - j, None)
        return carry

    lax.fori_loop(0, i, full_tile, 0)
    for h in range(hb):
        cs = _cols(h)
        o_ref[:, cs] = (acc_ref[h] * _silu(gate_ref[:, cs])).astype(o_ref.dtype)


def stick_prompt(qcat, kv, gcat, tri_ext, *, tq, hb):
    b, t, _ = qcat.shape
    tk = tq
    w = hb * HEAD_DIM
    kern = functools.partial(_stick_prompt_kernel, tq=tq, tk=tk, hb=hb)
    return pl.pallas_call(
        kern,
        grid=(b, H_C // hb, t // tq),
        in_specs=[
            pl.BlockSpec((None, tq, w), lambda bi, g, i: (bi, i, g)),
            _resident((None, t, w), lambda bi, g, i: (bi, 0, g)),
            _resident((None, t, w), lambda bi, g, i: (bi, 0, QA // w + g)),
            pl.BlockSpec((None, tq, w), lambda bi, g, i: (bi, i, g)),
            _resident((2 * tk, tk + LANES), lambda bi, g, i: (0, 0)),
        ],
        out_specs=pl.BlockSpec((None, tq, w), lambda bi, g, i: (bi, i, g)),
        out_shape=jax.ShapeDtypeStruct((b, t, QA), BF16),
        scratch_shapes=[pltpu.VMEM((hb, tq, HEAD_DIM), F32), pltpu.VMEM((hb, tq, LANES), F32)],
        compiler_params=_cparams("parallel", "parallel", "arbitrary"),
        name="stick_prompt",
    )(qcat, kv, kv, gcat, tri_ext)


def _stick_sample_kernel(q_ref, kn_ref, vn_ref, gate_ref, tri_ref, c_ref, o_ref, acc_ref, carry_ref, *, tq, pt):
    kt = pl.program_id(1)
    tk = TK_SAMPLE
    sub = pt // tk
    head_groups = [range(g * HB_SAMPLE, (g + 1) * HB_SAMPLE) for g in range(H_C // HB_SAMPLE)]

    @pl.when(kt == 0)
    def _():
        before = lax.broadcasted_iota(I32, (tq, tk), 1) < lax.broadcasted_iota(I32, (tq, tk), 0)
        tri = tri_ref[...]
        for heads in head_groups:
            new = _stick_tiles([(q_ref[:, _cols(h)], _pad_new_tile(kn_ref[:, _cols(h)], tk),
                                 _pad_new_tile(vn_ref[:, _cols(h)], tk), jnp.zeros((tq, LANES), F32))
                                for h in heads], tri, before)
            for h, (pv, carry) in zip(heads, new):
                acc_ref[h], carry_ref[h] = pv, carry

    def cache_tile(j, carry):
        key0 = (sub - 1 - j) * tk
        tri = tri_ref[:tk, :]
        for heads in head_groups:
            loaded = [(q_ref[:, _cols(h)], _head_rows(c_ref, key0, tk, H_C, 0, h).astype(BF16),
                       _head_rows(c_ref, key0, tk, H_C, 1, h).astype(BF16), carry_ref[h], acc_ref[h]) for h in heads]
            new = _stick_tiles([(q, kmat, vmat, c_prev) for q, kmat, vmat, c_prev, _ in loaded], tri, None)
            for h, (pv, c), (_, _, _, _, acc) in zip(heads, new, loaded):
                acc_ref[h], carry_ref[h] = acc + pv, c
        return carry

    lax.fori_loop(0, sub, cache_tile, 0)

    @pl.when(kt == pl.num_programs(1) - 1)
    def _():
        for h in range(H_C):
            cs = _cols(h)
            o_ref[:, cs] = (acc_ref[h] * _silu(gate_ref[:, cs])).astype(o_ref.dtype)


def stick_sample(qcat, kv_new, gcat, tri_ext, cache_kv, layer):
    b, tq, _ = qcat.shape
    p0 = cache_kv.shape[2]
    pt = PT_SAMPLE
    assert p0 % pt == 0 and tq <= TK_SAMPLE
    n_kt = p0 // pt
    kern = functools.partial(_stick_sample_kernel, tq=tq, pt=pt)
    return pl.pallas_call(
        kern,
        grid=(b, n_kt),
        in_specs=[
            pl.BlockSpec((None, tq, QA), lambda bi, kt: (bi, 0, 0)),
            pl.BlockSpec((None, tq, QA), lambda bi, kt: (bi, 0, 0)),
            pl.BlockSpec((None, tq, QA), lambda bi, kt: (bi, 0, 1)),
            pl.BlockSpec((None, tq, QA), lambda bi, kt: (bi, 0, 0)),
            pl.BlockSpec((2 * TK_SAMPLE, TK_SAMPLE + LANES), lambda bi, kt: (0, 0)),
            pl.BlockSpec((None, None, pt * 2 * H_C, HEAD_DIM), lambda bi, kt: (layer, bi, n_kt - 1 - kt, 0)),
        ],
        out_specs=pl.BlockSpec((None, tq, QA), lambda bi, kt: (bi, 0, 0)),
        out_shape=jax.ShapeDtypeStruct((b, tq, QA), BF16),
        scratch_shapes=[pltpu.VMEM((H_C, tq, HEAD_DIM), F32), pltpu.VMEM((H_C, tq, LANES), F32)],
        compiler_params=_cparams("parallel", "arbitrary"),
        name="stick_sample",
    )(qcat, kv_new, kv_new, gcat, tri_ext, _cache_rows_2d(cache_kv))


def _band_prompt_kernel(q_ref, k_ref, v_ref, gate_ref, bias_ref, o_ref, *, tq, win, hb):
    i = pl.program_id(2)
    span = win + tq
    rows = pl.ds(pl.multiple_of(i * tq, tq), span)
    row_chunk = (i * tq + lax.broadcasted_iota(I32, (tq, span), 0)) >> CHUNK_SHIFT
    key_chunk = (i * tq - win + lax.broadcasted_iota(I32, (tq, span), 1)) >> CHUNK_SHIFT
    visible = (key_chunk <= row_chunk) & (key_chunk >= jnp.maximum(row_chunk - BAND_CHUNKS, 0))
    madd = jnp.where(visible, 0.0, NEG)
    ones = jnp.ones((span, LANES), BF16)
    for h in range(hb):
        cs = _cols(h)
        s = _qk(q_ref[:, cs], k_ref[rows, cs]) * QK_SCALE_BITS + bias_ref[h] + madd
        p = jnp.exp2(s - jnp.max(s, axis=1, keepdims=True))
        pv = jnp.dot(p.astype(BF16), jnp.concatenate([v_ref[rows, cs], ones], axis=1), preferred_element_type=F32)
        o_ref[:, cs] = (_softmax_finish(pv) * _silu(gate_ref[:, cs])).astype(o_ref.dtype)


def band_prompt(qcat, kv, gcat, bias, *, hb):
    b, t, _ = qcat.shape
    tq = LANES
    win = BAND_CHUNKS * CHUNK
    w = hb * HEAD_DIM
    assert kv.shape[1] == win + t and bias.shape == (H_D, tq, win + tq)
    kern = functools.partial(_band_prompt_kernel, tq=tq, win=win, hb=hb)
    return pl.pallas_call(
        kern,
        grid=(b, H_D // hb, t // tq),
        in_specs=[
            pl.BlockSpec((None, tq, w), lambda bi, g, i: (bi, i, QA // w + g)),
            _resident((None, win + t, w), lambda bi, g, i: (bi, 0, g)),
            _resident((None, win + t, w), lambda bi, g, i: (bi, 0, QA // w + g)),
            pl.BlockSpec((None, tq, w), lambda bi, g, i: (bi, i, QA // w + g)),
            _resident((hb, tq, win + tq), lambda bi, g, i: (g, 0, 0)),
        ],
        out_specs=pl.BlockSpec((None, tq, w), lambda bi, g, i: (bi, i, g)),
        out_shape=jax.ShapeDtypeStruct((b, t, QA), BF16),
        compiler_params=_cparams("parallel", "parallel", "arbitrary"),
        name="band_prompt",
    )(qcat, kv, kv, gcat, bias)


def _band_sample_kernel(q_ref, kn_ref, vn_ref, gate_ref, bias_ref, c_ref, o_ref, *, tq, win):
    g = pl.program_id(1)
    tk = TK_SAMPLE
    lane = lax.broadcasted_iota(I32, (tq, win + tk), 1)
    madd = jnp.where(lane < win + tq, 0.0, NEG)
    ones = jnp.ones((win + tk, LANES), BF16)
    for h in range(HB_SAMPLE):
        cs = _cols(h)
        head = g * HB_SAMPLE + h
        kmat = jnp.concatenate([_head_rows(c_ref, 0, win, H_D, 0, head).astype(BF16),
                                _pad_new_tile(kn_ref[:, cs], tk)], axis=0)
        vmat = jnp.concatenate([_head_rows(c_ref, 0, win, H_D, 1, head).astype(BF16),
                                _pad_new_tile(vn_ref[:, cs], tk)], axis=0)
        s = _qk(q_ref[:, cs], kmat) * QK_SCALE_BITS + bias_ref[h] + madd
        p = jnp.exp2(s - jnp.max(s, axis=1, keepdims=True))
        pv = jnp.dot(p.astype(BF16), jnp.concatenate([vmat, ones], axis=1), preferred_element_type=F32)
        o_ref[:, cs] = (_softmax_finish(pv) * _silu(gate_ref[:, cs])).astype(o_ref.dtype)


def band_sample(qcat, kv_new, gcat, bias, cache_kv, layer):
    b, tq, _ = qcat.shape
    win = cache_kv.shape[2]
    w = HB_SAMPLE * HEAD_DIM
    assert win == BAND_CHUNKS * CHUNK and tq == CHUNK and bias.shape == (H_D, tq, win + TK_SAMPLE)
    kern = functools.partial(_band_sample_kernel, tq=tq, win=win)
    return pl.pallas_call(
        kern,
        grid=(b, H_D // HB_SAMPLE),
        in_specs=[
            pl.BlockSpec((None, tq, w), lambda bi, g: (bi, 0, QA // w + g)),
            pl.BlockSpec((None, tq, w), lambda bi, g: (bi, 0, g)),
            pl.BlockSpec((None, tq, w), lambda bi, g: (bi, 0, QA // w + g)),
            pl.BlockSpec((None, tq, w), lambda bi, g: (bi, 0, QA // w + g)),
            pl.BlockSpec((HB_SAMPLE, tq, win + TK_SAMPLE), lambda bi, g: (g, 0, 0)),
            pl.BlockSpec((None, None, win * 2 * H_D, HEAD_DIM), lambda bi, g: (layer, bi, 0, 0)),
        ],
        out_specs=pl.BlockSpec((None, tq, w), lambda bi, g: (bi, 0, g)),
        out_shape=jax.ShapeDtypeStruct((b, tq, QA), BF16),
        compiler_params=_cparams("parallel", "arbitrary"),
        name="band_sample",
    )(qcat, kv_new, kv_new, gcat, bias, _cache_rows_2d(cache_kv))


def _t5_bucket_np(rel):
    half = T5_BUCKETS // 2
    max_exact = half // 2
    n = np.abs(rel)
    nf = np.maximum(n, 1).astype(np.float64)
    large = max_exact + (np.log(nf / max_exact) / math.log(T5_MAX_DIST / max_exact)
                         * (half - max_exact)).astype(np.int32)
    large = np.minimum(large, half - 1)
    return np.where(rel < 0, half, 0) + np.where(n < max_exact, n, large)


def _band_index_np(rel):
    return np.clip(rel, -REL_CLIP, REL_CLIP) + REL_CLIP


def toeplitz_bias(table, index_of_rel, r0, n_rows, n_cols):
    rel = r0 + np.arange(n_rows)[:, None] - np.arange(n_cols)[None, :]
    idx = jnp.asarray(index_of_rel(rel).reshape(-1).astype(np.int32))
    onehot = (idx[None, :] == jnp.arange(table.shape[0], dtype=I32)[:, None]).astype(F32)
    out = jnp.einsum("bh,bn->hn", table, onehot, precision=lax.Precision.HIGHEST)
    return out.reshape(table.shape[1], n_rows, n_cols)


def bias_tiles(table, index_of_rel, tq, tk):
    far = (N_BIAS_VARIANTS - 1) * tk
    assert (index_of_rel(np.arange(far - tk + 1, far + tq)) == index_of_rel(np.array(far + tq))).all()
    return jnp.stack([toeplitz_bias(table, index_of_rel, v * tk, tq, tk) for v in range(N_BIAS_VARIANTS)], axis=0)


def band_window_bias(rel_tab, tq, win):
    far_cols = win - REL_CLIP
    return jnp.concatenate(
        [jnp.broadcast_to(rel_tab[2 * REL_CLIP][:, None, None], (H_D, tq, far_cols)),
         toeplitz_bias(rel_tab, _band_index_np, REL_CLIP, tq, win + LANES - far_cols)], axis=-1)


def _tri_ext(tk):
    tri = -np.tril(np.ones((tk, tk), np.float32), -1)
    ext = np.concatenate([tri, -np.ones((tk, LANES), np.float32)], axis=1)
    return jnp.asarray(np.concatenate([ext, ext], axis=0), BF16)


def _even_weights(w_in):
    c = np.cumsum((0, QA, KVA, KVA, QI, D_IDX, H_IDX, QA, QA, QA, QA, QA)).tolist()
    aq, ak, av, aqi, aki, aw, ag, bq, bk, bv, bg = (w_in[:, c[r]:c[r + 1]] for r in range(11))
    pad = jnp.zeros((w_in.shape[0], LANES - D_IDX - H_IDX), w_in.dtype)
    cat = lambda *xs: jnp.concatenate(xs, axis=1).astype(BF16)
    return cat(aq, bq, aqi), cat(ak, av), cat(bk, bv), cat(aki, aw, pad), cat(ag, bg)


def _odd_weights(w_in):
    cq, ck, cv, cg, dq, dk, dv, dg = (w_in[:, QA * r:QA * (r + 1)] for r in range(8))
    cat = lambda *xs: jnp.concatenate(xs, axis=1).astype(BF16)
    return cat(cq, dq), cat(ck, cv), cat(dk, dv), cat(cg, dg)


def _even_layer(h, caches, layer, weights, t5_tab, lam_vec, subln, lam_init):
    w_q, w_akv, w_bkv, w_kiw, w_g = weights
    b, t, d = h.shape
    h2 = h.reshape(b * t, d)
    (qcat,) = matmul(h2, w_q, (BF16,))
    akv, akv_b = matmul(h2, w_akv, (F32, BF16))
    bkv, bkv_b = matmul(h2, w_bkv, (F32, BF16))
    (kiw,) = matmul(h2, w_kiw, (F32,))
    (gcat,) = matmul(h2, w_g, (F32,))
    qcat, akv_b, bkv_b, kiw, gcat = (x.reshape(b, t, -1) for x in (qcat, akv_b, bkv_b, kiw, gcat))
    aki = kiw[..., :D_IDX]
    t5_a, t5_b = t5_tab[:, :H_A] * LOG2E, t5_tab[:, H_A:] * LOG2E
    if caches is None:
        tq_b = 256
        bias_a = bias_tiles(t5_a, _t5_bucket_np, LANES, LANES).reshape(N_BIAS_VARIANTS, H_A * LANES, LANES)
        o_a = dsa_prompt(qcat, kiw, aki.astype(BF16), akv_b, gcat, bias_a, n_sel=min(TOPK_MAX, t // 4))
        o_b = diff_prompt(qcat, bkv_b, gcat, bias_tiles(t5_b, _t5_bucket_np, tq_b, tq_b), lam_vec, subln,
                          tq=tq_b, hb=4, lam_init=lam_init)
    else:
        cache_a, cache_ki, cache_b = caches
        p0 = cache_a.shape[2]
        bias_a = bias_tiles(t5_a, _t5_bucket_np, t, TK_SAMPLE).reshape(N_BIAS_VARIANTS, H_A * t, TK_SAMPLE)
        o_a = dsa_sample(qcat, kiw, akv_b, gcat, bias_a, cache_ki, cache_a, layer,
                         n_sel=min(TOPK_MAX, (p0 + t) // 4))
        o_b = diff_sample(qcat, bkv_b, gcat, bias_tiles(t5_b, _t5_bucket_np, t, TK_SAMPLE), lam_vec, subln,
                          cache_b, layer, lam_init=lam_init)
    mixed = (o_a.reshape(b * t, QA), o_b.reshape(b * t, QA))
    return mixed, akv.reshape(b, t, -1), aki, bkv.reshape(b, t, -1)


def _odd_layer(h, caches, layer, weights, rel_tab):
    w_q, w_ckv, w_dkv, w_g = weights
    b, t, d = h.shape
    h2 = h.reshape(b * t, d)
    (qcat,) = matmul(h2, w_q, (BF16,))
    ckv, ckv_b = matmul(h2, w_ckv, (F32, BF16))
    dkv, dkv_b = matmul(h2, w_dkv, (F32, BF16))
    (gcat,) = matmul(h2, w_g, (F32,))
    qcat, ckv_b, dkv_b, gcat = (x.reshape(b, t, -1) for x in (qcat, ckv_b, dkv_b, gcat))
    win = BAND_CHUNKS * CHUNK
    rel_bits = rel_tab * LOG2E
    if caches is None:
        tq_c = 256
        o_c = stick_prompt(qcat, ckv_b, gcat, _tri_ext(tq_c), tq=tq_c, hb=4)
        o_d = band_prompt(qcat, jnp.pad(dkv_b, ((0, 0), (win, 0), (0, 0))), gcat,
                          band_window_bias(rel_bits, LANES, win), hb=4)
    else:
        cache_c, cache_d = caches
        o_c = stick_sample(qcat, ckv_b, gcat, _tri_ext(TK_SAMPLE), cache_c, layer)
        o_d = band_sample(qcat, dkv_b, gcat, band_window_bias(rel_bits, t, win), cache_d, layer)
    mixed = (o_c.reshape(b * t, QA), o_d.reshape(b * t, QA))
    return mixed, ckv.reshape(b, t, -1), dkv.reshape(b, t, -1)


def _finish_layer(x, mixed, w_out, p_i, g_post, w_proj, g_pl, w_gate):
    mix = matmul_pair(*mixed, w_out)
    x1, x1_b = residual_norm(x, mix, g_post)
    e = embed_norm(p_i.astype(BF16), w_proj, g_pl)
    return gate_matmul(x1_b, w_gate, x1, e)


def kernel(x_prompt, x_sample, p_prompt, p_sample, cache_a_kv, cache_a_kidx, cache_b_kv, cache_c_kv,
           cache_d_kv, norm_pre, norm_post, w_in_even, w_out_even, t5_bias, diff_lambda, diff_subln,
           w_in_odd, w_out_odd, d_rel_bias, w_pl_proj, pl_norm, w_pl_gate):
    bp, tp, d = x_prompt.shape
    bs, ts, _ = x_sample.shape
    depth = norm_pre.shape[0]
    assert d == D_MODEL and ts == CHUNK and tp % 256 == 0
    xp = x_prompt.reshape(bp * tp, d)
    xs = x_sample.reshape(bs * ts, d)
    outs = {name: [] for name in ("a_kv_p", "a_kv_s", "a_ki_p", "a_ki_s", "b_kv_p", "b_kv_s",
                                  "c_kv_p", "c_kv_s", "d_kv_p", "d_kv_s")}
    for i in range(depth):
        j = i // 2
        hp = rmsnorm_cast(xp, norm_pre[i]).reshape(bp, tp, d)
        hs = rmsnorm_cast(xs, norm_pre[i]).reshape(bs, ts, d)
        if i % 2 == 0:
            lam_init = 0.8 - 0.6 * math.exp(-0.3 * i)
            weights = _even_weights(w_in_even[j])
            w_out = w_out_even[j].astype(BF16)
            prm = (weights, t5_bias, diff_lambda[j], diff_subln[j], lam_init)
            mp, akv, aki, bkv = _even_layer(hp, None, j, *prm)
            outs["a_kv_p"].append(akv.reshape(bp, tp, 2, KV_A, HEAD_DIM))
            outs["a_ki_p"].append(aki)
            outs["b_kv_p"].append(bkv.reshape(bp, tp, 2, H_B, DV_B))
            ms, akv, aki, bkv = _even_layer(hs, (cache_a_kv, cache_a_kidx, cache_b_kv), j, *prm)
            outs["a_kv_s"].append(akv.reshape(bs, ts, 2, KV_A, HEAD_DIM))
            outs["a_ki_s"].append(aki)
            outs["b_kv_s"].append(bkv.reshape(bs, ts, 2, H_B, DV_B))
        else:
            weights = _odd_weights(w_in_odd[j])
            w_out = w_out_odd[j].astype(BF16)
            mp, ckv, dkv = _odd_layer(hp, None, j, weights, d_rel_bias[j])
            win_p = min(BAND_CHUNKS * CHUNK, tp)
            outs["c_kv_p"].append(ckv.reshape(bp, tp, 2, H_C, HEAD_DIM))
            outs["d_kv_p"].append(dkv[:, tp - win_p:].reshape(bp, win_p, 2, H_D, HEAD_DIM))
            ms, ckv, dkv = _odd_layer(hs, (cache_c_kv, cache_d_kv), j, weights, d_rel_bias[j])
            outs["c_kv_s"].append(ckv.reshape(bs, ts, 2, H_C, HEAD_DIM))
            full_d = jnp.concatenate([cache_d_kv[j], dkv.reshape(bs, ts, 2, H_D, HEAD_DIM)], axis=1)
            outs["d_kv_s"].append(full_d[:, ts:])
        fin = (norm_post[i], w_pl_proj[i].astype(BF16), pl_norm[i], w_pl_gate[i].astype(BF16))
        xp = _finish_layer(xp, mp, w_out, p_prompt[i].reshape(bp * tp, -1), *fin)
        xs = _finish_layer(xs, ms, w_out, p_sample[i].reshape(bs * ts, -1), *fin)
    st = lambda name: jnp.stack(outs[name])
    return (xp.reshape(bp, tp, d), xs.reshape(bs, ts, d), st("a_kv_p"), st("a_kv_s"), st("a_ki_p"), st("a_ki_s"),
            st("b_kv_p"), st("b_kv_s"), st("c_kv_p"), st("c_kv_s"), st("d_kv_p"), st("d_kv_s"))
```

```python
import functools
import math

import numpy as np
import jax
import jax.numpy as jnp
from jax import lax
from jax.experimental import pallas as pl
from jax.experimental.pallas import tpu as pltpu

F32 = jnp.float32
BF16 = jnp.bfloat16
I32 = jnp.int32

D_MODEL = 4096
CHUNK = 64
CHUNK_SHIFT = 6
HEAD_DIM = 128
H_A = 16
KV_A = 4
GROUP_A = H_A // KV_A
H_IDX = 16
D_IDX = 64
TOPK_MAX = 256
H_B = 8
H_C = 16
H_D = 16
BAND_CHUNKS = 8
REL_CLIP = 128
T5_BUCKETS = 32
T5_MAX_DIST = 128
RMS_EPS = 1e-6
QA = H_A * HEAD_DIM
KVA = KV_A * HEAD_DIM
QI = H_IDX * D_IDX
DV_B = 2 * HEAD_DIM
ATT_SCALE = HEAD_DIM ** -0.5
IDX_SCALE = (D_IDX ** -0.5) * (H_IDX ** -0.5)
LOG2E = math.log2(math.e)
QK_SCALE_BITS = ATT_SCALE * LOG2E

LANES = 128
NEG = -1e30
INT_MIN = -(2 ** 31)
VMEM_LIMIT = 56 * 1024 * 1024
N_BIAS_VARIANTS = 3
TK_SAMPLE = LANES
PT_SAMPLE = 512
HB_SAMPLE = 8
DSA_TILES_PER_STEP = 4
DSA_CHAINS_ABREAST = 4
DIFF_CHAINS_ABREAST = 4


def _cparams(*sem):
    return pltpu.CompilerParams(dimension_semantics=sem, vmem_limit_bytes=VMEM_LIMIT)


def _sigmoid(x):
    return 1.0 / (1.0 + jnp.exp(-x))


def _silu(x):
    return x * _sigmoid(x)


def _lane_tile(x, n):
    return x if n == 1 else jnp.concatenate([x] * n, axis=1)


def _cols(h, width=HEAD_DIM):
    return slice(width * h, width * (h + 1))


def _rmsnorm_cast_kernel(x_ref, g_ref, o_ref):
    x = x_ref[...]
    ms = jnp.mean(x * x, axis=-1, keepdims=True)
    o_ref[...] = (x * lax.rsqrt(ms + RMS_EPS) * g_ref[...]).astype(o_ref.dtype)


def rmsnorm_cast(x, g, bm=256):
    m, d = x.shape
    bm = min(bm, m)
    return pl.pallas_call(
        _rmsnorm_cast_kernel,
        grid=(m // bm,),
        in_specs=[pl.BlockSpec((bm, d), lambda i: (i, 0)), pl.BlockSpec((1, d), lambda i: (0, 0))],
        out_specs=pl.BlockSpec((bm, d), lambda i: (i, 0)),
        out_shape=jax.ShapeDtypeStruct((m, d), BF16),
        compiler_params=_cparams("parallel"),
        name="rmsnorm_cast",
    )(x, g.reshape(1, d))


def _residual_norm_kernel(x_ref, mix_ref, g_ref, o_ref, ob_ref):
    mix = mix_ref[...]
    ms = jnp.mean(mix * mix, axis=-1, keepdims=True)
    x1 = x_ref[...] + mix * lax.rsqrt(ms + RMS_EPS) * g_ref[...]
    o_ref[...] = x1
    ob_ref[...] = x1.astype(BF16)


def residual_norm(x, mix, g, bm=256):
    m, d = x.shape
    bm = min(bm, m)
    row = pl.BlockSpec((bm, d), lambda i: (i, 0))
    return pl.pallas_call(
        _residual_norm_kernel,
        grid=(m // bm,),
        in_specs=[row, row, pl.BlockSpec((1, d), lambda i: (0, 0))],
        out_specs=[row, row],
        out_shape=[jax.ShapeDtypeStruct((m, d), F32), jax.ShapeDtypeStruct((m, d), BF16)],
        compiler_params=_cparams("parallel"),
        name="residual_norm",
    )(x, mix, g.reshape(1, d))


def _embed_norm_kernel(p_ref, w_ref, g_ref, o_ref):
    y = jnp.dot(p_ref[...], w_ref[...], preferred_element_type=F32)
    ms = jnp.mean(y * y, axis=-1, keepdims=True)
    o_ref[...] = y * lax.rsqrt(ms + RMS_EPS) * g_ref[...]


def embed_norm(p, w, g, bm=256):
    m, k = p.shape
    d = w.shape[1]
    bm = min(bm, m)
    return pl.pallas_call(
        _embed_norm_kernel,
        grid=(m // bm,),
        in_specs=[pl.BlockSpec((bm, k), lambda i: (i, 0)), pl.BlockSpec((k, d), lambda i: (0, 0)),
                  pl.BlockSpec((1, d), lambda i: (0, 0))],
        out_specs=pl.BlockSpec((bm, d), lambda i: (i, 0)),
        out_shape=jax.ShapeDtypeStruct((m, d), F32),
        compiler_params=_cparams("parallel"),
        name="embed_norm",
    )(p, w, g.reshape(1, d))


def _mm_kernel(a_ref, w_ref, *o_refs):
    acc = jnp.dot(a_ref[...], w_ref[...], preferred_element_type=F32)
    for o_ref in o_refs:
        o_ref[...] = acc.astype(o_ref.dtype)


def matmul(a, w, out_dtypes, bm=1024, bn=1024):
    m, k = a.shape
    n = w.shape[1]
    bm, bn = min(bm, m), min(bn, n)
    o_spec = pl.BlockSpec((bm, bn), lambda i, j: (i, j))
    outs = pl.pallas_call(
        _mm_kernel,
        grid=(m // bm, n // bn),
        in_specs=[pl.BlockSpec((bm, k), lambda i, j: (i, 0)), pl.BlockSpec((k, bn), lambda i, j: (0, j))],
        out_specs=[o_spec] * len(out_dtypes),
        out_shape=[jax.ShapeDtypeStruct((m, n), dt) for dt in out_dtypes],
        compiler_params=_cparams("parallel", "parallel"),
        name="matmul",
    )(a, w)
    return outs


def _mm_pair_kernel(a1_ref, a2_ref, w1_ref, w2_ref, o_ref):
    o_ref[...] = (jnp.dot(a1_ref[...], w1_ref[...], preferred_element_type=F32)
                  + jnp.dot(a2_ref[...], w2_ref[...], preferred_element_type=F32))


def matmul_pair(a1, a2, w, bm=1024, bn=1024):
    m, k = a1.shape
    n = w.shape[1]
    assert a2.shape == (m, k) and w.shape[0] == 2 * k
    bm, bn = min(bm, m), min(bn, n)
    return pl.pallas_call(
        _mm_pair_kernel,
        grid=(m // bm, n // bn),
        in_specs=[pl.BlockSpec((bm, k), lambda i, j: (i, 0)), pl.BlockSpec((bm, k), lambda i, j: (i, 0)),
                  pl.BlockSpec((k, bn), lambda i, j: (0, j)), pl.BlockSpec((k, bn), lambda i, j: (1, j))],
        out_specs=pl.BlockSpec((bm, bn), lambda i, j: (i, j)),
        out_shape=jax.ShapeDtypeStruct((m, n), F32),
        compiler_params=_cparams("parallel", "parallel"),
        name="matmul_pair",
    )(a1, a2, w, w)


def _gate_mm_kernel(a_ref, w_ref, x_ref, e_ref, o_ref):
    logits = jnp.dot(a_ref[...], w_ref[...], preferred_element_type=F32)
    o_ref[...] = x_ref[...] + e_ref[...] * _sigmoid(logits)


def gate_matmul(a, w, x, e, bm=1024, bn=512):
    m, k = a.shape
    n = w.shape[1]
    bm, bn = min(bm, m), min(bn, n)
    tile = pl.BlockSpec((bm, bn), lambda i, j: (i, j))
    return pl.pallas_call(
        _gate_mm_kernel,
        grid=(m // bm, n // bn),
        in_specs=[pl.BlockSpec((bm, k), lambda i, j: (i, 0)), pl.BlockSpec((k, bn), lambda i, j: (0, j)),
                  tile, tile],
        out_specs=tile,
        out_shape=jax.ShapeDtypeStruct((m, n), F32),
        compiler_params=_cparams("parallel", "parallel"),
        name="gate_matmul",
    )(a, w, x, e)


def _resident(block_shape, index_map):
    return pl.BlockSpec(block_shape, index_map, pipeline_mode=pl.Buffered(1))


def _qk(q, kmat):
    return lax.dot_general(q, kmat, (((1,), (1,)), ((), ())), preferred_element_type=F32)


def _softmax_tiles(chains, mxu_rowsum=True, width=None):
    width = width or len(chains)
    if width < len(chains):
        return [r for i in range(0, len(chains), width) for r in _softmax_tiles(chains[i:i + width], mxu_rowsum)]
    chains = [(fn(), vmat, m_prev, acc_prev) for fn, vmat, m_prev, acc_prev in chains]
    tk = chains[0][0].shape[1]
    m_new = [jnp.maximum(m_prev, jnp.max(s, axis=1, keepdims=True)) for s, _, m_prev, _ in chains]
    alpha = [jnp.exp2(m_prev - m) for (_, _, m_prev, _), m in zip(chains, m_new)]
    p = [jnp.exp2(s - _lane_tile(m, tk // LANES)) for (s, _, _, _), m in zip(chains, m_new)]
    if mxu_rowsum:
        ones = jnp.ones((tk, LANES), BF16)
        pv = [jnp.dot(pi.astype(BF16), jnp.concatenate([vmat, ones], axis=1), preferred_element_type=F32)
              for pi, (_, vmat, _, _) in zip(p, chains)]
    else:
        rows = [jnp.broadcast_to(jnp.sum(pi, axis=1, keepdims=True), m.shape) for pi, m in zip(p, m_new)]
        pv = [jnp.concatenate([jnp.dot(pi.astype(BF16), vmat, preferred_element_type=F32), row], axis=1)
              for pi, (_, vmat, _, _), row in zip(p, chains, rows)]
    acc = [_lane_tile(a, x.shape[1] // LANES) * acc_prev + x for a, x, (_, _, _, acc_prev) in zip(alpha, pv, chains)]
    return list(zip(m_new, acc))


def _softmax_finish(acc):
    dv = acc.shape[1] - LANES
    return acc[:, :dv] / _lane_tile(acc[:, dv:], dv // LANES)


def _stick_tiles(chains, tri, before):
    tk = chains[0][1].shape[0]
    z = [_qk(q, kmat) * ATT_SCALE for q, kmat, _, _ in chains]
    softplus = [jnp.maximum(x, 0.0) + jnp.log(1.0 + jnp.exp(-jnp.abs(x))) for x in z]
    if before is None:
        sums = [jnp.dot(sp.astype(BF16), tri, preferred_element_type=F32) for sp in softplus]
    else:
        masked = [jnp.where(before, sp, 0.0) for sp in softplus]
        hi = [x.astype(BF16) for x in masked]
        lo = [(x - h.astype(F32)).astype(BF16) for x, h in zip(masked, hi)]
        sums = [jnp.dot(jnp.concatenate([h, l], axis=1), tri, preferred_element_type=F32) for h, l in zip(hi, lo)]
    w = [jnp.exp((x - sp) + s[:, :tk] + _lane_tile(carry, tk // LANES))
         for x, sp, s, (_, _, _, carry) in zip(z, softplus, sums, chains)]
    if before is not None:
        w = [jnp.where(before, x, 0.0) for x in w]
    pv = [jnp.dot(x.astype(BF16), vmat, preferred_element_type=F32) for x, (_, _, vmat, _) in zip(w, chains)]
    return [(p, carry + s[:, tk:]) for p, s, (_, _, _, carry) in zip(pv, sums, chains)]


def _diff_finish(acc0, acc1, lam_ref, subln_ref, gate, lam_init):
    lam_q1, lam_k1, lam_q2, lam_k2 = (lam_ref[r:r + 1, :] for r in range(4))
    lam = (jnp.exp(jnp.sum(lam_q1 * lam_k1, axis=1, keepdims=True))
           - jnp.exp(jnp.sum(lam_q2 * lam_k2, axis=1, keepdims=True)) + lam_init)
    o = _softmax_finish(acc0) - lam * _softmax_finish(acc1)
    ms = jnp.mean(o * o, axis=-1, keepdims=True)
    y = o * lax.rsqrt(ms + RMS_EPS) * subln_ref[...] * (1.0 - lam_init)
    return y * _silu(gate)


def _pad_new_tile(x, tk):
    return jnp.concatenate([x, jnp.zeros((tk - x.shape[0], x.shape[1]), x.dtype)], axis=0)


def _dsa_scores_to_keys(qih_ref, wb_ref, ki_rows, visible):
    tq = wb_ref.shape[1]
    n_keys = ki_rows.shape[0]
    s_all = _qk(qih_ref[...], ki_rows)
    acc = jnp.zeros((tq, n_keys), F32)
    for h in range(H_IDX):
        acc = acc + jnp.maximum(s_all[h * tq:(h + 1) * tq], 0.0) * _lane_tile(wb_ref[h], n_keys // LANES)
    bits = pltpu.bitcast(acc, I32)
    key = bits ^ ((bits >> 31) & 0x7FFFFFFF)
    return jnp.where(visible, key, INT_MIN)


def _dsa_prepare(q_ref, qi_ref, w_ref, qs_ref, qih_ref, wb_ref, tq):
    wv = w_ref[:, D_IDX:D_IDX + H_IDX] * IDX_SCALE
    for h in range(H_IDX):
        qih_ref[h * tq:(h + 1) * tq, :] = qi_ref[:, _cols(h, D_IDX)]
        wb_ref[h] = jnp.broadcast_to(wv[:, h:h + 1], (tq, LANES))
    for n in range(KV_A):
        for g in range(GROUP_A):
            qs_ref[n, g * tq:(g + 1) * tq, :] = q_ref[:, _cols(GROUP_A * n + g)]


def _dsa_threshold(key_ref, npairs, n_sel, idx_bits, tq, tk):
    lane = lax.broadcasted_iota(I32, (tq, tk), 1)

    def count_where(pred):
        def body(p, cnt):
            k = 2 * p
            return (cnt + jnp.where(pred(k, key_ref[k]), 1.0, 0.0)
                    + jnp.where(pred(k + 1, key_ref[k + 1]), 1.0, 0.0))
        cnt = lax.fori_loop(0, npairs, body, jnp.zeros((tq, tk), F32))
        return jnp.sum(cnt, axis=1, keepdims=True)

    def count_ge(cand):
        return count_where(lambda k, key: key >= cand)

    def thr_bit(b, t):
        cand = t + lax.shift_left(jnp.int32(1), 31 - b)
        return jnp.where(count_ge(cand) >= n_sel, cand, t)

    thr = lax.fori_loop(0, 32, thr_bit, jnp.full((tq, 1), INT_MIN, I32))
    thr = jnp.maximum(thr, INT_MIN + 1)

    @pl.when(jnp.max(count_ge(thr)) > n_sel)
    def _():
        need = n_sel - count_ge(thr + 1)

        def count_eq_below(c):
            return count_where(lambda k, key: (key == thr) & ((k * tk + lane) < c))

        def idx_bit(b, c):
            cand = c + lax.shift_left(jnp.int32(1), idx_bits - 1 - b)
            return jnp.where(count_eq_below(cand) <= need, cand, c)

        cut = lax.fori_loop(0, idx_bits, idx_bit, jnp.zeros((tq, 1), I32))

        def demote(k, carry):
            key = key_ref[k]
            key_ref[k] = jnp.where((key == thr) & ((k * tk + lane) >= cut), key - 1, key)
            return carry

        lax.fori_loop(0, 2 * npairs, demote, 0)

    return thr


def _dsa_attend_tiles(qs_ref, key_tiles, thr, bias_tiles, kv_of_head, m_ref, acc_ref, tq):
    rows4 = GROUP_A * tq
    madd = jnp.concatenate([jnp.where(key_tile >= thr, 0.0, NEG) for key_tile in key_tiles], axis=1)
    madd4 = jnp.concatenate([madd] * GROUP_A, axis=0)
    loaded = [(qs_ref[n], *kv_of_head(n), m_ref[n], acc_ref[n]) for n in range(KV_A)]
    def score_fn(n, q, kmat):
        def fn():
            bias = jnp.concatenate([b[rows4 * n:rows4 * (n + 1), :] for b in bias_tiles], axis=1)
            return _qk(q, kmat) * QK_SCALE_BITS + bias + madd4
        return fn

    new = _softmax_tiles([(score_fn(n, q, kmat), vmat, m_prev, acc_prev)
                          for n, (q, kmat, vmat, m_prev, acc_prev) in enumerate(loaded)], width=DSA_CHAINS_ABREAST)
    for n, (m_new, acc_new) in enumerate(new):
        m_ref[n], acc_ref[n] = m_new, acc_new


def _dsa_finish(acc_ref, gate_ref, o_ref, tq):
    for n in range(KV_A):
        o = _softmax_finish(acc_ref[n])
        for g in range(GROUP_A):
            cs = _cols(GROUP_A * n + g)
            o_ref[:, cs] = (o[g * tq:(g + 1) * tq] * _silu(gate_ref[:, cs])).astype(o_ref.dtype)


def _dsa_scratch(nk, tq, tk):
    rows4 = GROUP_A * tq
    return [
        pltpu.VMEM((KV_A, rows4, HEAD_DIM), BF16),
        pltpu.VMEM((H_IDX * tq, D_IDX), BF16),
        pltpu.VMEM((H_IDX, tq, LANES), F32),
        pltpu.VMEM((nk, tq, tk), I32),
        pltpu.VMEM((KV_A, rows4, LANES), F32),
        pltpu.VMEM((KV_A, rows4, HEAD_DIM + LANES), F32),
    ]


def _dsa_prompt_kernel(q_ref, qi_ref, w_ref, ki_ref, kv_ref, gate_ref, bias_ref, o_ref,
                       qs_ref, qih_ref, wb_ref, key_ref, m_ref, acc_ref, *, tq, tk, group, n_sel, idx_bits):
    i = pl.program_id(1)
    npairs = i // 2 + 1
    row_chunk = (i * tq + lax.broadcasted_iota(I32, (tq, 2 * tk), 0)) >> CHUNK_SHIFT
    lane = lax.broadcasted_iota(I32, (tq, 2 * tk), 1)
    _dsa_prepare(q_ref, qi_ref, w_ref, qs_ref, qih_ref, wb_ref, tq)

    def score_pair(p, carry):
        ks = pl.ds(pl.multiple_of(p * 2 * tk, 2 * tk), 2 * tk)
        visible = ((p * 2 * tk + lane) >> CHUNK_SHIFT) <= row_chunk
        keys = _dsa_scores_to_keys(qih_ref, wb_ref, ki_ref[ks, :], visible)
        key_ref[2 * p], key_ref[2 * p + 1] = keys[:, :tk], keys[:, tk:]
        return carry

    lax.fori_loop(0, npairs, score_pair, 0)
    for g in range(group - 2):
        key_ref[2 * npairs + g] = jnp.full((tq, tk), INT_MIN, I32)
    thr = _dsa_threshold(key_ref, npairs, n_sel, idx_bits, tq, tk)

    m_ref[...] = jnp.full(m_ref.shape, NEG, F32)
    acc_ref[...] = jnp.zeros(acc_ref.shape, F32)

    def attend_group(j, carry):
        k0 = j * group
        ks = pl.ds(pl.multiple_of(k0 * tk, group * tk), group * tk)
        key_tiles = [key_ref[k0 + g] for g in range(group)]
        bias_tiles = [bias_ref[jnp.clip(i - k0 - g, 0, N_BIAS_VARIANTS - 1)] for g in range(group)]
        kv_of_head = lambda n: (kv_ref[ks, _cols(n)], kv_ref[ks, KVA + HEAD_DIM * n:KVA + HEAD_DIM * (n + 1)])
        _dsa_attend_tiles(qs_ref, key_tiles, thr, bias_tiles, kv_of_head, m_ref, acc_ref, tq)
        return carry

    lax.fori_loop(0, i // group + 1, attend_group, 0)
    _dsa_finish(acc_ref, gate_ref, o_ref, tq)


def dsa_prompt(qcat, kiw, ki, kv, gcat, bias, *, n_sel):
    b, t, _ = qcat.shape
    tq = tk = LANES
    group = DSA_TILES_PER_STEP
    nk = t // tk
    assert nk % group == 0 and group % 2 == 0
    kern = functools.partial(_dsa_prompt_kernel, tq=tq, tk=tk, group=group, n_sel=n_sel,
                             idx_bits=int(t).bit_length())
    return pl.pallas_call(
        kern,
        grid=(b, t // tq),
        in_specs=[
            pl.BlockSpec((None, tq, QA), lambda bi, i: (bi, i, 0)),
            pl.BlockSpec((None, tq, QI), lambda bi, i: (bi, i, 2 * QA // QI)),
            pl.BlockSpec((None, tq, LANES), lambda bi, i: (bi, i, 0)),
            _resident((None, t, D_IDX), lambda bi, i: (bi, 0, 0)),
            _resident((None, t, 2 * KVA), lambda bi, i: (bi, 0, 0)),
            pl.BlockSpec((None, tq, QA), lambda bi, i: (bi, i, 0)),
            _resident((N_BIAS_VARIANTS, H_A * tq, tk), lambda bi, i: (0, 0, 0)),
        ],
        out_specs=pl.BlockSpec((None, tq, QA), lambda bi, i: (bi, i, 0)),
        out_shape=jax.ShapeDtypeStruct((b, t, QA), BF16),
        scratch_shapes=_dsa_scratch(nk + group, tq, tk),
        compiler_params=_cparams("parallel", "arbitrary"),
        name="dsa_prompt",
    )(qcat, qcat, kiw, ki, kv, gcat, bias)


def _dsa_sample_kernel(q_ref, qi_ref, w_ref, kin_ref, kvn_ref, gate_ref, bias_ref, kic_ref, kvc_ref, o_ref,
                       qs_ref, qih_ref, wb_ref, key_ref, m_ref, acc_ref, thr_ref, *, tq, p0, pt, n_sel, idx_bits):
    kt = pl.program_id(1)
    tk = TK_SAMPLE
    n_cache = p0 // tk
    sub = pt // tk

    @pl.when(kt == 0)
    def _():
        new_visible = lax.broadcasted_iota(I32, (tq, tk), 1) < tq
        _dsa_prepare(q_ref, qi_ref, w_ref, qs_ref, qih_ref, wb_ref, tq)

        def score_pair(p, carry):
            ks = pl.ds(pl.multiple_of(p * 2 * tk, 2 * tk), 2 * tk)
            keys = _dsa_scores_to_keys(qih_ref, wb_ref, kic_ref[ks, :].astype(BF16), True)
            key_ref[2 * p], key_ref[2 * p + 1] = keys[:, :tk], keys[:, tk:]
            return carry

        lax.fori_loop(0, n_cache // 2, score_pair, 0)
        ki_new = _pad_new_tile(kin_ref[:, :D_IDX].astype(BF16), tk)
        key_ref[n_cache] = _dsa_scores_to_keys(qih_ref, wb_ref, ki_new, new_visible)
        key_ref[n_cache + 1] = jnp.full((tq, tk), INT_MIN, I32)
        thr = _dsa_threshold(key_ref, n_cache // 2 + 1, n_sel, idx_bits, tq, tk)
        thr_ref[...] = jnp.broadcast_to(thr, thr_ref.shape)
        m_ref[...] = jnp.full(m_ref.shape, NEG, F32)
        acc_ref[...] = jnp.zeros(acc_ref.shape, F32)

    thr = thr_ref[...]
    k0 = kt * sub
    key_tiles = [key_ref[k0 + j] for j in range(sub)]
    bias_tiles = [bias_ref[jnp.minimum(n_cache - k0 - j, N_BIAS_VARIANTS - 1)] for j in range(sub)]
    kv_cached = lambda n: (kvc_ref[:, 0, n, :].astype(BF16), kvc_ref[:, 1, n, :].astype(BF16))
    _dsa_attend_tiles(qs_ref, key_tiles, thr, bias_tiles, kv_cached, m_ref, acc_ref, tq)

    @pl.when(kt == pl.num_programs(1) - 1)
    def _():
        kv_new = lambda n: (_pad_new_tile(kvn_ref[:, _cols(n)], tk),
                            _pad_new_tile(kvn_ref[:, KVA + HEAD_DIM * n:KVA + HEAD_DIM * (n + 1)], tk))
        _dsa_attend_tiles(qs_ref, [key_ref[n_cache]], thr, [bias_ref[0]], kv_new, m_ref, acc_ref, tq)
        _dsa_finish(acc_ref, gate_ref, o_ref, tq)


def dsa_sample(qcat, kiw, kv_new, gcat, bias, cache_kidx, cache_kv, layer, *, n_sel):
    b, tq, _ = qcat.shape
    p0 = cache_kv.shape[2]
    tk, pt = TK_SAMPLE, PT_SAMPLE
    assert p0 % pt == 0 and tq <= tk and (p0 // tk) % 2 == 0
    nk = p0 // tk + 2
    kern = functools.partial(_dsa_sample_kernel, tq=tq, p0=p0, pt=pt, n_sel=n_sel,
                             idx_bits=int(p0 + tk).bit_length())
    return pl.pallas_call(
        kern,
        grid=(b, p0 // pt),
        in_specs=[
            pl.BlockSpec((None, tq, QA), lambda bi, kt: (bi, 0, 0)),
            pl.BlockSpec((None, tq, QI), lambda bi, kt: (bi, 0, 2 * QA // QI)),
            pl.BlockSpec((None, tq, LANES), lambda bi, kt: (bi, 0, 0)),
            pl.BlockSpec((None, tq, LANES), lambda bi, kt: (bi, 0, 0)),
            pl.BlockSpec((None, tq, 2 * KVA), lambda bi, kt: (bi, 0, 0)),
            pl.BlockSpec((None, tq, QA), lambda bi, kt: (bi, 0, 0)),
            pl.BlockSpec((N_BIAS_VARIANTS, H_A * tq, tk), lambda bi, kt: (0, 0, 0)),
            pl.BlockSpec((None, None, p0, D_IDX), lambda bi, kt: (layer, bi, 0, 0)),
            pl.BlockSpec((None, None, pt, 2, KV_A, HEAD_DIM), lambda bi, kt: (layer, bi, kt, 0, 0, 0)),
        ],
        out_specs=pl.BlockSpec((None, tq, QA), lambda bi, kt: (bi, 0, 0)),
        out_shape=jax.ShapeDtypeStruct((b, tq, QA), BF16),
        scratch_shapes=_dsa_scratch(nk, tq, tk) + [pltpu.VMEM((tq, tk), I32)],
        compiler_params=_cparams("parallel", "arbitrary"),
        name="dsa_sample",
    )(qcat, qcat, kiw, kiw, kv_new, gcat, bias, cache_kidx, cache_kv)


def _diff_prompt_kernel(q_ref, k_ref, v_ref, gate_ref, bias_ref, lam_ref, subln_ref, o_ref,
                        m_ref, acc_ref, *, tq, tk, hb, lam_init):
    i = pl.program_id(2)
    row_chunk = (i * tq + lax.broadcasted_iota(I32, (tq, tk), 0)) >> CHUNK_SHIFT
    lane = lax.broadcasted_iota(I32, (tq, tk), 1)
    m_ref[...] = jnp.full(m_ref.shape, NEG, F32)
    acc_ref[...] = jnp.zeros(acc_ref.shape, F32)

    def step(k, masked):
        ks = pl.ds(pl.multiple_of(k * tk, tk), tk)
        variant = jnp.minimum(i - k, N_BIAS_VARIANTS - 1)
        madd = jnp.where(((k * tk + lane) >> CHUNK_SHIFT) <= row_chunk, 0.0, NEG) if masked else None
        loaded = []
        for h in range(hb):
            bias = bias_ref[variant, h]
            if masked:
                bias = bias + madd
            vmat = v_ref[ks, _cols(h, DV_B)]
            for c in range(2):
                col = slice(DV_B * h + HEAD_DIM * c, DV_B * h + HEAD_DIM * (c + 1))
                loaded.append((h, c, q_ref[:, col], k_ref[ks, col], vmat, bias, m_ref[h, c], acc_ref[h, c]))
        score_fn = lambda q, kmat, bias: lambda: _qk(q, kmat) * QK_SCALE_BITS + bias
        new = _softmax_tiles([(score_fn(q, kmat, bias), vmat, m_prev, acc_prev)
                              for _, _, q, kmat, vmat, bias, m_prev, acc_prev in loaded],
                             mxu_rowsum=False, width=DIFF_CHAINS_ABREAST)
        for (h, c, *_), (m_new, acc_new) in zip(loaded, new):
            m_ref[h, c], acc_ref[h, c] = m_new, acc_new

    def full_tile(k, carry):
        step(k, False)
        return carry

    lax.fori_loop(0, i, full_tile, 0)
    step(i, True)
    for h in range(hb):
        cs = _cols(h, DV_B)
        o_ref[:, cs] = _diff_finish(acc_ref[h, 0], acc_ref[h, 1], lam_ref, subln_ref, gate_ref[:, cs],
                                    lam_init).astype(o_ref.dtype)


def diff_prompt(qcat, kv, gcat, bias, lam_vec, subln, *, tq, hb, lam_init):
    b, t, _ = qcat.shape
    tk = tq
    w = hb * DV_B
    kern = functools.partial(_diff_prompt_kernel, tq=tq, tk=tk, hb=hb, lam_init=lam_init)
    return pl.pallas_call(
        kern,
        grid=(b, H_B // hb, t // tq),
        in_specs=[
            pl.BlockSpec((None, tq, w), lambda bi, g, i: (bi, i, QA // w + g)),
            _resident((None, t, w), lambda bi, g, i: (bi, 0, g)),
            _resident((None, t, w), lambda bi, g, i: (bi, 0, QA // w + g)),
            pl.BlockSpec((None, tq, w), lambda bi, g, i: (bi, i, QA // w + g)),
            _resident((N_BIAS_VARIANTS, hb, tq, tk), lambda bi, g, i: (0, g, 0, 0)),
            pl.BlockSpec((4, HEAD_DIM), lambda bi, g, i: (0, 0)),
            pl.BlockSpec((1, DV_B), lambda bi, g, i: (0, 0)),
        ],
        out_specs=pl.BlockSpec((None, tq, w), lambda bi, g, i: (bi, i, g)),
        out_shape=jax.ShapeDtypeStruct((b, t, QA), BF16),
        scratch_shapes=[pltpu.VMEM((hb, 2, tq, LANES), F32), pltpu.VMEM((hb, 2, tq, DV_B + LANES), F32)],
        compiler_params=_cparams("parallel", "parallel", "arbitrary"),
        name="diff_prompt",
    )(qcat, kv, kv, gcat, bias, lam_vec, subln.reshape(1, DV_B))


def _cache_rows_2d(cache):
    n, b, p, two, h, d = cache.shape
    assert two == 2 and h % 8 == 0
    return cache.reshape(n, b, p * two * h, d)


def _head_rows(c_ref, key0, n_keys, n_heads, sel, h):
    period = 2 * n_heads
    return c_ref[pl.ds(key0 * period + sel * n_heads + h, n_keys, stride=period), :]


def _diff_sample_kernel(q_ref, kn_ref, vn_ref, gate_ref, bias_ref, lam_ref, subln_ref, c0_ref, c1_ref, o_ref,
                        m_ref, acc_ref, *, tq, p0, pt, lam_init):
    kt = pl.program_id(1)
    tk = TK_SAMPLE
    n_cache = p0 // tk
    sub = pt // tk

    @pl.when(kt == 0)
    def _():
        m_ref[...] = jnp.full(m_ref.shape, NEG, F32)
        acc_ref[...] = jnp.zeros(acc_ref.shape, F32)

    def update_all(tile_of_head):
        loaded = []
        for h in range(H_B):
            kmat2, vmat, bias = tile_of_head(h)
            for c in range(2):
                q = q_ref[:, DV_B * h + HEAD_DIM * c:DV_B * h + HEAD_DIM * (c + 1)]
                loaded.append((h, c, q, kmat2[:, _cols(c)], vmat, bias, m_ref[h, c], acc_ref[h, c]))
        score_fn = lambda q, kmat, bias: lambda: _qk(q, kmat) * QK_SCALE_BITS + bias
        new = _softmax_tiles([(score_fn(q, kmat, bias), vmat, m_prev, acc_prev)
                              for _, _, q, kmat, vmat, bias, m_prev, acc_prev in loaded], mxu_rowsum=False)
        for (h, c, *_), (m_new, acc_new) in zip(loaded, new):
            m_ref[h, c], acc_ref[h, c] = m_new, acc_new

    def cache_tile(j, carry):
        variant = jnp.minimum(n_cache - (kt * sub + j), N_BIAS_VARIANTS - 1)
        halves = lambda sel, h: jnp.concatenate(
            [_head_rows(c_ref, j * tk, tk, H_B, sel, h).astype(BF16) for c_ref in (c0_ref, c1_ref)], axis=1)
        update_all(lambda h: (halves(0, h), halves(1, h), bias_ref[variant, h]))
        return carry

    lax.fori_loop(0, sub, cache_tile, 0)

    @pl.when(kt == pl.num_programs(1) - 1)
    def _():
        new_madd = jnp.where(lax.broadcasted_iota(I32, (tq, tk), 1) < tq, 0.0, NEG)
        update_all(lambda h: (_pad_new_tile(kn_ref[:, _cols(h, DV_B)], tk), _pad_new_tile(vn_ref[:, _cols(h, DV_B)], tk),
                              bias_ref[0, h] + new_madd))
        for h in range(H_B):
            cs = _cols(h, DV_B)
            o_ref[:, cs] = _diff_finish(acc_ref[h, 0], acc_ref[h, 1], lam_ref, subln_ref, gate_ref[:, cs],
                                        lam_init).astype(o_ref.dtype)


def diff_sample(qcat, kv_new, gcat, bias, lam_vec, subln, cache_kv, layer, *, lam_init):
    b, tq, _ = qcat.shape
    p0 = cache_kv.shape[2]
    tk, pt = TK_SAMPLE, PT_SAMPLE
    assert p0 % pt == 0 and tq <= tk
    kern = functools.partial(_diff_sample_kernel, tq=tq, p0=p0, pt=pt, lam_init=lam_init)
    cache_spec = lambda c: pl.BlockSpec((None, None, pt * 2 * H_B, LANES), lambda bi, kt: (layer, bi, kt, c))
    cache_2d = _cache_rows_2d(cache_kv)
    return pl.pallas_call(
        kern,
        grid=(b, p0 // pt),
        in_specs=[
            pl.BlockSpec((None, tq, QA), lambda bi, kt: (bi, 0, 1)),
            pl.BlockSpec((None, tq, QA), lambda bi, kt: (bi, 0, 0)),
            pl.BlockSpec((None, tq, QA), lambda bi, kt: (bi, 0, 1)),
            pl.BlockSpec((None, tq, QA), lambda bi, kt: (bi, 0, 1)),
            pl.BlockSpec((N_BIAS_VARIANTS, H_B, tq, tk), lambda bi, kt: (0, 0, 0, 0)),
            pl.BlockSpec((4, HEAD_DIM), lambda bi, kt: (0, 0)),
            pl.BlockSpec((1, DV_B), lambda bi, kt: (0, 0)),
            cache_spec(0), cache_spec(1),
        ],
        out_specs=pl.BlockSpec((None, tq, QA), lambda bi, kt: (bi, 0, 0)),
        out_shape=jax.ShapeDtypeStruct((b, tq, QA), BF16),
        scratch_shapes=[pltpu.VMEM((H_B, 2, tq, LANES), F32), pltpu.VMEM((H_B, 2, tq, DV_B + LANES), F32)],
        compiler_params=_cparams("parallel", "arbitrary"),
        name="diff_sample",
    )(qcat, kv_new, kv_new, gcat, bias, lam_vec, subln.reshape(1, DV_B), cache_2d, cache_2d)


def _stick_prompt_kernel(q_ref, k_ref, v_ref, gate_ref, tri_ref, o_ref, acc_ref, carry_ref, *, tq, tk, hb):
    i = pl.program_id(2)
    before = lax.broadcasted_iota(I32, (tq, tk), 1) < lax.broadcasted_iota(I32, (tq, tk), 0)
    acc_ref[...] = jnp.zeros(acc_ref.shape, F32)
    carry_ref[...] = jnp.zeros(carry_ref.shape, F32)

    def step(k, mask):
        ks = pl.ds(pl.multiple_of(k * tk, tk), tk)
        tri = tri_ref[:tk, :] if mask is None else tri_ref[...]
        loaded = [(q_ref[:, _cols(h)], k_ref[ks, _cols(h)], v_ref[ks, _cols(h)], carry_ref[h], acc_ref[h])
                  for h in range(hb)]
        new = _stick_tiles([(q, kmat, vmat, carry_prev) for q, kmat, vmat, carry_prev, _ in loaded], tri, mask)
        for h, ((pv, carry), (_, _, _, _, acc)) in enumerate(zip(new, loaded)):
            acc_ref[h], carry_ref[h] = acc + pv, carry

    step(i, before)

    def full_tile(j, carry):
        step(i - 1 - j, None)
        return carry

    lax.fori_loop(0, i, full_tile, 0)
    for h in range(hb):
        cs = _cols(h)
        o_ref[:, cs] = (acc_ref[h] * _silu(gate_ref[:, cs])).astype(o_ref.dtype)


def stick_prompt(qcat, kv, gcat, tri_ext, *, tq, hb):
    b, t, _ = qcat.shape
    tk = tq
    w = hb * HEAD_DIM
    kern = functools.partial(_stick_prompt_kernel, tq=tq, tk=tk, hb=hb)
    return pl.pallas_call(
        kern,
        grid=(b, H_C // hb, t // tq),
        in_specs=[
            pl.BlockSpec((None, tq, w), lambda bi, g, i: (bi, i, g)),
            _resident((None, t, w), lambda bi, g, i: (bi, 0, g)),
            _resident((None, t, w), lambda bi, g, i: (bi, 0, QA // w + g)),
            pl.BlockSpec((None, tq, w), lambda bi, g, i: (bi, i, g)),
            _resident((2 * tk, tk + LANES), lambda bi, g, i: (0, 0)),
        ],
        out_specs=pl.BlockSpec((None, tq, w), lambda bi, g, i: (bi, i, g)),
        out_shape=jax.ShapeDtypeStruct((b, t, QA), BF16),
        scratch_shapes=[pltpu.VMEM((hb, tq, HEAD_DIM), F32), pltpu.VMEM((hb, tq, LANES), F32)],
        compiler_params=_cparams("parallel", "parallel", "arbitrary"),
        name="stick_prompt",
    )(qcat, kv, kv, gcat, tri_ext)


def _stick_sample_kernel(q_ref, kn_ref, vn_ref, gate_ref, tri_ref, c_ref, o_ref, acc_ref, carry_ref, *, tq, pt):
    kt = pl.program_id(1)
    tk = TK_SAMPLE
    sub = pt // tk
    head_groups = [range(g * HB_SAMPLE, (g + 1) * HB_SAMPLE) for g in range(H_C // HB_SAMPLE)]

    @pl.when(kt == 0)
    def _():
        before = lax.broadcasted_iota(I32, (tq, tk), 1) < lax.broadcasted_iota(I32, (tq, tk), 0)
        tri = tri_ref[...]
        for heads in head_groups:
            new = _stick_tiles([(q_ref[:, _cols(h)], _pad_new_tile(kn_ref[:, _cols(h)], tk),
                                 _pad_new_tile(vn_ref[:, _cols(h)], tk), jnp.zeros((tq, LANES), F32))
                                for h in heads], tri, before)
            for h, (pv, carry) in zip(heads, new):
                acc_ref[h], carry_ref[h] = pv, carry

    def cache_tile(j, carry):
        key0 = (sub - 1 - j) * tk
        tri = tri_ref[:tk, :]
        for heads in head_groups:
            loaded = [(q_ref[:, _cols(h)], _head_rows(c_ref, key0, tk, H_C, 0, h).astype(BF16),
                       _head_rows(c_ref, key0, tk, H_C, 1, h).astype(BF16), carry_ref[h], acc_ref[h]) for h in heads]
            new = _stick_tiles([(q, kmat, vmat, c_prev) for q, kmat, vmat, c_prev, _ in loaded], tri, None)
            for h, (pv, c), (_, _, _, _, acc) in zip(heads, new, loaded):
                acc_ref[h], carry_ref[h] = acc + pv, c
        return carry

    lax.fori_loop(0, sub, cache_tile, 0)

    @pl.when(kt == pl.num_programs(1) - 1)
    def _():
        for h in range(H_C):
            cs = _cols(h)
            o_ref[:, cs] = (acc_ref[h] * _silu(gate_ref[:, cs])).astype(o_ref.dtype)


def stick_sample(qcat, kv_new, gcat, tri_ext, cache_kv, layer):
    b, tq, _ = qcat.shape
    p0 = cache_kv.shape[2]
    pt = PT_SAMPLE
    assert p0 % pt == 0 and tq <= TK_SAMPLE
    n_kt = p0 // pt
    kern = functools.partial(_stick_sample_kernel, tq=tq, pt=pt)
    return pl.pallas_call(
        kern,
        grid=(b, n_kt),
        in_specs=[
            pl.BlockSpec((None, tq, QA), lambda bi, kt: (bi, 0, 0)),
            pl.BlockSpec((None, tq, QA), lambda bi, kt: (bi, 0, 0)),
            pl.BlockSpec((None, tq, QA), lambda bi, kt: (bi, 0, 1)),
            pl.BlockSpec((None, tq, QA), lambda bi, kt: (bi, 0, 0)),
            pl.BlockSpec((2 * TK_SAMPLE, TK_SAMPLE + LANES), lambda bi, kt: (0, 0)),
            pl.BlockSpec((None, None, pt * 2 * H_C, HEAD_DIM), lambda bi, kt: (layer, bi, n_kt - 1 - kt, 0)),
        ],
        out_specs=pl.BlockSpec((None, tq, QA), lambda bi, kt: (bi, 0, 0)),
        out_shape=jax.ShapeDtypeStruct((b, tq, QA), BF16),
        scratch_shapes=[pltpu.VMEM((H_C, tq, HEAD_DIM), F32), pltpu.VMEM((H_C, tq, LANES), F32)],
        compiler_params=_cparams("parallel", "arbitrary"),
        name="stick_sample",
    )(qcat, kv_new, kv_new, gcat, tri_ext, _cache_rows_2d(cache_kv))


def _band_prompt_kernel(q_ref, k_ref, v_ref, gate_ref, bias_ref, o_ref, *, tq, win, hb):
    i = pl.program_id(2)
    span = win + tq
    rows = pl.ds(pl.multiple_of(i * tq, tq), span)
    row_chunk = (i * tq + lax.broadcasted_iota(I32, (tq, span), 0)) >> CHUNK_SHIFT
    key_chunk = (i * tq - win + lax.broadcasted_iota(I32, (tq, span), 1)) >> CHUNK_SHIFT
    visible = (key_chunk <= row_chunk) & (key_chunk >= jnp.maximum(row_chunk - BAND_CHUNKS, 0))
    madd = jnp.where(visible, 0.0, NEG)
    ones = jnp.ones((span, LANES), BF16)
    for h in range(hb):
        cs = _cols(h)
        s = _qk(q_ref[:, cs], k_ref[rows, cs]) * QK_SCALE_BITS + bias_ref[h] + madd
        p = jnp.exp2(s - jnp.max(s, axis=1, keepdims=True))
        pv = jnp.dot(p.astype(BF16), jnp.concatenate([v_ref[rows, cs], ones], axis=1), preferred_element_type=F32)
        o_ref[:, cs] = (_softmax_finish(pv) * _silu(gate_ref[:, cs])).astype(o_ref.dtype)


def band_prompt(qcat, kv, gcat, bias, *, hb):
    b, t, _ = qcat.shape
    tq = LANES
    win = BAND_CHUNKS * CHUNK
    w = hb * HEAD_DIM
    assert kv.shape[1] == win + t and bias.shape == (H_D, tq, win + tq)
    kern = functools.partial(_band_prompt_kernel, tq=tq, win=win, hb=hb)
    return pl.pallas_call(
        kern,
        grid=(b, H_D // hb, t // tq),
        in_specs=[
            pl.BlockSpec((None, tq, w), lambda bi, g, i: (bi, i, QA // w + g)),
            _resident((None, win + t, w), lambda bi, g, i: (bi, 0, g)),
            _resident((None, win + t, w), lambda bi, g, i: (bi, 0, QA // w + g)),
            pl.BlockSpec((None, tq, w), lambda bi, g, i: (bi, i, QA // w + g)),
            _resident((hb, tq, win + tq), lambda bi, g, i: (g, 0, 0)),
        ],
        out_specs=pl.BlockSpec((None, tq, w), lambda bi, g, i: (bi, i, g)),
        out_shape=jax.ShapeDtypeStruct((b, t, QA), BF16),
        compiler_params=_cparams("parallel", "parallel", "arbitrary"),
        name="band_prompt",
    )(qcat, kv, kv, gcat, bias)


def _band_sample_kernel(q_ref, kn_ref, vn_ref, gate_ref, bias_ref, c_ref, o_ref, *, tq, win):
    g = pl.program_id(1)
    tk = TK_SAMPLE
    lane = lax.broadcasted_iota(I32, (tq, win + tk), 1)
    madd = jnp.where(lane < win + tq, 0.0, NEG)
    ones = jnp.ones((win + tk, LANES), BF16)
    for h in range(HB_SAMPLE):
        cs = _cols(h)
        head = g * HB_SAMPLE + h
        kmat = jnp.concatenate([_head_rows(c_ref, 0, win, H_D, 0, head).astype(BF16),
                                _pad_new_tile(kn_ref[:, cs], tk)], axis=0)
        vmat = jnp.concatenate([_head_rows(c_ref, 0, win, H_D, 1, head).astype(BF16),
                                _pad_new_tile(vn_ref[:, cs], tk)], axis=0)
        s = _qk(q_ref[:, cs], kmat) * QK_SCALE_BITS + bias_ref[h] + madd
        p = jnp.exp2(s - jnp.max(s, axis=1, keepdims=True))
        pv = jnp.dot(p.astype(BF16), jnp.concatenate([vmat, ones], axis=1), preferred_element_type=F32)
        o_ref[:, cs] = (_softmax_finish(pv) * _silu(gate_ref[:, cs])).astype(o_ref.dtype)


def band_sample(qcat, kv_new, gcat, bias, cache_kv, layer):
    b, tq, _ = qcat.shape
    win = cache_kv.shape[2]
    w = HB_SAMPLE * HEAD_DIM
    assert win == BAND_CHUNKS * CHUNK and tq == CHUNK and bias.shape == (H_D, tq, win + TK_SAMPLE)
    kern = functools.partial(_band_sample_kernel, tq=tq, win=win)
    return pl.pallas_call(
        kern,
        grid=(b, H_D // HB_SAMPLE),
        in_specs=[
            pl.BlockSpec((None, tq, w), lambda bi, g: (bi, 0, QA // w + g)),
            pl.BlockSpec((None, tq, w), lambda bi, g: (bi, 0, g)),
            pl.BlockSpec((None, tq, w), lambda bi, g: (bi, 0, QA // w + g)),
            pl.BlockSpec((None, tq, w), lambda bi, g: (bi, 0, QA // w + g)),
            pl.BlockSpec((HB_SAMPLE, tq, win + TK_SAMPLE), lambda bi, g: (g, 0, 0)),
            pl.BlockSpec((None, None, win * 2 * H_D, HEAD_DIM), lambda bi, g: (layer, bi, 0, 0)),
        ],
        out_specs=pl.BlockSpec((None, tq, w), lambda bi, g: (bi, 0, g)),
        out_shape=jax.ShapeDtypeStruct((b, tq, QA), BF16),
        compiler_params=_cparams("parallel", "arbitrary"),
        name="band_sample",
    )(qcat, kv_new, kv_new, gcat, bias, _cache_rows_2d(cache_kv))


def _t5_bucket_np(rel):
    half = T5_BUCKETS // 2
    max_exact = half // 2
    n = np.abs(rel)
    nf = np.maximum(n, 1).astype(np.float64)
    large = max_exact + (np.log(nf / max_exact) / math.log(T5_MAX_DIST / max_exact)
                         * (half - max_exact)).astype(np.int32)
    large = np.minimum(large, half - 1)
    return np.where(rel < 0, half, 0) + np.where(n < max_exact, n, large)


def _band_index_np(rel):
    return np.clip(rel, -REL_CLIP, REL_CLIP) + REL_CLIP


def toeplitz_bias(table, index_of_rel, r0, n_rows, n_cols):
    rel = r0 + np.arange(n_rows)[:, None] - np.arange(n_cols)[None, :]
    idx = jnp.asarray(index_of_rel(rel).reshape(-1).astype(np.int32))
    onehot = (idx[None, :] == jnp.arange(table.shape[0], dtype=I32)[:, None]).astype(F32)
    out = jnp.einsum("bh,bn->hn", table, onehot, precision=lax.Precision.HIGHEST)
    return out.reshape(table.shape[1], n_rows, n_cols)


def bias_tiles(table, index_of_rel, tq, tk):
    far = (N_BIAS_VARIANTS - 1) * tk
    assert (index_of_rel(np.arange(far - tk + 1, far + tq)) == index_of_rel(np.array(far + tq))).all()
    return jnp.stack([toeplitz_bias(table, index_of_rel, v * tk, tq, tk) for v in range(N_BIAS_VARIANTS)], axis=0)


def band_window_bias(rel_tab, tq, win):
    far_cols = win - REL_CLIP
    return jnp.concatenate(
        [jnp.broadcast_to(rel_tab[2 * REL_CLIP][:, None, None], (H_D, tq, far_cols)),
         toeplitz_bias(rel_tab, _band_index_np, REL_CLIP, tq, win + LANES - far_cols)], axis=-1)


def _tri_ext(tk):
    tri = -np.tril(np.ones((tk, tk), np.float32), -1)
    ext = np.concatenate([tri, -np.ones((tk, LANES), np.float32)], axis=1)
    return jnp.asarray(np.concatenate([ext, ext], axis=0), BF16)


def _even_weights(w_in):
    c = np.cumsum((0, QA, KVA, KVA, QI, D_IDX, H_IDX, QA, QA, QA, QA, QA)).tolist()
    aq, ak, av, aqi, aki, aw, ag, bq, bk, bv, bg = (w_in[:, c[r]:c[r + 1]] for r in range(11))
    pad = jnp.zeros((w_in.shape[0], LANES - D_IDX - H_IDX), w_in.dtype)
    cat = lambda *xs: jnp.concatenate(xs, axis=1).astype(BF16)
    return cat(aq, bq, aqi), cat(ak, av), cat(bk, bv), cat(aki, aw, pad), cat(ag, bg)


def _odd_weights(w_in):
    cq, ck, cv, cg, dq, dk, dv, dg = (w_in[:, QA * r:QA * (r + 1)] for r in range(8))
    cat = lambda *xs: jnp.concatenate(xs, axis=1).astype(BF16)
    return cat(cq, dq), cat(ck, cv), cat(dk, dv), cat(cg, dg)


def _even_layer(h, caches, layer, weights, t5_tab, lam_vec, subln, lam_init):
    w_q, w_akv, w_bkv, w_kiw, w_g = weights
    b, t, d = h.shape
    h2 = h.reshape(b * t, d)
    (qcat,) = matmul(h2, w_q, (BF16,))
    akv, akv_b = matmul(h2, w_akv, (F32, BF16))
    bkv, bkv_b = matmul(h2, w_bkv, (F32, BF16))
    (kiw,) = matmul(h2, w_kiw, (F32,))
    (gcat,) = matmul(h2, w_g, (F32,))
    qcat, akv_b, bkv_b, kiw, gcat = (x.reshape(b, t, -1) for x in (qcat, akv_b, bkv_b, kiw, gcat))
    aki = kiw[..., :D_IDX]
    t5_a, t5_b = t5_tab[:, :H_A] * LOG2E, t5_tab[:, H_A:] * LOG2E
    if caches is None:
        tq_b = 256
        bias_a = bias_tiles(t5_a, _t5_bucket_np, LANES, LANES).reshape(N_BIAS_VARIANTS, H_A * LANES, LANES)
        o_a = dsa_prompt(qcat, kiw, aki.astype(BF16), akv_b, gcat, bias_a, n_sel=min(TOPK_MAX, t // 4))
        o_b = diff_prompt(qcat, bkv_b, gcat, bias_tiles(t5_b, _t5_bucket_np, tq_b, tq_b), lam_vec, subln,
                          tq=tq_b, hb=4, lam_init=lam_init)
    else:
        cache_a, cache_ki, cache_b = caches
        p0 = cache_a.shape[2]
        bias_a = bias_tiles(t5_a, _t5_bucket_np, t, TK_SAMPLE).reshape(N_BIAS_VARIANTS, H_A * t, TK_SAMPLE)
        o_a = dsa_sample(qcat, kiw, akv_b, gcat, bias_a, cache_ki, cache_a, layer,
                         n_sel=min(TOPK_MAX, (p0 + t) // 4))
        o_b = diff_sample(qcat, bkv_b, gcat, bias_tiles(t5_b, _t5_bucket_np, t, TK_SAMPLE), lam_vec, subln,
                          cache_b, layer, lam_init=lam_init)
    mixed = (o_a.reshape(b * t, QA), o_b.reshape(b * t, QA))
    return mixed, akv.reshape(b, t, -1), aki, bkv.reshape(b, t, -1)


def _odd_layer(h, caches, layer, weights, rel_tab):
    w_q, w_ckv, w_dkv, w_g = weights
    b, t, d = h.shape
    h2 = h.reshape(b * t, d)
    (qcat,) = matmul(h2, w_q, (BF16,))
    ckv, ckv_b = matmul(h2, w_ckv, (F32, BF16))
    dkv, dkv_b = matmul(h2, w_dkv, (F32, BF16))
    (gcat,) = matmul(h2, w_g, (F32,))
    qcat, ckv_b, dkv_b, gcat = (x.reshape(b, t, -1) for x in (qcat, ckv_b, dkv_b, gcat))
    win = BAND_CHUNKS * CHUNK
    rel_bits = rel_tab * LOG2E
    if caches is None:
        tq_c = 256
        o_c = stick_prompt(qcat, ckv_b, gcat, _tri_ext(tq_c), tq=tq_c, hb=8)
        o_d = band_prompt(qcat, jnp.pad(dkv_b, ((0, 0), (win, 0), (0, 0))), gcat,
                          band_window_bias(rel_bits, LANES, win), hb=8)
    else:
        cache_c, cache_d = caches
        o_c = stick_sample(qcat, ckv_b, gcat, _tri_ext(TK_SAMPLE), cache_c, layer)
        o_d = band_sample(qcat, dkv_b, gcat, band_window_bias(rel_bits, t, win), cache_d, layer)
    mixed = (o_c.reshape(b * t, QA), o_d.reshape(b * t, QA))
    return mixed, ckv.reshape(b, t, -1), dkv.reshape(b, t, -1)


def _finish_layer(x, mixed, w_out, p_i, g_post, w_proj, g_pl, w_gate):
    mix = matmul_pair(*mixed, w_out)
    x1, x1_b = residual_norm(x, mix, g_post)
    e = embed_norm(p_i.astype(BF16), w_proj, g_pl)
    return gate_matmul(x1_b, w_gate, x1, e)


def kernel(x_prompt, x_sample, p_prompt, p_sample, cache_a_kv, cache_a_kidx, cache_b_kv, cache_c_kv,
           cache_d_kv, norm_pre, norm_post, w_in_even, w_out_even, t5_bias, diff_lambda, diff_subln,
           w_in_odd, w_out_odd, d_rel_bias, w_pl_proj, pl_norm, w_pl_gate):
    bp, tp, d = x_prompt.shape
    bs, ts, _ = x_sample.shape
    depth = norm_pre.shape[0]
    assert d == D_MODEL and ts == CHUNK and tp % 256 == 0
    xp = x_prompt.reshape(bp * tp, d)
    xs = x_sample.reshape(bs * ts, d)
    outs = {name: [] for name in ("a_kv_p", "a_kv_s", "a_ki_p", "a_ki_s", "b_kv_p", "b_kv_s",
                                  "c_kv_p", "c_kv_s", "d_kv_p", "d_kv_s")}
    for i in range(depth):
        j = i // 2
        hp = rmsnorm_cast(xp, norm_pre[i]).reshape(bp, tp, d)
        hs = rmsnorm_cast(xs, norm_pre[i]).reshape(bs, ts, d)
        if i % 2 == 0:
            lam_init = 0.8 - 0.6 * math.exp(-0.3 * i)
            weights = _even_weights(w_in_even[j])
            w_out = w_out_even[j].astype(BF16)
            prm = (weights, t5_bias, diff_lambda[j], diff_subln[j], lam_init)
            mp, akv, aki, bkv = _even_layer(hp, None, j, *prm)
            outs["a_kv_p"].append(akv.reshape(bp, tp, 2, KV_A, HEAD_DIM))
            outs["a_ki_p"].append(aki)
            outs["b_kv_p"].append(bkv.reshape(bp, tp, 2, H_B, DV_B))
            ms, akv, aki, bkv = _even_layer(hs, (cache_a_kv, cache_a_kidx, cache_b_kv), j, *prm)
            outs["a_kv_s"].append(akv.reshape(bs, ts, 2, KV_A, HEAD_DIM))
            outs["a_ki_s"].append(aki)
            outs["b_kv_s"].append(bkv.reshape(bs, ts, 2, H_B, DV_B))
        else:
            weights = _odd_weights(w_in_odd[j])
            w_out = w_out_odd[j].astype(BF16)
            mp, ckv, dkv = _odd_layer(hp, None, j, weights, d_rel_bias[j])
            win_p = min(BAND_CHUNKS * CHUNK, tp)
            outs["c_kv_p"].append(ckv.reshape(bp, tp, 2, H_C, HEAD_DIM))
            outs["d_kv_p"].append(dkv[:, tp - win_p:].reshape(bp, win_p, 2, H_D, HEAD_DIM))
            ms, ckv, dkv = _odd_layer(hs, (cache_c_kv, cache_d_kv), j, weights, d_rel_bias[j])
            outs["c_kv_s"].append(ckv.reshape(bs, ts, 2, H_C, HEAD_DIM))
            full_d = jnp.concatenate([cache_d_kv[j], dkv.reshape(bs, ts, 2, H_D, HEAD_DIM)], axis=1)
            outs["d_kv_s"].append(full_d[:, ts:])
        fin = (norm_post[i], w_pl_proj[i].astype(BF16), pl_norm[i], w_pl_gate[i].astype(BF16))
        xp = _finish_layer(xp, mp, w_out, p_prompt[i].reshape(bp * tp, -1), *fin)
        xs = _finish_layer(xs, ms, w_out, p_sample[i].reshape(bs * ts, -1), *fin)
    st = lambda name: jnp.stack(outs[name])
    return (xp.reshape(bp, tp, d), xs.reshape(bs, ts, d), st("a_kv_p"), st("a_kv_s"), st("a_ki_p"), st("a_ki_s"),
            st("b_kv_p"), st("b_kv_s"), st("c_kv_p"), st("c_kv_s"), st("d_kv_p"), st("d_kv_s"))
```

```python
import functools
import math

import numpy as np
import jax
import jax.numpy as jnp
from jax import lax
from jax.experimental import pallas as pl
from jax.experimental.pallas import tpu as pltpu

F32 = jnp.float32
BF16 = jnp.bfloat16
I32 = jnp.int32

D_MODEL = 4096
CHUNK = 64
CHUNK_SHIFT = 6
HEAD_DIM = 128
H_A = 16
KV_A = 4
GROUP_A = H_A // KV_A
H_IDX = 16
D_IDX = 64
TOPK_MAX = 256
H_B = 8
H_C = 16
H_D = 16
BAND_CHUNKS = 8
REL_CLIP = 128
T5_BUCKETS = 32
T5_MAX_DIST = 128
RMS_EPS = 1e-6
QA = H_A * HEAD_DIM
KVA = KV_A * HEAD_DIM
QI = H_IDX * D_IDX
DV_B = 2 * HEAD_DIM
ATT_SCALE = HEAD_DIM ** -0.5
IDX_SCALE = (D_IDX ** -0.5) * (H_IDX ** -0.5)
LOG2E = math.log2(math.e)
QK_SCALE_BITS = ATT_SCALE * LOG2E

LANES = 128
NEG = -1e30
INT_MIN = -(2 ** 31)
VMEM_LIMIT = 56 * 1024 * 1024
N_BIAS_VARIANTS = 3
TK_SAMPLE = LANES
PT_SAMPLE = 1024
HB_SAMPLE = 8
DSA_TILES_PER_STEP = 4
DSA_CHAINS_ABREAST = 4
DIFF_CHAINS_ABREAST = 4


def _cparams(*sem):
    return pltpu.CompilerParams(dimension_semantics=sem, vmem_limit_bytes=VMEM_LIMIT)


def _sigmoid(x):
    return 1.0 / (1.0 + jnp.exp(-x))


def _silu(x):
    return x * _sigmoid(x)


def _lane_tile(x, n):
    return x if n == 1 else jnp.concatenate([x] * n, axis=1)


def _cols(h, width=HEAD_DIM):
    return slice(width * h, width * (h + 1))


def _rmsnorm_cast_kernel(x_ref, g_ref, o_ref):
    x = x_ref[...]
    ms = jnp.mean(x * x, axis=-1, keepdims=True)
    o_ref[...] = (x * lax.rsqrt(ms + RMS_EPS) * g_ref[...]).astype(o_ref.dtype)


def rmsnorm_cast(x, g, bm=256):
    m, d = x.shape
    bm = min(bm, m)
    return pl.pallas_call(
        _rmsnorm_cast_kernel,
        grid=(m // bm,),
        in_specs=[pl.BlockSpec((bm, d), lambda i: (i, 0)), pl.BlockSpec((1, d), lambda i: (0, 0))],
        out_specs=pl.BlockSpec((bm, d), lambda i: (i, 0)),
        out_shape=jax.ShapeDtypeStruct((m, d), BF16),
        compiler_params=_cparams("parallel"),
        name="rmsnorm_cast",
    )(x, g.reshape(1, d))


def _residual_norm_kernel(x_ref, mix_ref, g_ref, o_ref, ob_ref):
    mix = mix_ref[...]
    ms = jnp.mean(mix * mix, axis=-1, keepdims=True)
    x1 = x_ref[...] + mix * lax.rsqrt(ms + RMS_EPS) * g_ref[...]
    o_ref[...] = x1
    ob_ref[...] = x1.astype(BF16)


def residual_norm(x, mix, g, bm=256):
    m, d = x.shape
    bm = min(bm, m)
    row = pl.BlockSpec((bm, d), lambda i: (i, 0))
    return pl.pallas_call(
        _residual_norm_kernel,
        grid=(m // bm,),
        in_specs=[row, row, pl.BlockSpec((1, d), lambda i: (0, 0))],
        out_specs=[row, row],
        out_shape=[jax.ShapeDtypeStruct((m, d), F32), jax.ShapeDtypeStruct((m, d), BF16)],
        compiler_params=_cparams("parallel"),
        name="residual_norm",
    )(x, mix, g.reshape(1, d))


def _embed_norm_kernel(p_ref, w_ref, g_ref, o_ref):
    y = jnp.dot(p_ref[...], w_ref[...], preferred_element_type=F32)
    ms = jnp.mean(y * y, axis=-1, keepdims=True)
    o_ref[...] = y * lax.rsqrt(ms + RMS_EPS) * g_ref[...]


def embed_norm(p, w, g, bm=256):
    m, k = p.shape
    d = w.shape[1]
    bm = min(bm, m)
    return pl.pallas_call(
        _embed_norm_kernel,
        grid=(m // bm,),
        in_specs=[pl.BlockSpec((bm, k), lambda i: (i, 0)), pl.BlockSpec((k, d), lambda i: (0, 0)),
                  pl.BlockSpec((1, d), lambda i: (0, 0))],
        out_specs=pl.BlockSpec((bm, d), lambda i: (i, 0)),
        out_shape=jax.ShapeDtypeStruct((m, d), F32),
        compiler_params=_cparams("parallel"),
        name="embed_norm",
    )(p, w, g.reshape(1, d))


def _mm_kernel(a_ref, w_ref, *o_refs):
    acc = jnp.dot(a_ref[...], w_ref[...], preferred_element_type=F32)
    for o_ref in o_refs:
        o_ref[...] = acc.astype(o_ref.dtype)


def matmul(a, w, out_dtypes, bm=1024, bn=1024):
    m, k = a.shape
    n = w.shape[1]
    bm, bn = min(bm, m), min(bn, n)
    o_spec = pl.BlockSpec((bm, bn), lambda i, j: (i, j))
    outs = pl.pallas_call(
        _mm_kernel,
        grid=(m // bm, n // bn),
        in_specs=[pl.BlockSpec((bm, k), lambda i, j: (i, 0)), pl.BlockSpec((k, bn), lambda i, j: (0, j))],
        out_specs=[o_spec] * len(out_dtypes),
        out_shape=[jax.ShapeDtypeStruct((m, n), dt) for dt in out_dtypes],
        compiler_params=_cparams("parallel", "parallel"),
        name="matmul",
    )(a, w)
    return outs


def _mm_pair_kernel(a1_ref, a2_ref, w1_ref, w2_ref, o_ref):
    o_ref[...] = (jnp.dot(a1_ref[...], w1_ref[...], preferred_element_type=F32)
                  + jnp.dot(a2_ref[...], w2_ref[...], preferred_element_type=F32))


def matmul_pair(a1, a2, w, bm=1024, bn=1024):
    m, k = a1.shape
    n = w.shape[1]
    assert a2.shape == (m, k) and w.shape[0] == 2 * k
    bm, bn = min(bm, m), min(bn, n)
    return pl.pallas_call(
        _mm_pair_kernel,
        grid=(m // bm, n // bn),
        in_specs=[pl.BlockSpec((bm, k), lambda i, j: (i, 0)), pl.BlockSpec((bm, k), lambda i, j: (i, 0)),
                  pl.BlockSpec((k, bn), lambda i, j: (0, j)), pl.BlockSpec((k, bn), lambda i, j: (1, j))],
        out_specs=pl.BlockSpec((bm, bn), lambda i, j: (i, j)),
        out_shape=jax.ShapeDtypeStruct((m, n), F32),
        compiler_params=_cparams("parallel", "parallel"),
        name="matmul_pair",
    )(a1, a2, w, w)


def _gate_mm_kernel(a_ref, w_ref, x_ref, e_ref, o_ref):
    logits = jnp.dot(a_ref[...], w_ref[...], preferred_element_type=F32)
    o_ref[...] = x_ref[...] + e_ref[...] * _sigmoid(logits)


def gate_matmul(a, w, x, e, bm=1024, bn=512):
    m, k = a.shape
    n = w.shape[1]
    bm, bn = min(bm, m), min(bn, n)
    tile = pl.BlockSpec((bm, bn), lambda i, j: (i, j))
    return pl.pallas_call(
        _gate_mm_kernel,
        grid=(m // bm, n // bn),
        in_specs=[pl.BlockSpec((bm, k), lambda i, j: (i, 0)), pl.BlockSpec((k, bn), lambda i, j: (0, j)),
                  tile, tile],
        out_specs=tile,
        out_shape=jax.ShapeDtypeStruct((m, n), F32),
        compiler_params=_cparams("parallel", "parallel"),
        name="gate_matmul",
    )(a, w, x, e)


def _resident(block_shape, index_map):
    return pl.BlockSpec(block_shape, index_map, pipeline_mode=pl.Buffered(1))


def _qk(q, kmat):
    return lax.dot_general(q, kmat, (((1,), (1,)), ((), ())), preferred_element_type=F32)


def _softmax_tiles(chains, mxu_rowsum=True, width=None):
    width = width or len(chains)
    if width < len(chains):
        return [r for i in range(0, len(chains), width) for r in _softmax_tiles(chains[i:i + width], mxu_rowsum)]
    chains = [(fn(), vmat, m_prev, acc_prev) for fn, vmat, m_prev, acc_prev in chains]
    tk = chains[0][0].shape[1]
    m_new = [jnp.maximum(m_prev, jnp.max(s, axis=1, keepdims=True)) for s, _, m_prev, _ in chains]
    alpha = [jnp.exp2(m_prev - m) for (_, _, m_prev, _), m in zip(chains, m_new)]
    p = [jnp.exp2(s - _lane_tile(m, tk // LANES)) for (s, _, _, _), m in zip(chains, m_new)]
    if mxu_rowsum:
        ones = jnp.ones((tk, LANES), BF16)
        pv = [jnp.dot(pi.astype(BF16), jnp.concatenate([vmat, ones], axis=1), preferred_element_type=F32)
              for pi, (_, vmat, _, _) in zip(p, chains)]
    else:
        rows = [jnp.broadcast_to(jnp.sum(pi, axis=1, keepdims=True), m.shape) for pi, m in zip(p, m_new)]
        pv = [jnp.concatenate([jnp.dot(pi.astype(BF16), vmat, preferred_element_type=F32), row], axis=1)
              for pi, (_, vmat, _, _), row in zip(p, chains, rows)]
    acc = [_lane_tile(a, x.shape[1] // LANES) * acc_prev + x for a, x, (_, _, _, acc_prev) in zip(alpha, pv, chains)]
    return list(zip(m_new, acc))


def _softmax_finish(acc):
    dv = acc.shape[1] - LANES
    return acc[:, :dv] / _lane_tile(acc[:, dv:], dv // LANES)


def _stick_tiles(chains, tri, before):
    tk = chains[0][1].shape[0]
    z = [_qk(q, kmat) * ATT_SCALE for q, kmat, _, _ in chains]
    softplus = [jnp.maximum(x, 0.0) + jnp.log(1.0 + jnp.exp(-jnp.abs(x))) for x in z]
    if before is None:
        sums = [jnp.dot(sp.astype(BF16), tri, preferred_element_type=F32) for sp in softplus]
    else:
        masked = [jnp.where(before, sp, 0.0) for sp in softplus]
        hi = [x.astype(BF16) for x in masked]
        lo = [(x - h.astype(F32)).astype(BF16) for x, h in zip(masked, hi)]
        sums = [jnp.dot(jnp.concatenate([h, l], axis=1), tri, preferred_element_type=F32) for h, l in zip(hi, lo)]
    w = [jnp.exp((x - sp) + s[:, :tk] + _lane_tile(carry, tk // LANES))
         for x, sp, s, (_, _, _, carry) in zip(z, softplus, sums, chains)]
    if before is not None:
        w = [jnp.where(before, x, 0.0) for x in w]
    pv = [jnp.dot(x.astype(BF16), vmat, preferred_element_type=F32) for x, (_, _, vmat, _) in zip(w, chains)]
    return [(p, carry + s[:, tk:]) for p, s, (_, _, _, carry) in zip(pv, sums, chains)]


def _diff_finish(acc0, acc1, lam_ref, subln_ref, gate, lam_init):
    lam_q1, lam_k1, lam_q2, lam_k2 = (lam_ref[r:r + 1, :] for r in range(4))
    lam = (jnp.exp(jnp.sum(lam_q1 * lam_k1, axis=1, keepdims=True))
           - jnp.exp(jnp.sum(lam_q2 * lam_k2, axis=1, keepdims=True)) + lam_init)
    o = _softmax_finish(acc0) - lam * _softmax_finish(acc1)
    ms = jnp.mean(o * o, axis=-1, keepdims=True)
    y = o * lax.rsqrt(ms + RMS_EPS) * subln_ref[...] * (1.0 - lam_init)
    return y * _silu(gate)


def _pad_new_tile(x, tk):
    return jnp.concatenate([x, jnp.zeros((tk - x.shape[0], x.shape[1]), x.dtype)], axis=0)


def _dsa_scores_to_keys(qih_ref, wb_ref, ki_rows, visible):
    tq = wb_ref.shape[1]
    n_keys = ki_rows.shape[0]
    s_all = _qk(qih_ref[...], ki_rows)
    acc = jnp.zeros((tq, n_keys), F32)
    for h in range(H_IDX):
        acc = acc + jnp.maximum(s_all[h * tq:(h + 1) * tq], 0.0) * _lane_tile(wb_ref[h], n_keys // LANES)
    bits = pltpu.bitcast(acc, I32)
    key = bits ^ ((bits >> 31) & 0x7FFFFFFF)
    return jnp.where(visible, key, INT_MIN)


def _dsa_prepare(q_ref, qi_ref, w_ref, qs_ref, qih_ref, wb_ref, tq):
    wv = w_ref[:, D_IDX:D_IDX + H_IDX] * IDX_SCALE
    for h in range(H_IDX):
        qih_ref[h * tq:(h + 1) * tq, :] = qi_ref[:, _cols(h, D_IDX)]
        wb_ref[h] = jnp.broadcast_to(wv[:, h:h + 1], (tq, LANES))
    for n in range(KV_A):
        for g in range(GROUP_A):
            qs_ref[n, g * tq:(g + 1) * tq, :] = q_ref[:, _cols(GROUP_A * n + g)]


def _dsa_threshold(key_ref, npairs, n_sel, idx_bits, tq, tk):
    lane = lax.broadcasted_iota(I32, (tq, tk), 1)

    def count_where(pred):
        def body(p, cnt):
            k = 2 * p
            return (cnt + jnp.where(pred(k, key_ref[k]), 1.0, 0.0)
                    + jnp.where(pred(k + 1, key_ref[k + 1]), 1.0, 0.0))
        cnt = lax.fori_loop(0, npairs, body, jnp.zeros((tq, tk), F32))
        return jnp.sum(cnt, axis=1, keepdims=True)

    def count_ge(cand):
        return count_where(lambda k, key: key >= cand)

    def thr_bit(b, t):
        cand = t + lax.shift_left(jnp.int32(1), 31 - b)
        return jnp.where(count_ge(cand) >= n_sel, cand, t)

    thr = lax.fori_loop(0, 32, thr_bit, jnp.full((tq, 1), INT_MIN, I32))
    thr = jnp.maximum(thr, INT_MIN + 1)

    @pl.when(jnp.max(count_ge(thr)) > n_sel)
    def _():
        need = n_sel - count_ge(thr + 1)

        def count_eq_below(c):
            return count_where(lambda k, key: (key == thr) & ((k * tk + lane) < c))

        def idx_bit(b, c):
            cand = c + lax.shift_left(jnp.int32(1), idx_bits - 1 - b)
            return jnp.where(count_eq_below(cand) <= need, cand, c)

        cut = lax.fori_loop(0, idx_bits, idx_bit, jnp.zeros((tq, 1), I32))

        def demote(k, carry):
            key = key_ref[k]
            key_ref[k] = jnp.where((key == thr) & ((k * tk + lane) >= cut), key - 1, key)
            return carry

        lax.fori_loop(0, 2 * npairs, demote, 0)

    return thr


def _dsa_attend_tiles(qs_ref, key_tiles, thr, bias_tiles, kv_of_head, m_ref, acc_ref, tq):
    rows4 = GROUP_A * tq
    madd = jnp.concatenate([jnp.where(key_tile >= thr, 0.0, NEG) for key_tile in key_tiles], axis=1)
    madd4 = jnp.concatenate([madd] * GROUP_A, axis=0)
    loaded = [(qs_ref[n], *kv_of_head(n), m_ref[n], acc_ref[n]) for n in range(KV_A)]
    def score_fn(n, q, kmat):
        def fn():
            bias = jnp.concatenate([b[rows4 * n:rows4 * (n + 1), :] for b in bias_tiles], axis=1)
            return _qk(q, kmat) * QK_SCALE_BITS + bias + madd4
        return fn

    new = _softmax_tiles([(score_fn(n, q, kmat), vmat, m_prev, acc_prev)
                          for n, (q, kmat, vmat, m_prev, acc_prev) in enumerate(loaded)], width=DSA_CHAINS_ABREAST)
    for n, (m_new, acc_new) in enumerate(new):
        m_ref[n], acc_ref[n] = m_new, acc_new


def _dsa_finish(acc_ref, gate_ref, o_ref, tq):
    for n in range(KV_A):
        o = _softmax_finish(acc_ref[n])
        for g in range(GROUP_A):
            cs = _cols(GROUP_A * n + g)
            o_ref[:, cs] = (o[g * tq:(g + 1) * tq] * _silu(gate_ref[:, cs])).astype(o_ref.dtype)


def _dsa_scratch(nk, tq, tk):
    rows4 = GROUP_A * tq
    return [
        pltpu.VMEM((KV_A, rows4, HEAD_DIM), BF16),
        pltpu.VMEM((H_IDX * tq, D_IDX), BF16),
        pltpu.VMEM((H_IDX, tq, LANES), F32),
        pltpu.VMEM((nk, tq, tk), I32),
        pltpu.VMEM((KV_A, rows4, LANES), F32),
        pltpu.VMEM((KV_A, rows4, HEAD_DIM + LANES), F32),
    ]


def _dsa_prompt_kernel(q_ref, qi_ref, w_ref, ki_ref, kv_ref, gate_ref, bias_ref, o_ref,
                       qs_ref, qih_ref, wb_ref, key_ref, m_ref, acc_ref, *, tq, tk, group, n_sel, idx_bits):
    i = pl.program_id(1)
    npairs = i // 2 + 1
    row_chunk = (i * tq + lax.broadcasted_iota(I32, (tq, 2 * tk), 0)) >> CHUNK_SHIFT
    lane = lax.broadcasted_iota(I32, (tq, 2 * tk), 1)
    _dsa_prepare(q_ref, qi_ref, w_ref, qs_ref, qih_ref, wb_ref, tq)

    def score_pair(p, carry):
        ks = pl.ds(pl.multiple_of(p * 2 * tk, 2 * tk), 2 * tk)
        visible = ((p * 2 * tk + lane) >> CHUNK_SHIFT) <= row_chunk
        keys = _dsa_scores_to_keys(qih_ref, wb_ref, ki_ref[ks, :], visible)
        key_ref[2 * p], key_ref[2 * p + 1] = keys[:, :tk], keys[:, tk:]
        return carry

    lax.fori_loop(0, npairs, score_pair, 0)
    for g in range(group - 2):
        key_ref[2 * npairs + g] = jnp.full((tq, tk), INT_MIN, I32)
    thr = _dsa_threshold(key_ref, npairs, n_sel, idx_bits, tq, tk)

    m_ref[...] = jnp.full(m_ref.shape, NEG, F32)
    acc_ref[...] = jnp.zeros(acc_ref.shape, F32)

    def attend_group(j, carry):
        k0 = j * group
        ks = pl.ds(pl.multiple_of(k0 * tk, group * tk), group * tk)
        key_tiles = [key_ref[k0 + g] for g in range(group)]
        bias_tiles = [bias_ref[jnp.clip(i - k0 - g, 0, N_BIAS_VARIANTS - 1)] for g in range(group)]
        kv_of_head = lambda n: (kv_ref[ks, _cols(n)], kv_ref[ks, KVA + HEAD_DIM * n:KVA + HEAD_DIM * (n + 1)])
        _dsa_attend_tiles(qs_ref, key_tiles, thr, bias_tiles, kv_of_head, m_ref, acc_ref, tq)
        return carry

    lax.fori_loop(0, i // group + 1, attend_group, 0)
    _dsa_finish(acc_ref, gate_ref, o_ref, tq)


def dsa_prompt(qcat, kiw, ki, kv, gcat, bias, *, n_sel):
    b, t, _ = qcat.shape
    tq = tk = LANES
    group = DSA_TILES_PER_STEP
    nk = t // tk
    assert nk % group == 0 and group % 2 == 0
    kern = functools.partial(_dsa_prompt_kernel, tq=tq, tk=tk, group=group, n_sel=n_sel,
                             idx_bits=int(t).bit_length())
    return pl.pallas_call(
        kern,
        grid=(b, t // tq),
        in_specs=[
            pl.BlockSpec((None, tq, QA), lambda bi, i: (bi, i, 0)),
            pl.BlockSpec((None, tq, QI), lambda bi, i: (bi, i, 2 * QA // QI)),
            pl.BlockSpec((None, tq, LANES), lambda bi, i: (bi, i, 0)),
            _resident((None, t, D_IDX), lambda bi, i: (bi, 0, 0)),
            _resident((None, t, 2 * KVA), lambda bi, i: (bi, 0, 0)),
            pl.BlockSpec((None, tq, QA), lambda bi, i: (bi, i, 0)),
            _resident((N_BIAS_VARIANTS, H_A * tq, tk), lambda bi, i: (0, 0, 0)),
        ],
        out_specs=pl.BlockSpec((None, tq, QA), lambda bi, i: (bi, i, 0)),
        out_shape=jax.ShapeDtypeStruct((b, t, QA), BF16),
        scratch_shapes=_dsa_scratch(nk + group, tq, tk),
        compiler_params=_cparams("parallel", "arbitrary"),
        name="dsa_prompt",
    )(qcat, qcat, kiw, ki, kv, gcat, bias)


def _dsa_sample_kernel(q_ref, qi_ref, w_ref, kin_ref, kvn_ref, gate_ref, bias_ref, kic_ref, kvc_ref, o_ref,
                       qs_ref, qih_ref, wb_ref, key_ref, m_ref, acc_ref, thr_ref, *, tq, p0, pt, n_sel, idx_bits):
    kt = pl.program_id(1)
    tk = TK_SAMPLE
    n_cache = p0 // tk
    sub = pt // tk

    @pl.when(kt == 0)
    def _():
        new_visible = lax.broadcasted_iota(I32, (tq, tk), 1) < tq
        _dsa_prepare(q_ref, qi_ref, w_ref, qs_ref, qih_ref, wb_ref, tq)

        def score_pair(p, carry):
            ks = pl.ds(pl.multiple_of(p * 2 * tk, 2 * tk), 2 * tk)
            keys = _dsa_scores_to_keys(qih_ref, wb_ref, kic_ref[ks, :].astype(BF16), True)
            key_ref[2 * p], key_ref[2 * p + 1] = keys[:, :tk], keys[:, tk:]
            return carry

        lax.fori_loop(0, n_cache // 2, score_pair, 0)
        ki_new = _pad_new_tile(kin_ref[:, :D_IDX].astype(BF16), tk)
        key_ref[n_cache] = _dsa_scores_to_keys(qih_ref, wb_ref, ki_new, new_visible)
        key_ref[n_cache + 1] = jnp.full((tq, tk), INT_MIN, I32)
        thr = _dsa_threshold(key_ref, n_cache // 2 + 1, n_sel, idx_bits, tq, tk)
        thr_ref[...] = jnp.broadcast_to(thr, thr_ref.shape)
        m_ref[...] = jnp.full(m_ref.shape, NEG, F32)
        acc_ref[...] = jnp.zeros(acc_ref.shape, F32)

    thr = thr_ref[...]
    k0 = kt * sub
    key_tiles = [key_ref[k0 + j] for j in range(sub)]
    bias_tiles = [bias_ref[jnp.minimum(n_cache - k0 - j, N_BIAS_VARIANTS - 1)] for j in range(sub)]
    kv_cached = lambda n: (kvc_ref[:, 0, n, :].astype(BF16), kvc_ref[:, 1, n, :].astype(BF16))
    _dsa_attend_tiles(qs_ref, key_tiles, thr, bias_tiles, kv_cached, m_ref, acc_ref, tq)

    @pl.when(kt == pl.num_programs(1) - 1)
    def _():
        kv_new = lambda n: (_pad_new_tile(kvn_ref[:, _cols(n)], tk),
                            _pad_new_tile(kvn_ref[:, KVA + HEAD_DIM * n:KVA + HEAD_DIM * (n + 1)], tk))
        _dsa_attend_tiles(qs_ref, [key_ref[n_cache]], thr, [bias_ref[0]], kv_new, m_ref, acc_ref, tq)
        _dsa_finish(acc_ref, gate_ref, o_ref, tq)


def dsa_sample(qcat, kiw, kv_new, gcat, bias, cache_kidx, cache_kv, layer, *, n_sel):
    b, tq, _ = qcat.shape
    p0 = cache_kv.shape[2]
    tk, pt = TK_SAMPLE, PT_SAMPLE
    assert p0 % pt == 0 and tq <= tk and (p0 // tk) % 2 == 0
    nk = p0 // tk + 2
    kern = functools.partial(_dsa_sample_kernel, tq=tq, p0=p0, pt=pt, n_sel=n_sel,
                             idx_bits=int(p0 + tk).bit_length())
    return pl.pallas_call(
        kern,
        grid=(b, p0 // pt),
        in_specs=[
            pl.BlockSpec((None, tq, QA), lambda bi, kt: (bi, 0, 0)),
            pl.BlockSpec((None, tq, QI), lambda bi, kt: (bi, 0, 2 * QA // QI)),
            pl.BlockSpec((None, tq, LANES), lambda bi, kt: (bi, 0, 0)),
            pl.BlockSpec((None, tq, LANES), lambda bi, kt: (bi, 0, 0)),
            pl.BlockSpec((None, tq, 2 * KVA), lambda bi, kt: (bi, 0, 0)),
            pl.BlockSpec((None, tq, QA), lambda bi, kt: (bi, 0, 0)),
            pl.BlockSpec((N_BIAS_VARIANTS, H_A * tq, tk), lambda bi, kt: (0, 0, 0)),
            pl.BlockSpec((None, None, p0, D_IDX), lambda bi, kt: (layer, bi, 0, 0)),
            pl.BlockSpec((None, None, pt, 2, KV_A, HEAD_DIM), lambda bi, kt: (layer, bi, kt, 0, 0, 0)),
        ],
        out_specs=pl.BlockSpec((None, tq, QA), lambda bi, kt: (bi, 0, 0)),
        out_shape=jax.ShapeDtypeStruct((b, tq, QA), BF16),
        scratch_shapes=_dsa_scratch(nk, tq, tk) + [pltpu.VMEM((tq, tk), I32)],
        compiler_params=_cparams("parallel", "arbitrary"),
        name="dsa_sample",
    )(qcat, qcat, kiw, kiw, kv_new, gcat, bias, cache_kidx, cache_kv)


def _diff_prompt_kernel(q_ref, k_ref, v_ref, gate_ref, bias_ref, lam_ref, subln_ref, o_ref,
                        m_ref, acc_ref, *, tq, tk, hb, lam_init):
    i = pl.program_id(2)
    row_chunk = (i * tq + lax.broadcasted_iota(I32, (tq, tk), 0)) >> CHUNK_SHIFT
    lane = lax.broadcasted_iota(I32, (tq, tk), 1)
    m_ref[...] = jnp.full(m_ref.shape, NEG, F32)
    acc_ref[...] = jnp.zeros(acc_ref.shape, F32)

    def step(k, masked):
        ks = pl.ds(pl.multiple_of(k * tk, tk), tk)
        variant = jnp.minimum(i - k, N_BIAS_VARIANTS - 1)
        madd = jnp.where(((k * tk + lane) >> CHUNK_SHIFT) <= row_chunk, 0.0, NEG) if masked else None
        loaded = []
        for h in range(hb):
            bias = bias_ref[variant, h]
            if masked:
                bias = bias + madd
            vmat = v_ref[ks, _cols(h, DV_B)]
            for c in range(2):
                col = slice(DV_B * h + HEAD_DIM * c, DV_B * h + HEAD_DIM * (c + 1))
                loaded.append((h, c, q_ref[:, col], k_ref[ks, col], vmat, bias, m_ref[h, c], acc_ref[h, c]))
        score_fn = lambda q, kmat, bias: lambda: _qk(q, kmat) * QK_SCALE_BITS + bias
        new = _softmax_tiles([(score_fn(q, kmat, bias), vmat, m_prev, acc_prev)
                              for _, _, q, kmat, vmat, bias, m_prev, acc_prev in loaded],
                             mxu_rowsum=False, width=DIFF_CHAINS_ABREAST)
        for (h, c, *_), (m_new, acc_new) in zip(loaded, new):
            m_ref[h, c], acc_ref[h, c] = m_new, acc_new

    def full_tile(k, carry):
        step(k, False)
        return carry

    lax.fori_loop(0, i, full_tile, 0)
    step(i, True)
    for h in range(hb):
        cs = _cols(h, DV_B)
        o_ref[:, cs] = _diff_finish(acc_ref[h, 0], acc_ref[h, 1], lam_ref, subln_ref, gate_ref[:, cs],
                                    lam_init).astype(o_ref.dtype)


def diff_prompt(qcat, kv, gcat, bias, lam_vec, subln, *, tq, hb, lam_init):
    b, t, _ = qcat.shape
    tk = tq
    w = hb * DV_B
    kern = functools.partial(_diff_prompt_kernel, tq=tq, tk=tk, hb=hb, lam_init=lam_init)
    return pl.pallas_call(
        kern,
        grid=(b, H_B // hb, t // tq),
        in_specs=[
            pl.BlockSpec((None, tq, w), lambda bi, g, i: (bi, i, QA // w + g)),
            _resident((None, t, w), lambda bi, g, i: (bi, 0, g)),
            _resident((None, t, w), lambda bi, g, i: (bi, 0, QA // w + g)),
            pl.BlockSpec((None, tq, w), lambda bi, g, i: (bi, i, QA // w + g)),
            _resident((N_BIAS_VARIANTS, hb, tq, tk), lambda bi, g, i: (0, g, 0, 0)),
            pl.BlockSpec((4, HEAD_DIM), lambda bi, g, i: (0, 0)),
            pl.BlockSpec((1, DV_B), lambda bi, g, i: (0, 0)),
        ],
        out_specs=pl.BlockSpec((None, tq, w), lambda bi, g, i: (bi, i, g)),
        out_shape=jax.ShapeDtypeStruct((b, t, QA), BF16),
        scratch_shapes=[pltpu.VMEM((hb, 2, tq, LANES), F32), pltpu.VMEM((hb, 2, tq, DV_B + LANES), F32)],
        compiler_params=_cparams("parallel", "parallel", "arbitrary"),
        name="diff_prompt",
    )(qcat, kv, kv, gcat, bias, lam_vec, subln.reshape(1, DV_B))


def _cache_rows_2d(cache):
    n, b, p, two, h, d = cache.shape
    assert two == 2 and h % 8 == 0
    return cache.reshape(n, b, p * two * h, d)


def _head_rows(c_ref, key0, n_keys, n_heads, sel, h):
    period = 2 * n_heads
    return c_ref[pl.ds(key0 * period + sel * n_heads + h, n_keys, stride=period), :]


def _diff_sample_kernel(q_ref, kn_ref, vn_ref, gate_ref, bias_ref, lam_ref, subln_ref, c0_ref, c1_ref, o_ref,
                        m_ref, acc_ref, *, tq, p0, pt, lam_init):
    kt = pl.program_id(1)
    tk = TK_SAMPLE
    n_cache = p0 // tk
    sub = pt // tk

    @pl.when(kt == 0)
    def _():
        m_ref[...] = jnp.full(m_ref.shape, NEG, F32)
        acc_ref[...] = jnp.zeros(acc_ref.shape, F32)

    def update_all(tile_of_head):
        loaded = []
        for h in range(H_B):
            kmat2, vmat, bias = tile_of_head(h)
            for c in range(2):
                q = q_ref[:, DV_B * h + HEAD_DIM * c:DV_B * h + HEAD_DIM * (c + 1)]
                loaded.append((h, c, q, kmat2[:, _cols(c)], vmat, bias, m_ref[h, c], acc_ref[h, c]))
        score_fn = lambda q, kmat, bias: lambda: _qk(q, kmat) * QK_SCALE_BITS + bias
        new = _softmax_tiles([(score_fn(q, kmat, bias), vmat, m_prev, acc_prev)
                              for _, _, q, kmat, vmat, bias, m_prev, acc_prev in loaded], mxu_rowsum=False)
        for (h, c, *_), (m_new, acc_new) in zip(loaded, new):
            m_ref[h, c], acc_ref[h, c] = m_new, acc_new

    def cache_tile(j, carry):
        variant = jnp.minimum(n_cache - (kt * sub + j), N_BIAS_VARIANTS - 1)
        halves = lambda sel, h: jnp.concatenate(
            [_head_rows(c_ref, j * tk, tk, H_B, sel, h).astype(BF16) for c_ref in (c0_ref, c1_ref)], axis=1)
        update_all(lambda h: (halves(0, h), halves(1, h), bias_ref[variant, h]))
        return carry

    lax.fori_loop(0, sub, cache_tile, 0)

    @pl.when(kt == pl.num_programs(1) - 1)
    def _():
        new_madd = jnp.where(lax.broadcasted_iota(I32, (tq, tk), 1) < tq, 0.0, NEG)
        update_all(lambda h: (_pad_new_tile(kn_ref[:, _cols(h, DV_B)], tk), _pad_new_tile(vn_ref[:, _cols(h, DV_B)], tk),
                              bias_ref[0, h] + new_madd))
        for h in range(H_B):
            cs = _cols(h, DV_B)
            o_ref[:, cs] = _diff_finish(acc_ref[h, 0], acc_ref[h, 1], lam_ref, subln_ref, gate_ref[:, cs],
                                        lam_init).astype(o_ref.dtype)


def diff_sample(qcat, kv_new, gcat, bias, lam_vec, subln, cache_kv, layer, *, lam_init):
    b, tq, _ = qcat.shape
    p0 = cache_kv.shape[2]
    tk, pt = TK_SAMPLE, PT_SAMPLE
    assert p0 % pt == 0 and tq <= tk
    kern = functools.partial(_diff_sample_kernel, tq=tq, p0=p0, pt=pt, lam_init=lam_init)
    cache_spec = lambda c: pl.BlockSpec((None, None, pt * 2 * H_B, LANES), lambda bi, kt: (layer, bi, kt, c))
    cache_2d = _cache_rows_2d(cache_kv)
    return pl.pallas_call(
        kern,
        grid=(b, p0 // pt),
        in_specs=[
            pl.BlockSpec((None, tq, QA), lambda bi, kt: (bi, 0, 1)),
            pl.BlockSpec((None, tq, QA), lambda bi, kt: (bi, 0, 0)),
            pl.BlockSpec((None, tq, QA), lambda bi, kt: (bi, 0, 1)),
            pl.BlockSpec((None, tq, QA), lambda bi, kt: (bi, 0, 1)),
            pl.BlockSpec((N_BIAS_VARIANTS, H_B, tq, tk), lambda bi, kt: (0, 0, 0, 0)),
            pl.BlockSpec((4, HEAD_DIM), lambda bi, kt: (0, 0)),
            pl.BlockSpec((1, DV_B), lambda bi, kt: (0, 0)),
            cache_spec(0), cache_spec(1),
        ],
        out_specs=pl.BlockSpec((None, tq, QA), lambda bi, kt: (bi, 0, 0)),
        out_shape=jax.ShapeDtypeStruct((b, tq, QA), BF16),
        scratch_shapes=[pltpu.VMEM((H_B, 2, tq, LANES), F32), pltpu.VMEM((H_B, 2, tq, DV_B + LANES), F32)],
        compiler_params=_cparams("parallel", "arbitrary"),
        name="diff_sample",
    )(qcat, kv_new, kv_new, gcat, bias, lam_vec, subln.reshape(1, DV_B), cache_2d, cache_2d)


def _stick_prompt_kernel(q_ref, k_ref, v_ref, gate_ref, tri_ref, o_ref, acc_ref, carry_ref, *, tq, tk, hb):
    i = pl.program_id(2)
    before = lax.broadcasted_iota(I32, (tq, tk), 1) < lax.broadcasted_iota(I32, (tq, tk), 0)
    acc_ref[...] = jnp.zeros(acc_ref.shape, F32)
    carry_ref[...] = jnp.zeros(carry_ref.shape, F32)

    def step(k, mask):
        ks = pl.ds(pl.multiple_of(k * tk, tk), tk)
        tri = tri_ref[:tk, :] if mask is None else tri_ref[...]
        loaded = [(q_ref[:, _cols(h)], k_ref[ks, _cols(h)], v_ref[ks, _cols(h)], carry_ref[h], acc_ref[h])
                  for h in range(hb)]
        new = _stick_tiles([(q, kmat, vmat, carry_prev) for q, kmat, vmat, carry_prev, _ in loaded], tri, mask)
        for h, ((pv, carry), (_, _, _, _, acc)) in enumerate(zip(new, loaded)):
            acc_ref[h], carry_ref[h] = acc + pv, carry

    step(i, before)

    def full_tile(j, carry):
        step(i - 1 - j, None)
        return carry

    lax.fori_loop(0, i, full_tile, 0)
    for h in range(hb):
        cs = _cols(h)
        o_ref[:, cs] = (acc_ref[h] * _silu(gate_ref[:, cs])).astype(o_ref.dtype)


def stick_prompt(qcat, kv, gcat, tri_ext, *, tq, hb):
    b, t, _ = qcat.shape
    tk = tq
    w = hb * HEAD_DIM
    kern = functools.partial(_stick_prompt_kernel, tq=tq, tk=tk, hb=hb)
    return pl.pallas_call(
        kern,
        grid=(b, H_C // hb, t // tq),
        in_specs=[
            pl.BlockSpec((None, tq, w), lambda bi, g, i: (bi, i, g)),
            _resident((None, t, w), lambda bi, g, i: (bi, 0, g)),
            _resident((None, t, w), lambda bi, g, i: (bi, 0, QA // w + g)),
            pl.BlockSpec((None, tq, w), lambda bi, g, i: (bi, i, g)),
            _resident((2 * tk, tk + LANES), lambda bi, g, i: (0, 0)),
        ],
        out_specs=pl.BlockSpec((None, tq, w), lambda bi, g, i: (bi, i, g)),
        out_shape=jax.ShapeDtypeStruct((b, t, QA), BF16),
        scratch_shapes=[pltpu.VMEM((hb, tq, HEAD_DIM), F32), pltpu.VMEM((hb, tq, LANES), F32)],
        compiler_params=_cparams("parallel", "parallel", "arbitrary"),
        name="stick_prompt",
    )(qcat, kv, kv, gcat, tri_ext)


def _stick_sample_kernel(q_ref, kn_ref, vn_ref, gate_ref, tri_ref, c_ref, o_ref, acc_ref, carry_ref, *, tq, pt):
    kt = pl.program_id(1)
    tk = TK_SAMPLE
    sub = pt // tk
    head_groups = [range(g * HB_SAMPLE, (g + 1) * HB_SAMPLE) for g in range(H_C // HB_SAMPLE)]

    @pl.when(kt == 0)
    def _():
        before = lax.broadcasted_iota(I32, (tq, tk), 1) < lax.broadcasted_iota(I32, (tq, tk), 0)
        tri = tri_ref[...]
        for heads in head_groups:
            new = _stick_tiles([(q_ref[:, _cols(h)], _pad_new_tile(kn_ref[:, _cols(h)], tk),
                                 _pad_new_tile(vn_ref[:, _cols(h)], tk), jnp.zeros((tq, LANES), F32))
                                for h in heads], tri, before)
            for h, (pv, carry) in zip(heads, new):
                acc_ref[h], carry_ref[h] = pv, carry

    def cache_tile(j, carry):
        key0 = (sub - 1 - j) * tk
        tri = tri_ref[:tk, :]
        for heads in head_groups:
            loaded = [(q_ref[:, _cols(h)], _head_rows(c_ref, key0, tk, H_C, 0, h).astype(BF16),
                       _head_rows(c_ref, key0, tk, H_C, 1, h).astype(BF16), carry_ref[h], acc_ref[h]) for h in heads]
            new = _stick_tiles([(q, kmat, vmat, c_prev) for q, kmat, vmat, c_prev, _ in loaded], tri, None)
            for h, (pv, c), (_, _, _, _, acc) in zip(heads, new, loaded):
                acc_ref[h], carry_ref[h] = acc + pv, c
        return carry

    lax.fori_loop(0, sub, cache_tile, 0)

    @pl.when(kt == pl.num_programs(1) - 1)
    def _():
        for h in range(H_C):
            cs = _cols(h)
            o_ref[:, cs] = (acc_ref[h] * _silu(gate_ref[:, cs])).astype(o_ref.dtype)


def stick_sample(qcat, kv_new, gcat, tri_ext, cache_kv, layer):
    b, tq, _ = qcat.shape
    p0 = cache_kv.shape[2]
    pt = PT_SAMPLE
    assert p0 % pt == 0 and tq <= TK_SAMPLE
    n_kt = p0 // pt
    kern = functools.partial(_stick_sample_kernel, tq=tq, pt=pt)
    return pl.pallas_call(
        kern,
        grid=(b, n_kt),
        in_specs=[
            pl.BlockSpec((None, tq, QA), lambda bi, kt: (bi, 0, 0)),
            pl.BlockSpec((None, tq, QA), lambda bi, kt: (bi, 0, 0)),
            pl.BlockSpec((None, tq, QA), lambda bi, kt: (bi, 0, 1)),
            pl.BlockSpec((None, tq, QA), lambda bi, kt: (bi, 0, 0)),
            pl.BlockSpec((2 * TK_SAMPLE, TK_SAMPLE + LANES), lambda bi, kt: (0, 0)),
            pl.BlockSpec((None, None, pt * 2 * H_C, HEAD_DIM), lambda bi, kt: (layer, bi, n_kt - 1 - kt, 0)),
        ],
        out_specs=pl.BlockSpec((None, tq, QA), lambda bi, kt: (bi, 0, 0)),
        out_shape=jax.ShapeDtypeStruct((b, tq, QA), BF16),
        scratch_shapes=[pltpu.VMEM((H_C, tq, HEAD_DIM), F32), pltpu.VMEM((H_C, tq, LANES), F32)],
        compiler_params=_cparams("parallel", "arbitrary"),
        name="stick_sample",
    )(qcat, kv_new, kv_new, gcat, tri_ext, _cache_rows_2d(cache_kv))


def _band_prompt_kernel(q_ref, k_ref, v_ref, gate_ref, bias_ref, o_ref, *, tq, win, hb):
    i = pl.program_id(2)
    span = win + tq
    rows = pl.ds(pl.multiple_of(i * tq, tq), span)
    row_chunk = (i * tq + lax.broadcasted_iota(I32, (tq, span), 0)) >> CHUNK_SHIFT
    key_chunk = (i * tq - win + lax.broadcasted_iota(I32, (tq, span), 1)) >> CHUNK_SHIFT
    visible = (key_chunk <= row_chunk) & (key_chunk >= jnp.maximum(row_chunk - BAND_CHUNKS, 0))
    madd = jnp.where(visible, 0.0, NEG)
    ones = jnp.ones((span, LANES), BF16)
    for h in range(hb):
        cs = _cols(h)
        s = _qk(q_ref[:, cs], k_ref[rows, cs]) * QK_SCALE_BITS + bias_ref[h] + madd
        p = jnp.exp2(s - jnp.max(s, axis=1, keepdims=True))
        pv = jnp.dot(p.astype(BF16), jnp.concatenate([v_ref[rows, cs], ones], axis=1), preferred_element_type=F32)
        o_ref[:, cs] = (_softmax_finish(pv) * _silu(gate_ref[:, cs])).astype(o_ref.dtype)


def band_prompt(qcat, kv, gcat, bias, *, hb):
    b, t, _ = qcat.shape
    tq = LANES
    win = BAND_CHUNKS * CHUNK
    w = hb * HEAD_DIM
    assert kv.shape[1] == win + t and bias.shape == (H_D, tq, win + tq)
    kern = functools.partial(_band_prompt_kernel, tq=tq, win=win, hb=hb)
    return pl.pallas_call(
        kern,
        grid=(b, H_D // hb, t // tq),
        in_specs=[
            pl.BlockSpec((None, tq, w), lambda bi, g, i: (bi, i, QA // w + g)),
            _resident((None, win + t, w), lambda bi, g, i: (bi, 0, g)),
            _resident((None, win + t, w), lambda bi, g, i: (bi, 0, QA // w + g)),
            pl.BlockSpec((None, tq, w), lambda bi, g, i: (bi, i, QA // w + g)),
            _resident((hb, tq, win + tq), lambda bi, g, i: (g, 0, 0)),
        ],
        out_specs=pl.BlockSpec((None, tq, w), lambda bi, g, i: (bi, i, g)),
        out_shape=jax.ShapeDtypeStruct((b, t, QA), BF16),
        compiler_params=_cparams("parallel", "parallel", "arbitrary"),
        name="band_prompt",
    )(qcat, kv, kv, gcat, bias)


def _band_sample_kernel(q_ref, kn_ref, vn_ref, gate_ref, bias_ref, c_ref, o_ref, *, tq, win):
    g = pl.program_id(1)
    tk = TK_SAMPLE
    lane = lax.broadcasted_iota(I32, (tq, win + tk), 1)
    madd = jnp.where(lane < win + tq, 0.0, NEG)
    ones = jnp.ones((win + tk, LANES), BF16)
    for h in range(HB_SAMPLE):
        cs = _cols(h)
        head = g * HB_SAMPLE + h
        kmat = jnp.concatenate([_head_rows(c_ref, 0, win, H_D, 0, head).astype(BF16),
                                _pad_new_tile(kn_ref[:, cs], tk)], axis=0)
        vmat = jnp.concatenate([_head_rows(c_ref, 0, win, H_D, 1, head).astype(BF16),
                                _pad_new_tile(vn_ref[:, cs], tk)], axis=0)
        s = _qk(q_ref[:, cs], kmat) * QK_SCALE_BITS + bias_ref[h] + madd
        p = jnp.exp2(s - jnp.max(s, axis=1, keepdims=True))
        pv = jnp.dot(p.astype(BF16), jnp.concatenate([vmat, ones], axis=1), preferred_element_type=F32)
        o_ref[:, cs] = (_softmax_finish(pv) * _silu(gate_ref[:, cs])).astype(o_ref.dtype)


def band_sample(qcat, kv_new, gcat, bias, cache_kv, layer):
    b, tq, _ = qcat.shape
    win = cache_kv.shape[2]
    w = HB_SAMPLE * HEAD_DIM
    assert win == BAND_CHUNKS * CHUNK and tq == CHUNK and bias.shape == (H_D, tq, win + TK_SAMPLE)
    kern = functools.partial(_band_sample_kernel, tq=tq, win=win)
    return pl.pallas_call(
        kern,
        grid=(b, H_D // HB_SAMPLE),
        in_specs=[
            pl.BlockSpec((None, tq, w), lambda bi, g: (bi, 0, QA // w + g)),
            pl.BlockSpec((None, tq, w), lambda bi, g: (bi, 0, g)),
            pl.BlockSpec((None, tq, w), lambda bi, g: (bi, 0, QA // w + g)),
            pl.BlockSpec((None, tq, w), lambda bi, g: (bi, 0, QA // w + g)),
            pl.BlockSpec((HB_SAMPLE, tq, win + TK_SAMPLE), lambda bi, g: (g, 0, 0)),
            pl.BlockSpec((None, None, win * 2 * H_D, HEAD_DIM), lambda bi, g: (layer, bi, 0, 0)),
        ],
        out_specs=pl.BlockSpec((None, tq, w), lambda bi, g: (bi, 0, g)),
        out_shape=jax.ShapeDtypeStruct((b, tq, QA), BF16),
        compiler_params=_cparams("parallel", "arbitrary"),
        name="band_sample",
    )(qcat, kv_new, kv_new, gcat, bias, _cache_rows_2d(cache_kv))


def _t5_bucket_np(rel):
    half = T5_BUCKETS // 2
    max_exact = half // 2
    n = np.abs(rel)
    nf = np.maximum(n, 1).astype(np.float64)
    large = max_exact + (np.log(nf / max_exact) / math.log(T5_MAX_DIST / max_exact)
                         * (half - max_exact)).astype(np.int32)
    large = np.minimum(large, half - 1)
    return np.where(rel < 0, half, 0) + np.where(n < max_exact, n, large)


def _band_index_np(rel):
    return np.clip(rel, -REL_CLIP, REL_CLIP) + REL_CLIP


def toeplitz_bias(table, index_of_rel, r0, n_rows, n_cols):
    rel = r0 + np.arange(n_rows)[:, None] - np.arange(n_cols)[None, :]
    idx = jnp.asarray(index_of_rel(rel).reshape(-1).astype(np.int32))
    onehot = (idx[None, :] == jnp.arange(table.shape[0], dtype=I32)[:, None]).astype(F32)
    out = jnp.einsum("bh,bn->hn", table, onehot, precision=lax.Precision.HIGHEST)
    return out.reshape(table.shape[1], n_rows, n_cols)


def bias_tiles(table, index_of_rel, tq, tk):
    far = (N_BIAS_VARIANTS - 1) * tk
    assert (index_of_rel(np.arange(far - tk + 1, far + tq)) == index_of_rel(np.array(far + tq))).all()
    return jnp.stack([toeplitz_bias(table, index_of_rel, v * tk, tq, tk) for v in range(N_BIAS_VARIANTS)], axis=0)


def band_window_bias(rel_tab, tq, win):
    far_cols = win - REL_CLIP
    return jnp.concatenate(
        [jnp.broadcast_to(rel_tab[2 * REL_CLIP][:, None, None], (H_D, tq, far_cols)),
         toeplitz_bias(rel_tab, _band_index_np, REL_CLIP, tq, win + LANES - far_cols)], axis=-1)


def _tri_ext(tk):
    tri = -np.tril(np.ones((tk, tk), np.float32), -1)
    ext = np.concatenate([tri, -np.ones((tk, LANES), np.float32)], axis=1)
    return jnp.asarray(np.concatenate([ext, ext], axis=0), BF16)


def _even_weights(w_in):
    c = np.cumsum((0, QA, KVA, KVA, QI, D_IDX, H_IDX, QA, QA, QA, QA, QA)).tolist()
    aq, ak, av, aqi, aki, aw, ag, bq, bk, bv, bg = (w_in[:, c[r]:c[r + 1]] for r in range(11))
    pad = jnp.zeros((w_in.shape[0], LANES - D_IDX - H_IDX), w_in.dtype)
    cat = lambda *xs: jnp.concatenate(xs, axis=1).astype(BF16)
    return cat(aq, bq, aqi), cat(ak, av), cat(bk, bv), cat(aki, aw, pad), cat(ag, bg)


def _odd_weights(w_in):
    cq, ck, cv, cg, dq, dk, dv, dg = (w_in[:, QA * r:QA * (r + 1)] for r in range(8))
    cat = lambda *xs: jnp.concatenate(xs, axis=1).astype(BF16)
    return cat(cq, dq), cat(ck, cv), cat(dk, dv), cat(cg, dg)


def _even_layer(h, caches, layer, weights, t5_tab, lam_vec, subln, lam_init):
    w_q, w_akv, w_bkv, w_kiw, w_g = weights
    b, t, d = h.shape
    h2 = h.reshape(b * t, d)
    (qcat,) = matmul(h2, w_q, (BF16,))
    akv, akv_b = matmul(h2, w_akv, (F32, BF16))
    bkv, bkv_b = matmul(h2, w_bkv, (F32, BF16))
    (kiw,) = matmul(h2, w_kiw, (F32,))
    (gcat,) = matmul(h2, w_g, (F32,))
    qcat, akv_b, bkv_b, kiw, gcat = (x.reshape(b, t, -1) for x in (qcat, akv_b, bkv_b, kiw, gcat))
    aki = kiw[..., :D_IDX]
    t5_a, t5_b = t5_tab[:, :H_A] * LOG2E, t5_tab[:, H_A:] * LOG2E
    if caches is None:
        tq_b = 256
        bias_a = bias_tiles(t5_a, _t5_bucket_np, LANES, LANES).reshape(N_BIAS_VARIANTS, H_A * LANES, LANES)
        o_a = dsa_prompt(qcat, kiw, aki.astype(BF16), akv_b, gcat, bias_a, n_sel=min(TOPK_MAX, t // 4))
        o_b = diff_prompt(qcat, bkv_b, gcat, bias_tiles(t5_b, _t5_bucket_np, tq_b, tq_b), lam_vec, subln,
                          tq=tq_b, hb=4, lam_init=lam_init)
    else:
        cache_a, cache_ki, cache_b = caches
        p0 = cache_a.shape[2]
        bias_a = bias_tiles(t5_a, _t5_bucket_np, t, TK_SAMPLE).reshape(N_BIAS_VARIANTS, H_A * t, TK_SAMPLE)
        o_a = dsa_sample(qcat, kiw, akv_b, gcat, bias_a, cache_ki, cache_a, layer,
                         n_sel=min(TOPK_MAX, (p0 + t) // 4))
        o_b = diff_sample(qcat, bkv_b, gcat, bias_tiles(t5_b, _t5_bucket_np, t, TK_SAMPLE), lam_vec, subln,
                          cache_b, layer, lam_init=lam_init)
    mixed = (o_a.reshape(b * t, QA), o_b.reshape(b * t, QA))
    return mixed, akv.reshape(b, t, -1), aki, bkv.reshape(b, t, -1)


def _odd_layer(h, caches, layer, weights, rel_tab):
    w_q, w_ckv, w_dkv, w_g = weights
    b, t, d = h.shape
    h2 = h.reshape(b * t, d)
    (qcat,) = matmul(h2, w_q, (BF16,))
    ckv, ckv_b = matmul(h2, w_ckv, (F32, BF16))
    dkv, dkv_b = matmul(h2, w_dkv, (F32, BF16))
    (gcat,) = matmul(h2, w_g, (F32,))
    qcat, ckv_b, dkv_b, gcat = (x.reshape(b, t, -1) for x in (qcat, ckv_b, dkv_b, gcat))
    win = BAND_CHUNKS * CHUNK
    rel_bits = rel_tab * LOG2E
    if caches is None:
        tq_c = 256
        o_c = stick_prompt(qcat, ckv_b, gcat, _tri_ext(tq_c), tq=tq_c, hb=8)
        o_d = band_prompt(qcat, jnp.pad(dkv_b, ((0, 0), (win, 0), (0, 0))), gcat,
                          band_window_bias(rel_bits, LANES, win), hb=8)
    else:
        cache_c, cache_d = caches
        o_c = stick_sample(qcat, ckv_b, gcat, _tri_ext(TK_SAMPLE), cache_c, layer)
        o_d = band_sample(qcat, dkv_b, gcat, band_window_bias(rel_bits, t, win), cache_d, layer)
    mixed = (o_c.reshape(b * t, QA), o_d.reshape(b * t, QA))
    return mixed, ckv.reshape(b, t, -1), dkv.reshape(b, t, -1)


def _finish_layer(x, mixed, w_out, p_i, g_post, w_proj, g_pl, w_gate):
    mix = matmul_pair(*mixed, w_out)
    x1, x1_b = residual_norm(x, mix, g_post)
    e = embed_norm(p_i.astype(BF16), w_proj, g_pl)
    return gate_matmul(x1_b, w_gate, x1, e)


def kernel(x_prompt, x_sample, p_prompt, p_sample, cache_a_kv, cache_a_kidx, cache_b_kv, cache_c_kv,
           cache_d_kv, norm_pre, norm_post, w_in_even, w_out_even, t5_bias, diff_lambda, diff_subln,
           w_in_odd, w_out_odd, d_rel_bias, w_pl_proj, pl_norm, w_pl_gate):
    bp, tp, d = x_prompt.shape
    bs, ts, _ = x_sample.shape
    depth = norm_pre.shape[0]
    assert d == D_MODEL and ts == CHUNK and tp % 256 == 0
    xp = x_prompt.reshape(bp * tp, d)
    xs = x_sample.reshape(bs * ts, d)
    outs = {name: [] for name in ("a_kv_p", "a_kv_s", "a_ki_p", "a_ki_s", "b_kv_p", "b_kv_s",
                                  "c_kv_p", "c_kv_s", "d_kv_p", "d_kv_s")}
    for i in range(depth):
        j = i // 2
        hp = rmsnorm_cast(xp, norm_pre[i]).reshape(bp, tp, d)
        hs = rmsnorm_cast(xs, norm_pre[i]).reshape(bs, ts, d)
        if i % 2 == 0:
            lam_init = 0.8 - 0.6 * math.exp(-0.3 * i)
            weights = _even_weights(w_in_even[j])
            w_out = w_out_even[j].astype(BF16)
            prm = (weights, t5_bias, diff_lambda[j], diff_subln[j], lam_init)
            mp, akv, aki, bkv = _even_layer(hp, None, j, *prm)
            outs["a_kv_p"].append(akv.reshape(bp, tp, 2, KV_A, HEAD_DIM))
            outs["a_ki_p"].append(aki)
            outs["b_kv_p"].append(bkv.reshape(bp, tp, 2, H_B, DV_B))
            ms, akv, aki, bkv = _even_layer(hs, (cache_a_kv, cache_a_kidx, cache_b_kv), j, *prm)
            outs["a_kv_s"].append(akv.reshape(bs, ts, 2, KV_A, HEAD_DIM))
            outs["a_ki_s"].append(aki)
            outs["b_kv_s"].append(bkv.reshape(bs, ts, 2, H_B, DV_B))
        else:
            weights = _odd_weights(w_in_odd[j])
            w_out = w_out_odd[j].astype(BF16)
            mp, ckv, dkv = _odd_layer(hp, None, j, weights, d_rel_bias[j])
            win_p = min(BAND_CHUNKS * CHUNK, tp)
            outs["c_kv_p"].append(ckv.reshape(bp, tp, 2, H_C, HEAD_DIM))
            outs["d_kv_p"].append(dkv[:, tp - win_p:].reshape(bp, win_p, 2, H_D, HEAD_DIM))
            ms, ckv, dkv = _odd_layer(hs, (cache_c_kv, cache_d_kv), j, weights, d_rel_bias[j])
            outs["c_kv_s"].append(ckv.reshape(bs, ts, 2, H_C, HEAD_DIM))
            full_d = jnp.concatenate([cache_d_kv[j], dkv.reshape(bs, ts, 2, H_D, HEAD_DIM)], axis=1)
            outs["d_kv_s"].append(full_d[:, ts:])
        fin = (norm_post[i], w_pl_proj[i].astype(BF16), pl_norm[i], w_pl_gate[i].astype(BF16))
        xp = _finish_layer(xp, mp, w_out, p_prompt[i].reshape(bp * tp, -1), *fin)
        xs = _finish_layer(xs, ms, w_out, p_sample[i].reshape(bs * ts, -1), *fin)
    st = lambda name: jnp.stack(outs[name])
    return (xp.reshape(bp, tp, d), xs.reshape(bs, ts, d), st("a_kv_p"), st("a_kv_s"), st("a_ki_p"), st("a_ki_s"),
            st("b_kv_p"), st("b_kv_s"), st("c_kv_p"), st("c_kv_s"), st("d_kv_p"), st("d_kv_s"))
```
